```python
import math
import jax, jax.numpy as jnp
from jax import lax
import numpy as np

D_MODEL = 1024
BATCH = 16
SEQ = 2048
DEPTH = 2

HEAD_DIM = 64
GRID_W = 64
NA_HEADS = 4
NA_ROWS_MAX = 8
NA_COLS = 16
SWA_Q_HEADS = 6
SWA_KV_HEADS = 2
SWA_GROUP = SWA_Q_HEADS // SWA_KV_HEADS
SWA_HALF_WINDOW = 128
DIL_HEADS = 6
DIL_BRANCHES = ((128, 1), (512, 4), (2048, 16))
ATTN_BLOCK = 128
NA_W = NA_HEADS * HEAD_DIM
SWA_QW = SWA_Q_HEADS * HEAD_DIM
SWA_KVW = SWA_KV_HEADS * HEAD_DIM
DIL_W = DIL_HEADS * HEAD_DIM
IN_W = 3 * NA_W + SWA_QW + 2 * SWA_KVW + 3 * DIL_W
MIX_W = NA_W + SWA_QW + DIL_W
FFN_HIDDEN = -(-8 * D_MODEL // (3 * 256)) * 256
NORM_EPS = 1e-6
NEG_INF = -1e30

kernel_name = "hymba_style_natten_swa_dilated_encoder"


def rms_norm(x, g):
    xf = x.astype(jnp.float32)
    y = xf * lax.rsqrt(jnp.mean(xf * xf, axis=-1, keepdims=True) + NORM_EPS)
    return (y * g.astype(jnp.float32)).astype(x.dtype)


def alibi_slopes(n):
    return 2.0 ** (-8.0 * (jnp.arange(n, dtype=jnp.float32) + 1.0) / n)


def split_heads(t, n):
    b, s, _ = t.shape
    return t.reshape(b, s, n, HEAD_DIM).transpose(0, 2, 1, 3)


def merge_heads(t):
    b, h, s, d = t.shape
    return t.transpose(0, 2, 1, 3).reshape(b, s, h * d)


def neighbourhood_attention(q, k, v, rpb):
    b, h, s, hd = q.shape
    rows = s // GRID_W
    kr = min(NA_ROWS_MAX, rows)
    qg = q.reshape(b, h, rows, GRID_W, hd)
    kg = k.reshape(b, h, rows, GRID_W, hd)
    vg = v.reshape(b, h, rows, GRID_W, hd)
    r = jnp.arange(rows)
    row_start = jnp.clip(r - kr // 2, 0, rows - kr)
    row_idx = row_start[:, None] + jnp.arange(kr)[None, :]
    kb = kg[:, :, row_idx]
    vb = vg[:, :, row_idx]
    sc = jnp.einsum("bhrqd,bhrkcd->bhrqkc", qg, kb).astype(jnp.float32) * (hd ** -0.5)
    c = jnp.arange(GRID_W)
    col_start = jnp.clip(c - NA_COLS // 2, 0, GRID_W - NA_COLS)
    col_valid = (c[None, :] >= col_start[:, None]) & (c[None, :] < col_start[:, None] + NA_COLS)
    col_idx = jnp.clip(c[None, :] - c[:, None] + NA_COLS - 1, 0, 2 * NA_COLS - 2)
    row_off = row_idx - r[:, None]
    bias = rpb[:, row_off + NA_ROWS_MAX - 1]
    bias = bias[..., col_idx].transpose(0, 1, 3, 2, 4)
    sc = jnp.where(col_valid[:, None, :], sc + bias.astype(jnp.float32), NEG_INF)
    p = jax.nn.softmax(sc.reshape(b, h, rows, GRID_W, kr * GRID_W), axis=-1)
    p = p.reshape(b, h, rows, GRID_W, kr, GRID_W).astype(v.dtype)
    o = jnp.einsum("bhrqkc,bhrkcd->bhrqd", p, vb)
    return o.reshape(b, h, s, hd)


def banded_attention(q, k, v, half_w, slopes, dilation, sink=None):
    b, g, rq, L, hd = q.shape
    bq = math.gcd(L, ATTN_BLOCK)
    nb = L // bq
    kw = bq + 2 * half_w
    pad = ((0, 0), (0, 0), (half_w, half_w), (0, 0))
    kp = jnp.pad(k, pad)
    vp = jnp.pad(v, pad)
    idx = jnp.arange(nb)[:, None] * bq + jnp.arange(kw)[None, :]
    kb = kp[:, :, idx]
    vb = vp[:, :, idx]
    qb = q.reshape(b, g, rq, nb, bq, hd)
    sc = jnp.einsum("bgrnqd,bgnkd->bgrnqk", qb, kb).astype(jnp.float32) * (hd ** -0.5)
    qpos = jnp.arange(nb)[:, None] * bq + jnp.arange(bq)[None, :]
    kpos = (idx - half_w)[:, None, :]
    dist = jnp.abs(kpos - qpos[:, :, None])
    valid = (dist <= half_w) & (kpos >= 0) & (kpos < L)
    bias = -slopes.astype(jnp.float32)[:, :, None, None, None] * (dist * dilation).astype(jnp.float32)
    sc = jnp.where(valid, sc + bias, NEG_INF)
    m = jnp.max(sc, axis=-1, keepdims=True)
    if sink is not None:
        sink_f = sink.astype(jnp.float32)[:, :, None, None, None]
        m = jnp.maximum(m, sink_f)
    p = jnp.exp(sc - m)
    den = jnp.sum(p, axis=-1, keepdims=True)
    if sink is not None:
        den = den + jnp.exp(sink_f - m)
    o = jnp.einsum("bgrnqk,bgnkd->bgrnqd", p.astype(v.dtype), vb).astype(jnp.float32) / den
    lse = (m + jnp.log(den))[..., 0]
    return o.astype(v.dtype).reshape(b, g, rq, L, hd), lse.reshape(b, g, rq, L)


def dilated_attention(q, k, v, slopes):
    b, h, s, hd = q.shape
    outs, lses = [], []
    for window, d in DIL_BRANCHES:
        L = s // d

        def to_sub(t):
            return t.reshape(b, h, L, d, hd).transpose(0, 3, 1, 2, 4).reshape(b * d, h, L, hd)

        o, lse = banded_attention(to_sub(q)[:, :, None], to_sub(k), to_sub(v),
                                  window // (2 * d), slopes[:, None], d)
        outs.append(o[:, :, 0].reshape(b, d, h, L, hd).transpose(0, 2, 3, 1, 4).reshape(b, h, s, hd))
        lses.append(lse[:, :, 0].reshape(b, d, h, L).transpose(0, 2, 3, 1).reshape(b, h, s))
    w = jax.nn.softmax(jnp.stack(lses), axis=0)
    o = jnp.einsum("nbhs,nbhsd->bhsd", w, jnp.stack(outs).astype(jnp.float32))
    return o.astype(q.dtype)


def hybrid_mixer(h, w_in, qk_gain, rpb, sink, out_gain, w_out):
    sizes = (NA_W, NA_W, NA_W, SWA_QW, SWA_KVW, SWA_KVW, DIL_W, DIL_W, DIL_W)
    cuts = [int(c) for c in np.cumsum(sizes)[:-1]]
    proj = h @ w_in
    qa, ka, va, qb, kb, vb, qc, kc, vc = jnp.split(proj, cuts, axis=-1)
    b, s, _ = h.shape
    slopes = alibi_slopes(SWA_Q_HEADS + DIL_HEADS)
    oa = neighbourhood_attention(rms_norm(split_heads(qa, NA_HEADS), qk_gain[0, 0]),
                                 rms_norm(split_heads(ka, NA_HEADS), qk_gain[0, 1]),
                                 split_heads(va, NA_HEADS), rpb)
    qbh = rms_norm(split_heads(qb, SWA_Q_HEADS), qk_gain[1, 0]).reshape(b, SWA_KV_HEADS, SWA_GROUP, s, HEAD_DIM)
    kbh = rms_norm(split_heads(kb, SWA_KV_HEADS), qk_gain[1, 1])
    ob, _ = banded_attention(qbh, kbh, split_heads(vb, SWA_KV_HEADS), SWA_HALF_WINDOW,
                             slopes[:SWA_Q_HEADS].reshape(SWA_KV_HEADS, SWA_GROUP), 1,
                             sink.reshape(SWA_KV_HEADS, SWA_GROUP))
    ob = ob.reshape(b, SWA_Q_HEADS, s, HEAD_DIM)
    oc = dilated_attention(rms_norm(split_heads(qc, DIL_HEADS), qk_gain[2, 0]),
                           rms_norm(split_heads(kc, DIL_HEADS), qk_gain[2, 1]),
                           split_heads(vc, DIL_HEADS), slopes[SWA_Q_HEADS:])
    ga, gb, gc = jnp.split(out_gain, [NA_W, NA_W + SWA_QW])
    y = jnp.concatenate([rms_norm(merge_heads(oa), ga),
                         rms_norm(merge_heads(ob), gb),
                         rms_norm(merge_heads(oc), gc)], axis=-1)
    return y @ w_out


def swiglu(h, w_gu, w_down):
    gu = h @ w_gu
    gate, up = jnp.split(gu, 2, axis=-1)
    return (jax.nn.silu(gate) * up) @ w_down


def setup_inputs(seed: int = 0) -> dict:
    key = jax.random.key(seed)
    ks = jax.random.split(key, 12)
    f32 = jnp.float32
    x = jax.random.normal(ks[0], (BATCH, SEQ, D_MODEL), f32)
    attn_norm = 1.0 + 0.05 * jax.random.normal(ks[1], (DEPTH, D_MODEL), f32)
    w_in = jax.random.normal(ks[2], (DEPTH, D_MODEL, IN_W), f32) * D_MODEL ** -0.5
    qk_gain = 1.0 + 0.05 * jax.random.normal(ks[3], (DEPTH, 3, 2, HEAD_DIM), f32)
    rpb = 0.1 * jax.random.normal(ks[4], (DEPTH, NA_HEADS, 2 * NA_ROWS_MAX - 1, 2 * NA_COLS - 1), f32)
    sink = 0.5 * jax.random.normal(ks[5], (DEPTH, SWA_Q_HEADS), f32)
    out_gain = 1.0 + 0.05 * jax.random.normal(ks[6], (DEPTH, MIX_W), f32)
    w_out = jax.random.normal(ks[7], (DEPTH, MIX_W, D_MODEL), f32) * MIX_W ** -0.5
    ffn_norm = 1.0 + 0.05 * jax.random.normal(ks[8], (DEPTH, D_MODEL), f32)
    w_gu = jax.random.normal(ks[9], (DEPTH, D_MODEL, 2 * FFN_HIDDEN), f32) * D_MODEL ** -0.5
    w_down = jax.random.normal(ks[10], (DEPTH, FFN_HIDDEN, D_MODEL), f32) * FFN_HIDDEN ** -0.5
    return {"x": x, "attn_norm": attn_norm, "w_in": w_in, "qk_gain": qk_gain, "rpb": rpb,
            "sink": sink, "out_gain": out_gain, "w_out": w_out, "ffn_norm": ffn_norm,
            "w_gu": w_gu, "w_down": w_down}


def reference(x, attn_norm, w_in, qk_gain, rpb, sink, out_gain, w_out, ffn_norm, w_gu, w_down):
    for l in range(DEPTH):
        h = rms_norm(x, attn_norm[l])
        x = x + hybrid_mixer(h, w_in[l], qk_gain[l], rpb[l], sink[l], out_gain[l], w_out[l])
        h = rms_norm(x, ffn_norm[l])
        x = x + swiglu(h, w_gu[l], w_down[l])
    return x
```

```python
import functools

import numpy as np
import jax
import jax.numpy as jnp
from jax import lax
from jax.experimental import pallas as pl
from jax.experimental.pallas import tpu as pltpu

D_MODEL = 1024
SEQ = 2048
HEAD_DIM = 64
GRID_W = 64
NA_HEADS = 4
NA_ROWS = 8
NA_COLS = 16
SWA_Q_HEADS = 6
SWA_KV_HEADS = 2
SWA_GROUP = SWA_Q_HEADS // SWA_KV_HEADS
SWA_HALF_WINDOW = 128
DIL_HEADS = 6
DIL_STRIDES = (1, 4, 16)
DIL_HALF = 64
NA_W = NA_HEADS * HEAD_DIM
SWA_QW = SWA_Q_HEADS * HEAD_DIM
SWA_KVW = SWA_KV_HEADS * HEAD_DIM
DIL_W = DIL_HEADS * HEAD_DIM
IN_W = 3 * NA_W + SWA_QW + 2 * SWA_KVW + 3 * DIL_W
MIX_W = NA_W + SWA_QW + DIL_W
FFN_HIDDEN = 2816
NORM_EPS = 1e-6
NEG_INF = -1e30
QK_SCALE = HEAD_DIM ** -0.5

LANES = 128
VMEM_LIMIT = 56 * 1024 * 1024

QB = 128
PREP_ROWS = 256
NQB = SEQ // QB
ROW_TILE = 512
PROJ_N_CHUNK = 512
FFN_H_CHUNK = 1408

NA_Q0, NA_K0, NA_V0 = 0, NA_W // LANES, 2 * NA_W // LANES
SWA_COL0 = 3 * NA_W
SWA_K0 = (SWA_COL0 + SWA_QW) // LANES
SWA_V0 = SWA_K0 + SWA_KVW // LANES
DIL_COL0 = SWA_COL0 + SWA_QW + 2 * SWA_KVW
DIL_Q0 = DIL_COL0 // LANES
DIL_K0 = DIL_Q0 + DIL_W // LANES
DIL_V0 = DIL_K0 + DIL_W // LANES

NA_KEY_BLOCKS = 5
NA_WIN = NA_KEY_BLOCKS * QB
NA_VARIANTS = 5

F32 = jnp.float32
BF16 = jnp.bfloat16


def _dot(a, b):
    return jnp.dot(a, b, preferred_element_type=F32)


def _dot_nt(a, b):
    return lax.dot_general(a, b, (((1,), (1,)), ((), ())), preferred_element_type=F32)


def _low_half(shape):
    return lax.broadcasted_iota(jnp.int32, shape, len(shape) - 1) < HEAD_DIM


def _pair_rms(x, gain):
    lo = _low_half(x.shape)
    sq = x * x
    s_lo = jnp.sum(jnp.where(lo, sq, 0.0), axis=-1, keepdims=True)
    s_hi = jnp.sum(jnp.where(lo, 0.0, sq), axis=-1, keepdims=True)
    ms = jnp.where(lo, s_lo, s_hi) * (1.0 / HEAD_DIM)
    return x * lax.rsqrt(ms + NORM_EPS) * gain


def _softmax_block(s):
    m = jnp.max(s, axis=-1, keepdims=True)
    p = jnp.exp(s - m)
    return m, p, jnp.sum(p, axis=-1, keepdims=True)


def _proj_kernel(x_ref, g_ref, w_ref, o_ref):
    x = x_ref[...]
    ms = jnp.mean(x * x, axis=-1, keepdims=True)
    h = (x * lax.rsqrt(ms + NORM_EPS) * g_ref[...]).astype(BF16)
    for c in range(IN_W // PROJ_N_CHUNK):
        cols = slice(c * PROJ_N_CHUNK, (c + 1) * PROJ_N_CHUNK)
        o_ref[:, cols] = _dot(h, w_ref[:, cols]).astype(BF16)


def _group_rms(y, gain):
    yf = y.astype(F32)
    ms = jnp.mean(yf * yf, axis=-1, keepdims=True)
    return (yf * lax.rsqrt(ms + NORM_EPS) * gain).astype(BF16)


def _out_kernel(ya_ref, yb_ref, yc_ref, x_ref, g_ref, w_ref, o_ref, y_scr):
    y_scr[:, 0:NA_W] = _group_rms(ya_ref[...], g_ref[:, 0:NA_W])
    y_scr[:, NA_W:NA_W + SWA_QW] = _group_rms(yb_ref[...], g_ref[:, NA_W:NA_W + SWA_QW])
    y_scr[:, NA_W + SWA_QW:MIX_W] = _group_rms(yc_ref[...], g_ref[:, NA_W + SWA_QW:MIX_W])
    o_ref[...] = x_ref[...] + _dot(y_scr[...], w_ref[...])


def _ffn_kernel(x_ref, g_ref, wgu_ref, wd_ref, o_ref, h_scr):
    x = x_ref[...]
    ms = jnp.mean(x * x, axis=-1, keepdims=True)
    h_scr[...] = (x * lax.rsqrt(ms + NORM_EPS) * g_ref[...]).astype(BF16)
    o_ref[...] = x
    for c in range(FFN_HIDDEN // FFN_H_CHUNK):
        lo = c * FFN_H_CHUNK
        gate = _dot(h_scr[...], wgu_ref[:, lo:lo + FFN_H_CHUNK])
        up = _dot(h_scr[...], wgu_ref[:, FFN_HIDDEN + lo:FFN_HIDDEN + lo + FFN_H_CHUNK])
        act = (gate / (1.0 + jnp.exp(-gate)) * up).astype(BF16)
        o_ref[...] += _dot(act, wd_ref[lo:lo + FFN_H_CHUNK, :])


def _na_kernel(q_ref, k_ref, v_ref, g_ref, tbl_ref, o_ref, qm_scr, kn_scr):
    def prep(i, carry):
        rows = pl.ds(pl.multiple_of(i * PREP_ROWS, PREP_ROWS), PREP_ROWS)
        lo = _low_half((PREP_ROWS, LANES))
        qn = _pair_rms(q_ref[rows, :].astype(F32), g_ref[0:1, :]) * QK_SCALE
        qm_scr[0, rows, :] = jnp.where(lo, qn, 0.0).astype(BF16)
        qm_scr[1, rows, :] = jnp.where(lo, 0.0, qn).astype(BF16)
        kn_scr[rows, :] = _pair_rms(k_ref[rows, :].astype(F32), g_ref[1:2, :]).astype(BF16)
        return carry

    lax.fori_loop(0, SEQ // PREP_ROWS, prep, 0)

    def block(j, carry):
        start = jnp.clip(j - 2, 0, NQB - NA_KEY_BLOCKS)
        variant = j - start
        krows = pl.ds(pl.multiple_of(start * QB, QB), NA_WIN)
        qrows = pl.ds(pl.multiple_of(j * QB, QB), QB)
        kw = kn_scr[krows, :]
        vw = v_ref[krows, :]
        outs = []
        for h in range(2):
            s = _dot_nt(qm_scr[h, qrows, :], kw) + tbl_ref[h, variant]
            _, p, l = _softmax_block(s)
            outs.append(_dot(p.astype(BF16), vw) * (1.0 / l))
        o_ref[qrows, :] = jnp.where(_low_half((QB, LANES)), outs[0], outs[1]).astype(BF16)
        return carry

    lax.fori_loop(0, NQB, block, 0)


def _na_bias_table(rpb):
    rows = SEQ // GRID_W
    rows_per_block = QB // GRID_W
    rep_block = np.array([0, 1, 2, NQB - 2, NQB - 1])
    start_block = np.clip(rep_block - 2, 0, NQB - NA_KEY_BLOCKS)
    assert np.array_equal(rep_block - start_block, np.arange(NA_VARIANTS))
    qq = np.arange(QB)
    kk = np.arange(NA_WIN)
    qrow = rep_block[:, None] * rows_per_block + qq[None, :] // GRID_W
    qcol = qq % GRID_W
    krow = start_block[:, None] * rows_per_block + kk[None, :] // GRID_W
    kcol = kk % GRID_W
    row_start = np.clip(qrow - NA_ROWS // 2, 0, rows - NA_ROWS)
    col_start = np.clip(qcol - NA_COLS // 2, 0, GRID_W - NA_COLS)
    row_ok = (krow[:, None, :] >= row_start[:, :, None]) & (krow[:, None, :] < row_start[:, :, None] + NA_ROWS)
    col_ok = (kcol[None, :] >= col_start[:, None]) & (kcol[None, :] < col_start[:, None] + NA_COLS)
    valid = row_ok & col_ok[None]
    row_idx = np.clip(krow[:, None, :] - qrow[:, :, None] + NA_ROWS - 1, 0, 2 * NA_ROWS - 2)
    col_idx = np.clip(kcol[None, :] - qcol[:, None] + NA_COLS - 1, 0, 2 * NA_COLS - 2)
    col_idx = np.broadcast_to(col_idx[None], row_idx.shape)
    bias = rpb.astype(F32)[:, row_idx, col_idx]
    return jnp.where(valid[None], bias, NEG_INF)


def _swa_kernel(slopes_ref, sink_ref, q_ref, k_ref, v_ref, g_ref, o_ref, qm_scr, kn_scr, bias_scr):
    win = QB + 2 * SWA_HALF_WINDOW
    qq = lax.broadcasted_iota(jnp.int32, (QB, win), 0)
    kk = lax.broadcasted_iota(jnp.int32, (QB, win), 1)
    dist = jnp.abs(kk - SWA_HALF_WINDOW - qq)
    in_window = dist <= SWA_HALF_WINDOW
    dist_f = dist.astype(F32)
    for h in range(SWA_Q_HEADS):
        bias_scr[h] = jnp.where(in_window, -slopes_ref[h] * dist_f, NEG_INF)

    def prep(i, carry):
        rows = pl.ds(pl.multiple_of(i * PREP_ROWS, PREP_ROWS), PREP_ROWS)
        lo = _low_half((PREP_ROWS, LANES))
        kn_scr[rows, :] = _pair_rms(k_ref[rows, :].astype(F32), g_ref[1:2, :]).astype(BF16)
        for t in range(SWA_QW // LANES):
            qn = _pair_rms(q_ref[rows, t * LANES:(t + 1) * LANES].astype(F32), g_ref[0:1, :]) * QK_SCALE
            for half in range(2):
                h = 2 * t + half
                kv = h // SWA_GROUP
                src = qn if half == kv else pltpu.roll(qn, HEAD_DIM, axis=1)
                keep = lo if kv == 0 else jnp.logical_not(lo)
                qm_scr[h, rows, :] = jnp.where(keep, src, 0.0).astype(BF16)
        return carry

    lax.fori_loop(0, SEQ // PREP_ROWS, prep, 0)

    def block(j, key_start, n_keys, bias_col):
        krows = pl.ds(pl.multiple_of(key_start, QB), n_keys)
        qrows = pl.ds(pl.multiple_of(j * QB, QB), QB)
        kw = kn_scr[krows, :]
        vw = v_ref[krows, :]
        lo = _low_half((QB, LANES))
        outs = []
        for h in range(SWA_Q_HEADS):
            kv = h // SWA_GROUP
            s = _dot_nt(qm_scr[h, qrows, :], kw) + bias_scr[h, :, bias_col:bias_col + n_keys]
            sink = sink_ref[h]
            m = jnp.maximum(jnp.max(s, axis=-1, keepdims=True), sink)
            p = jnp.exp(s - m)
            den = jnp.sum(p, axis=-1, keepdims=True) + jnp.exp(sink - m)
            o = _dot(p.astype(BF16), vw) * (1.0 / den)
            outs.append(o if h % 2 == kv else pltpu.roll(o, HEAD_DIM, axis=1))
        for t in range(SWA_QW // LANES):
            o_ref[qrows, t * LANES:(t + 1) * LANES] = jnp.where(lo, outs[2 * t], outs[2 * t + 1]).astype(BF16)

    block(0, 0, win - SWA_HALF_WINDOW, SWA_HALF_WINDOW)

    def middle(j, carry):
        block(j, j * QB - SWA_HALF_WINDOW, win, 0)
        return carry

    lax.fori_loop(1, NQB - 1, middle, 0)
    block(NQB - 1, SEQ - (win - SWA_HALF_WINDOW), win - SWA_HALF_WINDOW, 0)


DIL_WIN = QB + 2 * DIL_HALF


def _dil_kernel(slopes_ref, q_ref, k_ref, v_ref, g_ref, o_ref,
                nat_scr, lay_scr, bias_scr, bias16_scr, m_scr, l_scr, acc_scr):
    pair = pl.program_id(1)
    slope = [slopes_ref[SWA_Q_HEADS + 2 * pair + h] for h in range(2)]

    qq = lax.broadcasted_iota(jnp.int32, (QB, DIL_WIN), 0)
    kk = lax.broadcasted_iota(jnp.int32, (QB, DIL_WIN), 1)
    for shift in range(3):
        dist = jnp.abs(kk - shift * DIL_HALF - qq)
        ok = dist <= DIL_HALF
        dist_f = dist.astype(F32)
        for h in range(2):
            for b, stride in enumerate(DIL_STRIDES[:2]):
                bias_scr[h, b, shift] = jnp.where(ok, -slope[h] * (dist_f * float(stride)), NEG_INF)
    qq = lax.broadcasted_iota(jnp.int32, (QB, QB), 0)
    kk = lax.broadcasted_iota(jnp.int32, (QB, QB), 1)
    dist = jnp.abs(kk - qq)
    for h in range(2):
        bias16_scr[h] = jnp.where(dist <= DIL_HALF, -slope[h] * (dist.astype(F32) * float(DIL_STRIDES[2])),
                                  NEG_INF)

    def prep(i, carry):
        rows = pl.ds(pl.multiple_of(i * PREP_ROWS, PREP_ROWS), PREP_ROWS)
        lo = _low_half((PREP_ROWS, LANES))
        qn = _pair_rms(q_ref[rows, :].astype(F32), g_ref[0:1, :]) * QK_SCALE
        kn = _pair_rms(k_ref[rows, :].astype(F32), g_ref[1:2, :])
        nat_scr[0, rows, :] = qn
        nat_scr[1, rows, :] = kn
        nat_scr[2, rows, :] = v_ref[rows, :].astype(F32)
        lay_scr[0, 0, rows, :] = jnp.where(lo, qn, 0.0).astype(BF16)
        lay_scr[0, 1, rows, :] = jnp.where(lo, 0.0, qn).astype(BF16)
        lay_scr[0, 2, rows, :] = kn.astype(BF16)
        lay_scr[0, 3, rows, :] = v_ref[rows, :]
        return carry

    lax.fori_loop(0, SEQ // PREP_ROWS, prep, 0)

    for b, stride in enumerate(DIL_STRIDES[1:], start=1):
        sub = SEQ // stride
        for phase in range(stride):
            src = pl.ds(phase, sub, stride=stride)
            dst = slice(phase * sub, (phase + 1) * sub)
            lo = _low_half((sub, LANES))
            qn = nat_scr[0, src, :]
            lay_scr[b, 0, dst, :] = jnp.where(lo, qn, 0.0).astype(BF16)
            lay_scr[b, 1, dst, :] = jnp.where(lo, 0.0, qn).astype(BF16)
            lay_scr[b, 2, dst, :] = nat_scr[1, src, :].astype(BF16)
            lay_scr[b, 3, dst, :] = nat_scr[2, src, :].astype(BF16)

    for b, stride in enumerate(DIL_STRIDES):
        sub = SEQ // stride
        blocks_per_sub = sub // QB

        def block(j, carry, b=b, stride=stride, sub=sub, blocks_per_sub=blocks_per_sub):
            phase = j // blocks_per_sub
            jl = j % blocks_per_sub
            qrows = pl.ds(pl.multiple_of(j * QB, QB), QB)
            if blocks_per_sub == 1:
                n_keys = QB
                krows = qrows
            else:
                n_keys = DIL_WIN
                local = jnp.clip(jl * QB - DIL_HALF, 0, sub - DIL_WIN)
                shift = (jl * QB - local) // DIL_HALF
                krows = pl.ds(pl.multiple_of(phase * sub + local, DIL_HALF), n_keys)
            kw = lay_scr[b, 2, krows, :]
            vw = lay_scr[b, 3, krows, :]
            ms, ls, accs = [], [], []
            for h in range(2):
                bias = bias16_scr[h] if blocks_per_sub == 1 else bias_scr[h, b, shift]
                s = _dot_nt(lay_scr[b, h, qrows, :], kw) + bias
                m, p, l = _softmax_block(s)
                ms.append(m)
                ls.append(l)
                accs.append(_dot(p.astype(BF16), vw))
            lo = _low_half((QB, LANES))
            if stride == 1:
                dst = qrows
            else:
                dst = pl.ds(jl * (QB * stride) + phase, QB, stride=stride)
            m_scr[b, dst, :] = jnp.where(lo, ms[0], ms[1])
            l_scr[b, dst, :] = jnp.where(lo, ls[0], ls[1])
            acc_scr[b, dst, :] = jnp.where(lo, accs[0], accs[1])
            return carry

        lax.fori_loop(0, NQB, block, 0)

    def merge(i, carry):
        rows = pl.ds(pl.multiple_of(i * PREP_ROWS, PREP_ROWS), PREP_ROWS)
        m = [m_scr[b, rows, :] for b in range(3)]
        top = jnp.maximum(jnp.maximum(m[0], m[1]), m[2])
        num = jnp.zeros((PREP_ROWS, LANES), F32)
        den = jnp.zeros((PREP_ROWS, LANES), F32)
        for b in range(3):
            w = jnp.exp(m[b] - top)
            num = num + w * acc_scr[b, rows, :]
            den = den + w * l_scr[b, rows, :]
        o_ref[rows, :] = (num / den).astype(BF16)
        return carry

    lax.fori_loop(0, SEQ // PREP_ROWS, merge, 0)


def _params(semantics):
    return pltpu.CompilerParams(dimension_semantics=semantics, vmem_limit_bytes=VMEM_LIMIT)


def _resident(shape):
    return pl.BlockSpec(shape, lambda *_: (0,) * len(shape), pipeline_mode=pl.Buffered(1))


def _smem():
    return pl.BlockSpec(memory_space=pltpu.SMEM)


def _proj(x, gain, w):
    t = x.shape[0]
    return pl.pallas_call(
        _proj_kernel,
        out_shape=jax.ShapeDtypeStruct((t, IN_W), BF16),
        grid=(t // ROW_TILE,),
        in_specs=[pl.BlockSpec((ROW_TILE, D_MODEL), lambda i: (i, 0)),
                  _resident((1, D_MODEL)),
                  _resident((D_MODEL, IN_W))],
        out_specs=pl.BlockSpec((ROW_TILE, IN_W), lambda i: (i, 0)),
        compiler_params=_params(("parallel",)),
        name="proj",
    )(x, gain, w)


def _seq_block(col_block, width=LANES):
    return pl.BlockSpec((SEQ, width), col_block)


def _na(proj, gains, table):
    t = proj.shape[0]
    pairs = NA_W // LANES
    return pl.pallas_call(
        _na_kernel,
        out_shape=jax.ShapeDtypeStruct((t, NA_W), BF16),
        grid=(pairs, t // SEQ),
        in_specs=[_seq_block(lambda p, b: (b, NA_Q0 + p)),
                  _seq_block(lambda p, b: (b, NA_K0 + p)),
                  _seq_block(lambda p, b: (b, NA_V0 + p)),
                  _resident((2, LANES)),
                  pl.BlockSpec((2, NA_VARIANTS, QB, NA_WIN), lambda p, b: (p, 0, 0, 0))],
        out_specs=_seq_block(lambda p, b: (b, p)),
        scratch_shapes=[pltpu.VMEM((2, SEQ, LANES), BF16), pltpu.VMEM((SEQ, LANES), BF16)],
        compiler_params=_params(("arbitrary", "arbitrary")),
        name="na",
    )(proj, proj, proj, gains, table)


def _swa(proj, gains, slopes, sink):
    t = proj.shape[0]
    win = QB + 2 * SWA_HALF_WINDOW
    return pl.pallas_call(
        _swa_kernel,
        out_shape=jax.ShapeDtypeStruct((t, SWA_QW), BF16),
        grid=(t // SEQ,),
        in_specs=[_smem(), _smem(),
                  _seq_block(lambda b: (b, SWA_COL0 // SWA_QW), SWA_QW),
                  _seq_block(lambda b: (b, SWA_K0)),
                  _seq_block(lambda b: (b, SWA_V0)),
                  _resident((2, LANES))],
        out_specs=_seq_block(lambda b: (b, 0), SWA_QW),
        scratch_shapes=[pltpu.VMEM((SWA_Q_HEADS, SEQ, LANES), BF16), pltpu.VMEM((SEQ, LANES), BF16),
                        pltpu.VMEM((SWA_Q_HEADS, QB, win), F32)],
        compiler_params=_params(("parallel",)),
        name="swa",
    )(slopes, sink, proj, proj, proj, gains)


def _dil(proj, gains, slopes):
    t = proj.shape[0]
    pairs = DIL_W // LANES
    return pl.pallas_call(
        _dil_kernel,
        out_shape=jax.ShapeDtypeStruct((t, DIL_W), BF16),
        grid=(t // SEQ, pairs),
        in_specs=[_smem(),
                  _seq_block(lambda b, p: (b, DIL_Q0 + p)),
                  _seq_block(lambda b, p: (b, DIL_K0 + p)),
                  _seq_block(lambda b, p: (b, DIL_V0 + p)),
                  _resident((2, LANES))],
        out_specs=_seq_block(lambda b, p: (b, p)),
        scratch_shapes=[pltpu.VMEM((3, SEQ, LANES), F32),
                        pltpu.VMEM((3, 4, SEQ, LANES), BF16),
                        pltpu.VMEM((2, 2, 3, QB, DIL_WIN), F32),
                        pltpu.VMEM((2, QB, QB), F32),
                        pltpu.VMEM((3, SEQ, LANES), F32),
                        pltpu.VMEM((3, SEQ, LANES), F32),
                        pltpu.VMEM((3, SEQ, LANES), F32)],
        compiler_params=_params(("parallel", "parallel")),
        name="dil",
    )(slopes, proj, proj, proj, gains)


def _out(ya, yb, yc, x, gain, w):
    t = x.shape[0]
    row = lambda width: pl.BlockSpec((ROW_TILE, width), lambda i: (i, 0))
    return pl.pallas_call(
        _out_kernel,
        out_shape=jax.ShapeDtypeStruct((t, D_MODEL), F32),
        grid=(t // ROW_TILE,),
        in_specs=[row(NA_W), row(SWA_QW), row(DIL_W), row(D_MODEL),
                  _resident((1, MIX_W)), _resident((MIX_W, D_MODEL))],
        out_specs=row(D_MODEL),
        scratch_shapes=[pltpu.VMEM((ROW_TILE, MIX_W), BF16)],
        compiler_params=_params(("parallel",)),
        name="out",
    )(ya, yb, yc, x, gain, w)


def _ffn(x, gain, w_gu, w_down):
    t = x.shape[0]
    row = pl.BlockSpec((ROW_TILE, D_MODEL), lambda i: (i, 0))
    return pl.pallas_call(
        _ffn_kernel,
        out_shape=jax.ShapeDtypeStruct((t, D_MODEL), F32),
        grid=(t // ROW_TILE,),
        in_specs=[row, _resident((1, D_MODEL)),
                  _resident((D_MODEL, 2 * FFN_HIDDEN)), _resident((FFN_HIDDEN, D_MODEL))],
        out_specs=row,
        scratch_shapes=[pltpu.VMEM((ROW_TILE, D_MODEL), BF16)],
        compiler_params=_params(("parallel",)),
        name="ffn",
    )(x, gain, w_gu, w_down)


def _pair_gain(g):
    return jnp.concatenate([g, g], axis=-1).astype(F32)


def kernel(x, attn_norm, w_in, qk_gain, rpb, sink, out_gain, w_out, ffn_norm, w_gu, w_down):
    batch, seq, d = x.shape
    assert (seq, d) == (SEQ, D_MODEL)
    depth = w_in.shape[0]
    n_slopes = SWA_Q_HEADS + DIL_HEADS
    slopes = 2.0 ** (-8.0 * (jnp.arange(n_slopes, dtype=F32) + 1.0) / n_slopes)
    xf = x.reshape(batch * seq, d)
    for l in range(depth):
        proj = _proj(xf, attn_norm[l][None, :], w_in[l].astype(BF16))
        ya = _na(proj, _pair_gain(qk_gain[l, 0]), _na_bias_table(rpb[l]))
        yb = _swa(proj, _pair_gain(qk_gain[l, 1]), slopes, sink[l].astype(F32))
        yc = _dil(proj, _pair_gain(qk_gain[l, 2]), slopes)
        xf = _out(ya, yb, yc, xf, out_gain[l][None, :], w_out[l].astype(BF16))
        xf = _ffn(xf, ffn_norm[l][None, :], w_gu[l].astype(BF16), w_down[l].astype(BF16))
    return xf.reshape(batch, seq, d)
```

```python
import functools

import numpy as np
import jax
import jax.numpy as jnp
from jax import lax
from jax.experimental import pallas as pl
from jax.experimental.pallas import tpu as pltpu

D_MODEL = 1024
SEQ = 2048
HEAD_DIM = 64
GRID_W = 64
NA_HEADS = 4
NA_ROWS = 8
NA_COLS = 16
SWA_Q_HEADS = 6
SWA_KV_HEADS = 2
SWA_GROUP = SWA_Q_HEADS // SWA_KV_HEADS
SWA_HALF_WINDOW = 128
DIL_HEADS = 6
DIL_STRIDES = (1, 4, 16)
DIL_HALF = 64
NA_W = NA_HEADS * HEAD_DIM
SWA_QW = SWA_Q_HEADS * HEAD_DIM
SWA_KVW = SWA_KV_HEADS * HEAD_DIM
DIL_W = DIL_HEADS * HEAD_DIM
IN_W = 3 * NA_W + SWA_QW + 2 * SWA_KVW + 3 * DIL_W
MIX_W = NA_W + SWA_QW + DIL_W
FFN_HIDDEN = 2816
NORM_EPS = 1e-6
NEG_INF = -1e30
QK_SCALE = HEAD_DIM ** -0.5

LANES = 128
VMEM_LIMIT = 56 * 1024 * 1024

QB = 128
PREP_ROWS = 256
NQB = SEQ // QB
ROW_TILE = 512
PROJ_N_CHUNK = 512
NA_UNROLL = 2
SWA_UNROLL = 2
DIL_UNROLL = 4
FFN_H_CHUNK = 1408

NA_Q0, NA_K0, NA_V0 = 0, NA_W // LANES, 2 * NA_W // LANES
SWA_COL0 = 3 * NA_W
SWA_K0 = (SWA_COL0 + SWA_QW) // LANES
SWA_V0 = SWA_K0 + SWA_KVW // LANES
DIL_COL0 = SWA_COL0 + SWA_QW + 2 * SWA_KVW
DIL_Q0 = DIL_COL0 // LANES
DIL_K0 = DIL_Q0 + DIL_W // LANES
DIL_V0 = DIL_K0 + DIL_W // LANES

NA_KEY_BLOCKS = 5
NA_WIN = NA_KEY_BLOCKS * QB
NA_VARIANTS = 5

F32 = jnp.float32
BF16 = jnp.bfloat16


def _dot(a, b):
    return jnp.dot(a, b, preferred_element_type=F32)


def _dot_nt(a, b):
    return lax.dot_general(a, b, (((1,), (1,)), ((), ())), preferred_element_type=F32)


def _low_half(shape):
    return lax.broadcasted_iota(jnp.int32, shape, len(shape) - 1) < HEAD_DIM


def _pair_rms(x, gain):
    lo = _low_half(x.shape)
    sq = x * x
    s_lo = jnp.sum(jnp.where(lo, sq, 0.0), axis=-1, keepdims=True)
    s_hi = jnp.sum(jnp.where(lo, 0.0, sq), axis=-1, keepdims=True)
    ms = jnp.where(lo, s_lo, s_hi) * (1.0 / HEAD_DIM)
    return x * lax.rsqrt(ms + NORM_EPS) * gain


def _softmax_block(s):
    m = jnp.max(s, axis=-1, keepdims=True)
    p = jnp.exp(s - m)
    return m, p, jnp.sum(p, axis=-1, keepdims=True)


def _proj_kernel(x_ref, g_ref, w_ref, o_ref):
    x = x_ref[...]
    ms = jnp.mean(x * x, axis=-1, keepdims=True)
    h = (x * lax.rsqrt(ms + NORM_EPS) * g_ref[...]).astype(BF16)
    for c in range(IN_W // PROJ_N_CHUNK):
        cols = slice(c * PROJ_N_CHUNK, (c + 1) * PROJ_N_CHUNK)
        o_ref[:, cols] = _dot(h, w_ref[:, cols]).astype(BF16)


def _group_rms(y, gain):
    yf = y.astype(F32)
    ms = jnp.mean(yf * yf, axis=-1, keepdims=True)
    return (yf * lax.rsqrt(ms + NORM_EPS) * gain).astype(BF16)


def _out_kernel(ya_ref, yb_ref, yc_ref, x_ref, g_ref, w_ref, o_ref, y_scr):
    y_scr[:, 0:NA_W] = _group_rms(ya_ref[...], g_ref[:, 0:NA_W])
    y_scr[:, NA_W:NA_W + SWA_QW] = _group_rms(yb_ref[...], g_ref[:, NA_W:NA_W + SWA_QW])
    y_scr[:, NA_W + SWA_QW:MIX_W] = _group_rms(yc_ref[...], g_ref[:, NA_W + SWA_QW:MIX_W])
    o_ref[...] = x_ref[...] + _dot(y_scr[...], w_ref[...])


def _ffn_kernel(x_ref, g_ref, wgu_ref, wd_ref, o_ref, h_scr):
    x = x_ref[...]
    ms = jnp.mean(x * x, axis=-1, keepdims=True)
    h_scr[...] = (x * lax.rsqrt(ms + NORM_EPS) * g_ref[...]).astype(BF16)
    o_ref[...] = x
    for c in range(FFN_HIDDEN // FFN_H_CHUNK):
        lo = c * FFN_H_CHUNK
        gate = _dot(h_scr[...], wgu_ref[:, lo:lo + FFN_H_CHUNK])
        up = _dot(h_scr[...], wgu_ref[:, FFN_HIDDEN + lo:FFN_HIDDEN + lo + FFN_H_CHUNK])
        act = (gate / (1.0 + jnp.exp(-gate)) * up).astype(BF16)
        o_ref[...] += _dot(act, wd_ref[lo:lo + FFN_H_CHUNK, :])


def _na_kernel(q_ref, k_ref, v_ref, g_ref, tbl_ref, o_ref, qm_scr, kn_scr):
    def prep(i, carry):
        rows = pl.ds(pl.multiple_of(i * PREP_ROWS, PREP_ROWS), PREP_ROWS)
        lo = _low_half((PREP_ROWS, LANES))
        qn = _pair_rms(q_ref[rows, :].astype(F32), g_ref[0:1, :]) * QK_SCALE
        qm_scr[0, rows, :] = jnp.where(lo, qn, 0.0).astype(BF16)
        qm_scr[1, rows, :] = jnp.where(lo, 0.0, qn).astype(BF16)
        kn_scr[rows, :] = _pair_rms(k_ref[rows, :].astype(F32), g_ref[1:2, :]).astype(BF16)
        return carry

    lax.fori_loop(0, SEQ // PREP_ROWS, prep, 0)

    def block(j, carry):
        start = jnp.clip(j - 2, 0, NQB - NA_KEY_BLOCKS)
        variant = j - start
        krows = pl.ds(pl.multiple_of(start * QB, QB), NA_WIN)
        qrows = pl.ds(pl.multiple_of(j * QB, QB), QB)
        kw = kn_scr[krows, :]
        vw = v_ref[krows, :]
        outs = []
        for h in range(2):
            s = _dot_nt(qm_scr[h, qrows, :], kw) + tbl_ref[h, variant]
            _, p, l = _softmax_block(s)
            outs.append(_dot(p.astype(BF16), vw) * (1.0 / l))
        o_ref[qrows, :] = jnp.where(_low_half((QB, LANES)), outs[0], outs[1]).astype(BF16)
        return carry

    lax.fori_loop(0, NQB, block, 0, unroll=NA_UNROLL)


def _na_bias_table(rpb):
    rows = SEQ // GRID_W
    rows_per_block = QB // GRID_W
    rep_block = np.array([0, 1, 2, NQB - 2, NQB - 1])
    start_block = np.clip(rep_block - 2, 0, NQB - NA_KEY_BLOCKS)
    assert np.array_equal(rep_block - start_block, np.arange(NA_VARIANTS))
    qq = np.arange(QB)
    kk = np.arange(NA_WIN)
    qrow = rep_block[:, None] * rows_per_block + qq[None, :] // GRID_W
    qcol = qq % GRID_W
    krow = start_block[:, None] * rows_per_block + kk[None, :] // GRID_W
    kcol = kk % GRID_W
    row_start = np.clip(qrow - NA_ROWS // 2, 0, rows - NA_ROWS)
    col_start = np.clip(qcol - NA_COLS // 2, 0, GRID_W - NA_COLS)
    row_ok = (krow[:, None, :] >= row_start[:, :, None]) & (krow[:, None, :] < row_start[:, :, None] + NA_ROWS)
    col_ok = (kcol[None, :] >= col_start[:, None]) & (kcol[None, :] < col_start[:, None] + NA_COLS)
    valid = row_ok & col_ok[None]
    row_idx = np.clip(krow[:, None, :] - qrow[:, :, None] + NA_ROWS - 1, 0, 2 * NA_ROWS - 2)
    row_idx = row_idx[:, ::GRID_W, ::GRID_W]
    heads, n_row_rel, n_col_rel = rpb.shape
    period = 2 * GRID_W - 1
    lead = GRID_W - NA_COLS
    seq = jnp.pad(rpb.astype(F32), ((0, 0), (0, 0), (lead, period - lead - n_col_rel)))
    skew = jnp.tile(seq, (1, 1, GRID_W + 1))[..., :GRID_W * 2 * GRID_W]
    toe = skew.reshape(heads, n_row_rel, GRID_W, 2 * GRID_W)[:, :, ::-1, :GRID_W]
    tiles = jnp.stack([toe[:, a] for a in row_idx.reshape(-1)], axis=1)
    tiles = tiles.reshape(heads, NA_VARIANTS, rows_per_block, NA_WIN // GRID_W, GRID_W, GRID_W)
    bias = tiles.transpose(0, 1, 2, 4, 3, 5).reshape(heads, NA_VARIANTS, QB, NA_WIN)
    return jnp.where(valid[None], bias, NEG_INF)


def _swa_kernel(slopes_ref, sink_ref, q_ref, k_ref, v_ref, g_ref, o_ref, qm_scr, kn_scr, bias_scr):
    win = QB + 2 * SWA_HALF_WINDOW
    qq = lax.broadcasted_iota(jnp.int32, (QB, win), 0)
    kk = lax.broadcasted_iota(jnp.int32, (QB, win), 1)
    dist = jnp.abs(kk - SWA_HALF_WINDOW - qq)
    in_window = dist <= SWA_HALF_WINDOW
    dist_f = dist.astype(F32)
    for h in range(SWA_Q_HEADS):
        bias_scr[h] = jnp.where(in_window, -slopes_ref[h] * dist_f, NEG_INF)

    def prep(i, carry):
        rows = pl.ds(pl.multiple_of(i * PREP_ROWS, PREP_ROWS), PREP_ROWS)
        lo = _low_half((PREP_ROWS, LANES))
        kn_scr[rows, :] = _pair_rms(k_ref[rows, :].astype(F32), g_ref[1:2, :]).astype(BF16)
        for t in range(SWA_QW // LANES):
            qn = _pair_rms(q_ref[rows, t * LANES:(t + 1) * LANES].astype(F32), g_ref[0:1, :]) * QK_SCALE
            for half in range(2):
                h = 2 * t + half
                kv = h // SWA_GROUP
                src = qn if half == kv else pltpu.roll(qn, HEAD_DIM, axis=1)
                keep = lo if kv == 0 else jnp.logical_not(lo)
                qm_scr[h, rows, :] = jnp.where(keep, src, 0.0).astype(BF16)
        return carry

    lax.fori_loop(0, SEQ // PREP_ROWS, prep, 0)

    def block(j, key_start, n_keys, bias_col):
        krows = pl.ds(pl.multiple_of(key_start, QB), n_keys)
        qrows = pl.ds(pl.multiple_of(j * QB, QB), QB)
        kw = kn_scr[krows, :]
        vw = v_ref[krows, :]
        lo = _low_half((QB, LANES))
        outs = []
        for h in range(SWA_Q_HEADS):
            kv = h // SWA_GROUP
            s = _dot_nt(qm_scr[h, qrows, :], kw) + bias_scr[h, :, bias_col:bias_col + n_keys]
            sink = sink_ref[h]
            m = jnp.maximum(jnp.max(s, axis=-1, keepdims=True), sink)
            p = jnp.exp(s - m)
            den = jnp.sum(p, axis=-1, keepdims=True) + jnp.exp(sink - m)
            o = _dot(p.astype(BF16), vw) * (1.0 / den)
            outs.append(o if h % 2 == kv else pltpu.roll(o, HEAD_DIM, axis=1))
        for t in range(SWA_QW // LANES):
            o_ref[qrows, t * LANES:(t + 1) * LANES] = jnp.where(lo, outs[2 * t], outs[2 * t + 1]).astype(BF16)

    block(0, 0, win - SWA_HALF_WINDOW, SWA_HALF_WINDOW)

    def middle(j, carry):
        block(j, j * QB - SWA_HALF_WINDOW, win, 0)
        return carry

    lax.fori_loop(1, NQB - 1, middle, 0, unroll=SWA_UNROLL)
    block(NQB - 1, SEQ - (win - SWA_HALF_WINDOW), win - SWA_HALF_WINDOW, 0)


DIL_WIN = QB + 2 * DIL_HALF


def _dil_kernel(slopes_ref, q_ref, k_ref, v_ref, g_ref, o_ref,
                nat_scr, lay_scr, bias_scr, bias16_scr, m_scr, l_scr, acc_scr):
    pair = pl.program_id(1)
    slope = [slopes_ref[SWA_Q_HEADS + 2 * pair + h] for h in range(2)]

    qq = lax.broadcasted_iota(jnp.int32, (QB, DIL_WIN), 0)
    kk = lax.broadcasted_iota(jnp.int32, (QB, DIL_WIN), 1)
    for shift in range(3):
        dist = jnp.abs(kk - shift * DIL_HALF - qq)
        ok = dist <= DIL_HALF
        dist_f = dist.astype(F32)
        for h in range(2):
            for b, stride in enumerate(DIL_STRIDES[:2]):
                bias_scr[h, b, shift] = jnp.where(ok, -slope[h] * (dist_f * float(stride)), NEG_INF)
    qq = lax.broadcasted_iota(jnp.int32, (QB, QB), 0)
    kk = lax.broadcasted_iota(jnp.int32, (QB, QB), 1)
    dist = jnp.abs(kk - qq)
    for h in range(2):
        bias16_scr[h] = jnp.where(dist <= DIL_HALF, -slope[h] * (dist.astype(F32) * float(DIL_STRIDES[2])),
                                  NEG_INF)

    def prep(i, carry):
        rows = pl.ds(pl.multiple_of(i * PREP_ROWS, PREP_ROWS), PREP_ROWS)
        lo = _low_half((PREP_ROWS, LANES))
        qn = _pair_rms(q_ref[rows, :].astype(F32), g_ref[0:1, :]) * QK_SCALE
        kn = _pair_rms(k_ref[rows, :].astype(F32), g_ref[1:2, :])
        nat_scr[0, rows, :] = qn
        nat_scr[1, rows, :] = kn
        nat_scr[2, rows, :] = v_ref[rows, :].astype(F32)
        lay_scr[0, 0, rows, :] = jnp.where(lo, qn, 0.0).astype(BF16)
        lay_scr[0, 1, rows, :] = jnp.where(lo, 0.0, qn).astype(BF16)
        lay_scr[0, 2, rows, :] = kn.astype(BF16)
        lay_scr[0, 3, rows, :] = v_ref[rows, :]
        return carry

    lax.fori_loop(0, SEQ // PREP_ROWS, prep, 0)

    for b, stride in enumerate(DIL_STRIDES[1:], start=1):
        sub = SEQ // stride
        for phase in range(stride):
            src = pl.ds(phase, sub, stride=stride)
            dst = slice(phase * sub, (phase + 1) * sub)
            lo = _low_half((sub, LANES))
            qn = nat_scr[0, src, :]
            lay_scr[b, 0, dst, :] = jnp.where(lo, qn, 0.0).astype(BF16)
            lay_scr[b, 1, dst, :] = jnp.where(lo, 0.0, qn).astype(BF16)
            lay_scr[b, 2, dst, :] = nat_scr[1, src, :].astype(BF16)
            lay_scr[b, 3, dst, :] = nat_scr[2, src, :].astype(BF16)

    for b, stride in enumerate(DIL_STRIDES):
        sub = SEQ // stride
        blocks_per_sub = sub // QB

        def block(j, carry, b=b, stride=stride, sub=sub, blocks_per_sub=blocks_per_sub):
            phase = j // blocks_per_sub
            jl = j % blocks_per_sub
            qrows = pl.ds(pl.multiple_of(j * QB, QB), QB)
            if blocks_per_sub == 1:
                n_keys = QB
                krows = qrows
            else:
                n_keys = DIL_WIN
                local = jnp.clip(jl * QB - DIL_HALF, 0, sub - DIL_WIN)
                shift = (jl * QB - local) // DIL_HALF
                krows = pl.ds(pl.multiple_of(phase * sub + local, DIL_HALF), n_keys)
            kw = lay_scr[b, 2, krows, :]
            vw = lay_scr[b, 3, krows, :]
            ms, ls, accs = [], [], []
            for h in range(2):
                bias = bias16_scr[h] if blocks_per_sub == 1 else bias_scr[h, b, shift]
                s = _dot_nt(lay_scr[b, h, qrows, :], kw) + bias
                m, p, l = _softmax_block(s)
                ms.append(m)
                ls.append(l)
                accs.append(_dot(p.astype(BF16), vw))
            lo = _low_half((QB, LANES))
            if stride == 1:
                dst = qrows
            else:
                dst = pl.ds(jl * (QB * stride) + phase, QB, stride=stride)
            m_scr[b, dst, :] = jnp.where(lo, ms[0], ms[1])
            l_scr[b, dst, :] = jnp.where(lo, ls[0], ls[1])
            acc_scr[b, dst, :] = jnp.where(lo, accs[0], accs[1])
            return carry

        lax.fori_loop(0, NQB, block, 0, unroll=DIL_UNROLL)

    def merge(i, carry):
        rows = pl.ds(pl.multiple_of(i * PREP_ROWS, PREP_ROWS), PREP_ROWS)
        m = [m_scr[b, rows, :] for b in range(3)]
        top = jnp.maximum(jnp.maximum(m[0], m[1]), m[2])
        num = jnp.zeros((PREP_ROWS, LANES), F32)
        den = jnp.zeros((PREP_ROWS, LANES), F32)
        for b in range(3):
            w = jnp.exp(m[b] - top)
            num = num + w * acc_scr[b, rows, :]
            den = den + w * l_scr[b, rows, :]
        o_ref[rows, :] = (num / den).astype(BF16)
        return carry

    lax.fori_loop(0, SEQ // PREP_ROWS, merge, 0)


def _params(semantics):
    return pltpu.CompilerParams(dimension_semantics=semantics, vmem_limit_bytes=VMEM_LIMIT)


def _resident(shape):
    return pl.BlockSpec(shape, lambda *_: (0,) * len(shape), pipeline_mode=pl.Buffered(1))


def _smem():
    return pl.BlockSpec(memory_space=pltpu.SMEM)


def _proj(x, gain, w):
    t = x.shape[0]
    return pl.pallas_call(
        _proj_kernel,
        out_shape=jax.ShapeDtypeStruct((t, IN_W), BF16),
        grid=(t // ROW_TILE,),
        in_specs=[pl.BlockSpec((ROW_TILE, D_MODEL), lambda i: (i, 0)),
                  _resident((1, D_MODEL)),
                  _resident((D_MODEL, IN_W))],
        out_specs=pl.BlockSpec((ROW_TILE, IN_W), lambda i: (i, 0)),
        compiler_params=_params(("parallel",)),
        name="proj",
    )(x, gain, w)


def _seq_block(col_block, width=LANES):
    return pl.BlockSpec((SEQ, width), col_block)


def _na(proj, gains, table):
    t = proj.shape[0]
    pairs = NA_W // LANES
    return pl.pallas_call(
        _na_kernel,
        out_shape=jax.ShapeDtypeStruct((t, NA_W), BF16),
        grid=(pairs, t // SEQ),
        in_specs=[_seq_block(lambda p, b: (b, NA_Q0 + p)),
                  _seq_block(lambda p, b: (b, NA_K0 + p)),
                  _seq_block(lambda p, b: (b, NA_V0 + p)),
                  _resident((2, LANES)),
                  pl.BlockSpec((2, NA_VARIANTS, QB, NA_WIN), lambda p, b: (p, 0, 0, 0))],
        out_specs=_seq_block(lambda p, b: (b, p)),
        scratch_shapes=[pltpu.VMEM((2, SEQ, LANES), BF16), pltpu.VMEM((SEQ, LANES), BF16)],
        compiler_params=_params(("arbitrary", "arbitrary")),
        name="na",
    )(proj, proj, proj, gains, table)


def _swa(proj, gains, slopes, sink):
    t = proj.shape[0]
    win = QB + 2 * SWA_HALF_WINDOW
    return pl.pallas_call(
        _swa_kernel,
        out_shape=jax.ShapeDtypeStruct((t, SWA_QW), BF16),
        grid=(t // SEQ,),
        in_specs=[_smem(), _smem(),
                  _seq_block(lambda b: (b, SWA_COL0 // SWA_QW), SWA_QW),
                  _seq_block(lambda b: (b, SWA_K0)),
                  _seq_block(lambda b: (b, SWA_V0)),
                  _resident((2, LANES))],
        out_specs=_seq_block(lambda b: (b, 0), SWA_QW),
        scratch_shapes=[pltpu.VMEM((SWA_Q_HEADS, SEQ, LANES), BF16), pltpu.VMEM((SEQ, LANES), BF16),
                        pltpu.VMEM((SWA_Q_HEADS, QB, win), F32)],
        compiler_params=_params(("parallel",)),
        name="swa",
    )(slopes, sink, proj, proj, proj, gains)


def _dil(proj, gains, slopes):
    t = proj.shape[0]
    pairs = DIL_W // LANES
    return pl.pallas_call(
        _dil_kernel,
        out_shape=jax.ShapeDtypeStruct((t, DIL_W), BF16),
        grid=(t // SEQ, pairs),
        in_specs=[_smem(),
                  _seq_block(lambda b, p: (b, DIL_Q0 + p)),
                  _seq_block(lambda b, p: (b, DIL_K0 + p)),
                  _seq_block(lambda b, p: (b, DIL_V0 + p)),
                  _resident((2, LANES))],
        out_specs=_seq_block(lambda b, p: (b, p)),
        scratch_shapes=[pltpu.VMEM((3, SEQ, LANES), F32),
                        pltpu.VMEM((3, 4, SEQ, LANES), BF16),
                        pltpu.VMEM((2, 2, 3, QB, DIL_WIN), F32),
                        pltpu.VMEM((2, QB, QB), F32),
                        pltpu.VMEM((3, SEQ, LANES), F32),
                        pltpu.VMEM((3, SEQ, LANES), F32),
                        pltpu.VMEM((3, SEQ, LANES), F32)],
        compiler_params=_params(("parallel", "parallel")),
        name="dil",
    )(slopes, proj, proj, proj, gains)


def _out(ya, yb, yc, x, gain, w):
    t = x.shape[0]
    row = lambda width: pl.BlockSpec((ROW_TILE, width), lambda i: (i, 0))
    return pl.pallas_call(
        _out_kernel,
        out_shape=jax.ShapeDtypeStruct((t, D_MODEL), F32),
        grid=(t // ROW_TILE,),
        in_specs=[row(NA_W), row(SWA_QW), row(DIL_W), row(D_MODEL),
                  _resident((1, MIX_W)), _resident((MIX_W, D_MODEL))],
        out_specs=row(D_MODEL),
        scratch_shapes=[pltpu.VMEM((ROW_TILE, MIX_W), BF16)],
        compiler_params=_params(("parallel",)),
        name="out",
    )(ya, yb, yc, x, gain, w)


def _ffn(x, gain, w_gu, w_down):
    t = x.shape[0]
    row = pl.BlockSpec((ROW_TILE, D_MODEL), lambda i: (i, 0))
    return pl.pallas_call(
        _ffn_kernel,
        out_shape=jax.ShapeDtypeStruct((t, D_MODEL), F32),
        grid=(t // ROW_TILE,),
        in_specs=[row, _resident((1, D_MODEL)),
                  _resident((D_MODEL, 2 * FFN_HIDDEN)), _resident((FFN_HIDDEN, D_MODEL))],
        out_specs=row,
        scratch_shapes=[pltpu.VMEM((ROW_TILE, D_MODEL), BF16)],
        compiler_params=_params(("parallel",)),
        name="ffn",
    )(x, gain, w_gu, w_down)


def _pair_gain(g):
    return jnp.concatenate([g, g], axis=-1).astype(F32)


def kernel(x, attn_norm, w_in, qk_gain, rpb, sink, out_gain, w_out, ffn_norm, w_gu, w_down):
    batch, seq, d = x.shape
    assert (seq, d) == (SEQ, D_MODEL)
    depth = w_in.shape[0]
    n_slopes = SWA_Q_HEADS + DIL_HEADS
    slopes = 2.0 ** (-8.0 * (jnp.arange(n_slopes, dtype=F32) + 1.0) / n_slopes)
    xf = x.reshape(batch * seq, d)
    for l in range(depth):
        proj = _proj(xf, attn_norm[l][None, :], w_in[l].astype(BF16))
        ya = _na(proj, _pair_gain(qk_gain[l, 0]), _na_bias_table(rpb[l]))
        yb = _swa(proj, _pair_gain(qk_gain[l, 1]), slopes, sink[l].astype(F32))
        yc = _dil(proj, _pair_gain(qk_gain[l, 2]), slopes)
        xf = _out(ya, yb, yc, xf, out_gain[l][None, :], w_out[l].astype(BF16))
        xf = _ffn(xf, ffn_norm[l][None, :], w_gu[l].astype(BF16), w_down[l].astype(BF16))
    return xf.reshape(batch, seq, d)
```

```python
import numpy as np
import jax
import jax.numpy as jnp
from jax import lax
from jax.experimental import pallas as pl
from jax.experimental.pallas import tpu as pltpu

D_MODEL = 1024
SEQ = 2048
HEAD_DIM = 64
GRID_W = 64
NA_HEADS = 4
NA_ROWS = 8
NA_COLS = 16
SWA_Q_HEADS = 6
SWA_KV_HEADS = 2
SWA_GROUP = SWA_Q_HEADS // SWA_KV_HEADS
SWA_HALF_WINDOW = 128
DIL_HEADS = 6
DIL_STRIDES = (1, 4, 16)
DIL_HALF = 64
NA_W = NA_HEADS * HEAD_DIM
SWA_QW = SWA_Q_HEADS * HEAD_DIM
SWA_KVW = SWA_KV_HEADS * HEAD_DIM
DIL_W = DIL_HEADS * HEAD_DIM
IN_W = 3 * NA_W + SWA_QW + 2 * SWA_KVW + 3 * DIL_W
MIX_W = NA_W + SWA_QW + DIL_W
FFN_HIDDEN = 2816
NORM_EPS = 1e-6
NEG_INF = -1e30
QK_SCALE = HEAD_DIM ** -0.5

LANES = 128
VMEM_LIMIT = 56 * 1024 * 1024

QB = 128
QBT = 256
PREP_ROWS = 256
NQB = SEQ // QB
ROW_TILE = 512
FFN_H_CHUNK = 1408
DIL_UNROLL = 4

PROJ_T_Q = NA_W + SWA_QW
PROJ_T_V = NA_W + SWA_KVW
PROJ_R_K = NA_W + SWA_KVW
PROJ_R = PROJ_R_K + 3 * DIL_W
PROJ_R_CHUNK = 768

NA_WIN = 768
NA_VARIANTS = 3
NA_LAST_START = SEQ - NA_WIN
SWA_WIN = QBT + 2 * SWA_HALF_WINDOW

F32 = jnp.float32
BF16 = jnp.bfloat16


def _dot(a, b):
    return jnp.dot(a, b, preferred_element_type=F32)


def _dot_nt(a, b):
    return lax.dot_general(a, b, (((1,), (1,)), ((), ())), preferred_element_type=F32)


def _dot_tn(a, b):
    return lax.dot_general(a, b, (((0,), (0,)), ((), ())), preferred_element_type=F32)


def _low_half(shape):
    return lax.broadcasted_iota(jnp.int32, shape, len(shape) - 1) < HEAD_DIM


def _pair_rms(x, gain):
    lo = _low_half(x.shape)
    sq = x * x
    s_lo = jnp.sum(jnp.where(lo, sq, 0.0), axis=-1, keepdims=True)
    s_hi = jnp.sum(jnp.where(lo, 0.0, sq), axis=-1, keepdims=True)
    ms = jnp.where(lo, s_lo, s_hi) * (1.0 / HEAD_DIM)
    return x * lax.rsqrt(ms + NORM_EPS) * gain


def _softmax_block(s):
    m = jnp.max(s, axis=-1, keepdims=True)
    p = jnp.exp(s - m)
    return m, p, jnp.sum(p, axis=-1, keepdims=True)


def _proj_kernel(x_ref, g_ref, wt_ref, wr_ref, gq_ref, gr_ref,
                 qa_ref, qb_ref, va_ref, vb_ref, k_ref, dil_ref):
    x = x_ref[...]
    ms = jnp.mean(x * x, axis=-1, keepdims=True)
    h = (x * lax.rsqrt(ms + NORM_EPS) * g_ref[...]).astype(BF16)
    tm = h.shape[0]

    qt = _dot_nt(wt_ref[0:PROJ_T_Q, :], h)
    q3 = qt.reshape(PROJ_T_Q // HEAD_DIM, HEAD_DIM, tm)
    q3 = q3 * lax.rsqrt(jnp.mean(q3 * q3, axis=1, keepdims=True) + NORM_EPS)
    qt = q3.reshape(PROJ_T_Q, tm) * gq_ref[...] * QK_SCALE
    qa_ref[...] = qt[0:NA_W].astype(BF16)
    qb_ref[...] = qt[NA_W:PROJ_T_Q].astype(BF16)
    vt = _dot_nt(wt_ref[PROJ_T_Q:PROJ_T_Q + PROJ_T_V, :], h)
    va_ref[...] = vt[0:NA_W].astype(BF16)
    vb_ref[...] = vt[NA_W:PROJ_T_V].astype(BF16)

    n_norm = (PROJ_R_K + 2 * DIL_W) // LANES
    n_key = PROJ_R_K // LANES
    n_dil_q = DIL_W // LANES
    for c in range(PROJ_R // PROJ_R_CHUNK):
        r = _dot(h, wr_ref[:, c * PROJ_R_CHUNK:(c + 1) * PROJ_R_CHUNK])
        for t in range(PROJ_R_CHUNK // LANES):
            blk = c * (PROJ_R_CHUNK // LANES) + t
            tile = r[:, t * LANES:(t + 1) * LANES]
            if blk < n_norm:
                tile = _pair_rms(tile, gr_ref[:, blk * LANES:(blk + 1) * LANES])
            if n_key <= blk < n_key + n_dil_q:
                tile = tile * QK_SCALE
            if blk < n_key:
                k_ref[:, blk * LANES:(blk + 1) * LANES] = tile.astype(BF16)
            else:
                dil_ref[:, (blk - n_key) * LANES:(blk - n_key + 1) * LANES] = tile.astype(BF16)


def _group_rms_t(yt, gain_col):
    yf = yt.astype(F32)
    ms = jnp.mean(yf * yf, axis=0, keepdims=True)
    return (yf * lax.rsqrt(ms + NORM_EPS) * gain_col).astype(BF16)


def _out_kernel(ya_ref, yb_ref, yc_ref, x_ref, gab_ref, gc_ref, w_ref, o_ref):
    ya = _group_rms_t(ya_ref[...], gab_ref[0:NA_W, :])
    yb = _group_rms_t(yb_ref[...], gab_ref[NA_W:NA_W + SWA_QW, :])
    yc = yc_ref[...].astype(F32)
    ms = jnp.mean(yc * yc, axis=-1, keepdims=True)
    yc = (yc * lax.rsqrt(ms + NORM_EPS) * gc_ref[...]).astype(BF16)
    acc = _dot_tn(ya, w_ref[0:NA_W, :])
    acc = acc + _dot_tn(yb, w_ref[NA_W:NA_W + SWA_QW, :])
    acc = acc + _dot(yc, w_ref[NA_W + SWA_QW:MIX_W, :])
    o_ref[...] = x_ref[...] + acc


def _ffn_kernel(x_ref, g_ref, wgu_ref, wd_ref, o_ref, h_scr):
    x = x_ref[...]
    ms = jnp.mean(x * x, axis=-1, keepdims=True)
    h_scr[...] = (x * lax.rsqrt(ms + NORM_EPS) * g_ref[...]).astype(BF16)
    o_ref[...] = x
    for c in range(FFN_HIDDEN // FFN_H_CHUNK):
        lo = c * FFN_H_CHUNK
        gate = _dot(h_scr[...], wgu_ref[:, lo:lo + FFN_H_CHUNK])
        up = _dot(h_scr[...], wgu_ref[:, FFN_HIDDEN + lo:FFN_HIDDEN + lo + FFN_H_CHUNK])
        act = (gate / (1.0 + jnp.exp(-gate)) * up).astype(BF16)
        o_ref[...] += _dot(act, wd_ref[lo:lo + FFN_H_CHUNK, :])


def _attend_t(chains):
    def scores(c):
        qt = c["q"]()
        zeros = jnp.zeros_like(qt)
        qpad = jnp.concatenate([qt, zeros] if c["half"] == 0 else [zeros, qt], axis=0)
        return _dot(c["k"](), qpad) + c["bias"]()

    def softmax(c, s):
        sink = c["sink"]
        m = jnp.max(s, axis=0, keepdims=True)
        if sink is not None:
            m = jnp.maximum(m, sink)
        p = jnp.exp(s - m)
        den = jnp.sum(p, axis=0, keepdims=True)
        if sink is not None:
            den = den + jnp.exp(sink - m)
        return p.astype(BF16), den

    def values(c, p, den):
        o = _dot(c["v"](), p)
        half = c["half"]
        c["store"](o[half * HEAD_DIM:(half + 1) * HEAD_DIM, :] * (1.0 / den))

    n = len(chains)
    s_next = {i: scores(chains[i]) for i in range(min(2, n))}
    p_next = {0: softmax(chains[0], s_next.pop(0))}
    for i in range(n):
        if i + 2 < n:
            s_next[i + 2] = scores(chains[i + 2])
        if i + 1 < n:
            p_next[i + 1] = softmax(chains[i + 1], s_next.pop(i + 1))
        values(chains[i], *p_next.pop(i))


def _na_kernel(q_ref, k_ref, v_ref, tbl_ref, o_ref):
    def store(rows, cols):
        def put(o):
            o_ref[rows, cols] = o.astype(BF16)
        return put

    chains = []
    for j in range(SEQ // QBT):
        q0 = j * QBT
        start = min(max(q0 - QBT, 0), NA_LAST_START)
        variant = (q0 - start) // QBT
        cols = slice(q0, q0 + QBT)
        keys = slice(start, start + NA_WIN)
        for h in range(NA_HEADS):
            rows = slice(h * HEAD_DIM, (h + 1) * HEAD_DIM)
            pair = slice((h // 2) * LANES, (h // 2 + 1) * LANES)
            chains.append(dict(
                half=h % 2, sink=None,
                q=lambda rows=rows, cols=cols: q_ref[rows, cols],
                k=lambda keys=keys, pair=pair: k_ref[keys, pair],
                v=lambda keys=keys, pair=pair: v_ref[pair, keys],
                bias=lambda h=h, variant=variant: tbl_ref[h, variant],
                store=store(rows, cols)))
    _attend_t(chains)


def _na_bias_table(rpb):
    rows = SEQ // GRID_W
    n_blocks = SEQ // QBT
    rep_block = np.array([0, 1, n_blocks - 1])
    start = np.clip(rep_block * QBT - QBT, 0, NA_LAST_START)
    assert np.array_equal((rep_block * QBT - start) // QBT, np.arange(NA_VARIANTS))
    qq = np.arange(QBT)
    kk = np.arange(NA_WIN)
    qrow = (rep_block[:, None] * QBT + qq[None, :]) // GRID_W
    qcol = qq % GRID_W
    krow = (start[:, None] + kk[None, :]) // GRID_W
    kcol = kk % GRID_W
    row_start = np.clip(qrow - NA_ROWS // 2, 0, rows - NA_ROWS)
    col_start = np.clip(qcol - NA_COLS // 2, 0, GRID_W - NA_COLS)
    row_ok = (krow[:, :, None] >= row_start[:, None, :]) & (krow[:, :, None] < row_start[:, None, :] + NA_ROWS)
    col_ok = (kcol[:, None] >= col_start[None, :]) & (kcol[:, None] < col_start[None, :] + NA_COLS)
    valid = row_ok & col_ok[None]
    row_idx = np.clip(krow[:, :, None] - qrow[:, None, :] + NA_ROWS - 1, 0, 2 * NA_ROWS - 2)
    row_idx = row_idx[:, ::GRID_W, ::GRID_W]
    heads, n_row_rel, n_col_rel = rpb.shape
    period = 2 * GRID_W - 1
    lead = GRID_W - NA_COLS
    seq = jnp.pad(rpb.astype(F32), ((0, 0), (0, 0), (lead, period - lead - n_col_rel)))
    skew = jnp.tile(seq, (1, 1, GRID_W + 1))[..., :GRID_W * 2 * GRID_W]
    toe = skew.reshape(heads, n_row_rel, GRID_W, 2 * GRID_W)[:, :, ::-1, :GRID_W]
    tiles = jnp.stack([toe[:, a] for a in row_idx.reshape(-1)], axis=1)
    tiles = tiles.reshape(heads, NA_VARIANTS, NA_WIN // GRID_W, QBT // GRID_W, GRID_W, GRID_W)
    bias = tiles.transpose(0, 1, 2, 5, 3, 4).reshape(heads, NA_VARIANTS, NA_WIN, QBT)
    return jnp.where(valid[None], bias, NEG_INF)


def _swa_kernel(slopes_ref, sink_ref, q_ref, k_ref, v_ref, o_ref, bias_scr):
    @pl.when(pl.program_id(0) == 0)
    def _():
        kk = lax.broadcasted_iota(jnp.int32, (SWA_WIN, QBT), 0)
        qq = lax.broadcasted_iota(jnp.int32, (SWA_WIN, QBT), 1)
        dist = jnp.abs(kk - SWA_HALF_WINDOW - qq)
        in_window = dist <= SWA_HALF_WINDOW
        dist_f = dist.astype(F32)
        for h in range(SWA_Q_HEADS):
            bias_scr[h] = jnp.where(in_window, -slopes_ref[h] * dist_f, NEG_INF)

    def store(rows, cols):
        def put(o):
            o_ref[rows, cols] = o.astype(BF16)
        return put

    chains = []
    for j in range(SEQ // QBT):
        q0 = j * QBT
        lo = max(q0 - SWA_HALF_WINDOW, 0)
        hi = min(q0 + QBT + SWA_HALF_WINDOW, SEQ)
        first = lo - (q0 - SWA_HALF_WINDOW)
        cols = slice(q0, q0 + QBT)
        keys = slice(lo, hi)
        win = slice(first, first + hi - lo)
        for h in range(SWA_Q_HEADS):
            rows = slice(h * HEAD_DIM, (h + 1) * HEAD_DIM)
            chains.append(dict(
                half=h // SWA_GROUP, sink=sink_ref[h],
                q=lambda rows=rows, cols=cols: q_ref[rows, cols],
                k=lambda keys=keys: k_ref[keys, :],
                v=lambda keys=keys: v_ref[:, keys],
                bias=lambda h=h, win=win: bias_scr[h, win, :],
                store=store(rows, cols)))
    _attend_t(chains)


DIL_WIN = QB + 2 * DIL_HALF


def _dil_kernel(slopes_ref, q_ref, k_ref, v_ref, o_ref,
                nat_scr, lay_scr, bias_scr, bias16_scr, m_scr, l_scr, acc_scr):
    pair = pl.program_id(1)
    slope = [slopes_ref[SWA_Q_HEADS + 2 * pair + h] for h in range(2)]

    qq = lax.broadcasted_iota(jnp.int32, (QB, DIL_WIN), 0)
    kk = lax.broadcasted_iota(jnp.int32, (QB, DIL_WIN), 1)
    for shift in range(3):
        dist = jnp.abs(kk - shift * DIL_HALF - qq)
        ok = dist <= DIL_HALF
        dist_f = dist.astype(F32)
        for h in range(2):
            for b, stride in enumerate(DIL_STRIDES[:2]):
                bias_scr[h, b, shift] = jnp.where(ok, -slope[h] * (dist_f * float(stride)), NEG_INF)
    qq = lax.broadcasted_iota(jnp.int32, (QB, QB), 0)
    kk = lax.broadcasted_iota(jnp.int32, (QB, QB), 1)
    dist = jnp.abs(kk - qq)
    for h in range(2):
        bias16_scr[h] = jnp.where(dist <= DIL_HALF, -slope[h] * (dist.astype(F32) * float(DIL_STRIDES[2])),
                                  NEG_INF)

    def prep(i, carry):
        rows = pl.ds(pl.multiple_of(i * PREP_ROWS, PREP_ROWS), PREP_ROWS)
        lo = _low_half((PREP_ROWS, LANES))
        q = q_ref[rows, :]
        nat_scr[0, rows, :] = q.astype(F32)
        nat_scr[1, rows, :] = k_ref[rows, :].astype(F32)
        nat_scr[2, rows, :] = v_ref[rows, :].astype(F32)
        lay_scr[0, 0, rows, :] = jnp.where(lo, q, jnp.zeros_like(q))
        lay_scr[0, 1, rows, :] = jnp.where(lo, jnp.zeros_like(q), q)
        lay_scr[0, 2, rows, :] = k_ref[rows, :]
        lay_scr[0, 3, rows, :] = v_ref[rows, :]
        return carry

    lax.fori_loop(0, SEQ // PREP_ROWS, prep, 0)

    for b, stride in enumerate(DIL_STRIDES[1:], start=1):
        sub = SEQ // stride
        for phase in range(stride):
            src = pl.ds(phase, sub, stride=stride)
            dst = slice(phase * sub, (phase + 1) * sub)
            lo = _low_half((sub, LANES))
            qn = nat_scr[0, src, :]
            lay_scr[b, 0, dst, :] = jnp.where(lo, qn, 0.0).astype(BF16)
            lay_scr[b, 1, dst, :] = jnp.where(lo, 0.0, qn).astype(BF16)
            lay_scr[b, 2, dst, :] = nat_scr[1, src, :].astype(BF16)
            lay_scr[b, 3, dst, :] = nat_scr[2, src, :].astype(BF16)

    for b, stride in enumerate(DIL_STRIDES):
        sub = SEQ // stride
        blocks_per_sub = sub // QB

        def block(j, carry, b=b, stride=stride, sub=sub, blocks_per_sub=blocks_per_sub):
            phase = j // blocks_per_sub
            jl = j % blocks_per_sub
            qrows = pl.ds(pl.multiple_of(j * QB, QB), QB)
            if blocks_per_sub == 1:
                n_keys = QB
                krows = qrows
            else:
                n_keys = DIL_WIN
                local = jnp.clip(jl * QB - DIL_HALF, 0, sub - DIL_WIN)
                shift = (jl * QB - local) // DIL_HALF
                krows = pl.ds(pl.multiple_of(phase * sub + local, DIL_HALF), n_keys)
            kw = lay_scr[b, 2, krows, :]
            vw = lay_scr[b, 3, krows, :]
            ms, ls, accs = [], [], []
            for h in range(2):
                bias = bias16_scr[h] if blocks_per_sub == 1 else bias_scr[h, b, shift]
                s = _dot_nt(lay_scr[b, h, qrows, :], kw) + bias
                m, p, l = _softmax_block(s)
                ms.append(m)
                ls.append(l)
                accs.append(_dot(p.astype(BF16), vw))
            lo = _low_half((QB, LANES))
            if stride == 1:
                dst = qrows
            else:
                dst = pl.ds(jl * (QB * stride) + phase, QB, stride=stride)
            m_scr[b, dst, :] = jnp.where(lo, ms[0], ms[1])
            l_scr[b, dst, :] = jnp.where(lo, ls[0], ls[1])
            acc_scr[b, dst, :] = jnp.where(lo, accs[0], accs[1])
            return carry

        lax.fori_loop(0, NQB, block, 0, unroll=DIL_UNROLL)

    def merge(i, carry):
        rows = pl.ds(pl.multiple_of(i * PREP_ROWS, PREP_ROWS), PREP_ROWS)
        m = [m_scr[b, rows, :] for b in range(3)]
        top = jnp.maximum(jnp.maximum(m[0], m[1]), m[2])
        num = jnp.zeros((PREP_ROWS, LANES), F32)
        den = jnp.zeros((PREP_ROWS, LANES), F32)
        for b in range(3):
            w = jnp.exp(m[b] - top)
            num = num + w * acc_scr[b, rows, :]
            den = den + w * l_scr[b, rows, :]
        o_ref[rows, :] = (num / den).astype(BF16)
        return carry

    lax.fori_loop(0, SEQ // PREP_ROWS, merge, 0)


def _params(semantics):
    return pltpu.CompilerParams(dimension_semantics=semantics, vmem_limit_bytes=VMEM_LIMIT)


def _resident(shape):
    return pl.BlockSpec(shape, lambda *_: (0,) * len(shape), pipeline_mode=pl.Buffered(1))


def _smem():
    return pl.BlockSpec(memory_space=pltpu.SMEM)


def _proj(x, gain, w_t, w_r, gain_q, gain_r):
    t = x.shape[0]
    feat = lambda n: pl.BlockSpec((n, ROW_TILE), lambda i: (0, i))
    tok = lambda n: pl.BlockSpec((ROW_TILE, n), lambda i: (i, 0))
    return pl.pallas_call(
        _proj_kernel,
        out_shape=[jax.ShapeDtypeStruct((NA_W, t), BF16), jax.ShapeDtypeStruct((SWA_QW, t), BF16),
                   jax.ShapeDtypeStruct((NA_W, t), BF16), jax.ShapeDtypeStruct((SWA_KVW, t), BF16),
                   jax.ShapeDtypeStruct((t, PROJ_R_K), BF16), jax.ShapeDtypeStruct((t, 3 * DIL_W), BF16)],
        grid=(t // ROW_TILE,),
        in_specs=[tok(D_MODEL), _resident((1, D_MODEL)),
                  _resident((PROJ_T_Q + PROJ_T_V, D_MODEL)), _resident((D_MODEL, PROJ_R)),
                  _resident((PROJ_T_Q, 1)), _resident((1, PROJ_R_K + 2 * DIL_W))],
        out_specs=[feat(NA_W), feat(SWA_QW), feat(NA_W), feat(SWA_KVW), tok(PROJ_R_K), tok(3 * DIL_W)],
        compiler_params=_params(("parallel",)),
        name="proj",
    )(x, gain, w_t, w_r, gain_q, gain_r)


def _na(q_t, keys, v_t, table):
    t = q_t.shape[1]
    feat = pl.BlockSpec((NA_W, SEQ), lambda b: (0, b))
    return pl.pallas_call(
        _na_kernel,
        out_shape=jax.ShapeDtypeStruct((NA_W, t), BF16),
        grid=(t // SEQ,),
        in_specs=[feat, pl.BlockSpec((SEQ, NA_W), lambda b: (b, 0)), feat,
                  _resident((NA_HEADS, NA_VARIANTS, NA_WIN, QBT))],
        out_specs=feat,
        compiler_params=_params(("parallel",)),
        name="na",
    )(q_t, keys, v_t, table)


def _swa(q_t, keys, v_t, slopes, sink):
    t = q_t.shape[1]
    return pl.pallas_call(
        _swa_kernel,
        out_shape=jax.ShapeDtypeStruct((SWA_QW, t), BF16),
        grid=(t // SEQ,),
        in_specs=[_smem(), _smem(),
                  pl.BlockSpec((SWA_QW, SEQ), lambda b: (0, b)),
                  pl.BlockSpec((SEQ, SWA_KVW), lambda b: (b, NA_W // SWA_KVW)),
                  pl.BlockSpec((SWA_KVW, SEQ), lambda b: (0, b))],
        out_specs=pl.BlockSpec((SWA_QW, SEQ), lambda b: (0, b)),
        scratch_shapes=[pltpu.VMEM((SWA_Q_HEADS, SWA_WIN, QBT), F32)],
        compiler_params=_params(("arbitrary",)),
        name="swa",
    )(slopes, sink, q_t, keys, v_t)


def _dil(qkv, slopes):
    t = qkv.shape[0]
    pairs = DIL_W // LANES
    seq_block = lambda first: pl.BlockSpec((SEQ, LANES), lambda b, p: (b, first + p))
    return pl.pallas_call(
        _dil_kernel,
        out_shape=jax.ShapeDtypeStruct((t, DIL_W), BF16),
        grid=(t // SEQ, pairs),
        in_specs=[_smem(), seq_block(0), seq_block(pairs), seq_block(2 * pairs)],
        out_specs=seq_block(0),
        scratch_shapes=[pltpu.VMEM((3, SEQ, LANES), F32),
                        pltpu.VMEM((3, 4, SEQ, LANES), BF16),
                        pltpu.VMEM((2, 2, 3, QB, DIL_WIN), F32),
                        pltpu.VMEM((2, QB, QB), F32),
                        pltpu.VMEM((3, SEQ, LANES), F32),
                        pltpu.VMEM((3, SEQ, LANES), F32),
                        pltpu.VMEM((3, SEQ, LANES), F32)],
        compiler_params=_params(("parallel", "parallel")),
        name="dil",
    )(slopes, qkv, qkv, qkv)


def _out(ya_t, yb_t, yc, x, gain_ab, gain_c, w):
    t = x.shape[0]
    feat = lambda n: pl.BlockSpec((n, ROW_TILE), lambda i: (0, i))
    tok = lambda n: pl.BlockSpec((ROW_TILE, n), lambda i: (i, 0))
    return pl.pallas_call(
        _out_kernel,
        out_shape=jax.ShapeDtypeStruct((t, D_MODEL), F32),
        grid=(t // ROW_TILE,),
        in_specs=[feat(NA_W), feat(SWA_QW), tok(DIL_W), tok(D_MODEL),
                  _resident((NA_W + SWA_QW, 1)), _resident((1, DIL_W)), _resident((MIX_W, D_MODEL))],
        out_specs=tok(D_MODEL),
        compiler_params=_params(("parallel",)),
        name="out",
    )(ya_t, yb_t, yc, x, gain_ab, gain_c, w)


def _ffn(x, gain, w_gu, w_down):
    t = x.shape[0]
    row = pl.BlockSpec((ROW_TILE, D_MODEL), lambda i: (i, 0))
    return pl.pallas_call(
        _ffn_kernel,
        out_shape=jax.ShapeDtypeStruct((t, D_MODEL), F32),
        grid=(t // ROW_TILE,),
        in_specs=[row, _resident((1, D_MODEL)),
                  _resident((D_MODEL, 2 * FFN_HIDDEN)), _resident((FFN_HIDDEN, D_MODEL))],
        out_specs=row,
        scratch_shapes=[pltpu.VMEM((ROW_TILE, D_MODEL), BF16)],
        compiler_params=_params(("parallel",)),
        name="ffn",
    )(x, gain, w_gu, w_down)


def _split_w_in(w):
    cuts = np.cumsum([NA_W, NA_W, NA_W, SWA_QW, SWA_KVW, SWA_KVW, DIL_W, DIL_W])
    qa, ka, va, qb, kb, vb, qc, kc, vc = jnp.split(w, [int(c) for c in cuts], axis=1)
    w_t = jnp.concatenate([qa, qb, va, vb], axis=1).T
    w_r = jnp.concatenate([ka, kb, qc, kc, vc], axis=1)
    return w_t.astype(BF16), w_r.astype(BF16)


def kernel(x, attn_norm, w_in, qk_gain, rpb, sink, out_gain, w_out, ffn_norm, w_gu, w_down):
    batch, seq, d = x.shape
    assert (seq, d) == (SEQ, D_MODEL)
    depth = w_in.shape[0]
    n_slopes = SWA_Q_HEADS + DIL_HEADS
    slopes = 2.0 ** (-8.0 * (jnp.arange(n_slopes, dtype=F32) + 1.0) / n_slopes)
    xf = x.reshape(batch * seq, d)
    for l in range(depth):
        g = qk_gain[l].astype(F32)
        gain_q = jnp.concatenate([jnp.tile(g[0, 0], NA_HEADS), jnp.tile(g[1, 0], SWA_Q_HEADS)])[:, None]
        gain_r = jnp.concatenate([jnp.tile(g[0, 1], NA_HEADS), jnp.tile(g[1, 1], SWA_KV_HEADS),
                                  jnp.tile(g[2, 0], DIL_HEADS), jnp.tile(g[2, 1], DIL_HEADS)])[None, :]
        w_t, w_r = _split_w_in(w_in[l])
        qa_t, qb_t, va_t, vb_t, keys, dil_qkv = _proj(xf, attn_norm[l][None, :], w_t, w_r, gain_q, gain_r)
        ya_t = _na(qa_t, keys, va_t, _na_bias_table(rpb[l]))
        yb_t = _swa(qb_t, keys, vb_t, slopes, sink[l].astype(F32))
        yc = _dil(dil_qkv, slopes)
        og = out_gain[l].astype(F32)
        xf = _out(ya_t, yb_t, yc, xf, og[:NA_W + SWA_QW, None], og[None, NA_W + SWA_QW:], w_out[l].astype(BF16))
        xf = _ffn(xf, ffn_norm[l][None, :], w_gu[l].astype(BF16), w_down[l].astype(BF16))
    return xf.reshape(batch, seq, d)
```

```python
import numpy as np
import jax
import jax.numpy as jnp
from jax import lax
from jax.experimental import pallas as pl
from jax.experimental.pallas import tpu as pltpu

D_MODEL = 1024
SEQ = 2048
HEAD_DIM = 64
GRID_W = 64
NA_HEADS = 4
NA_ROWS = 8
NA_COLS = 16
SWA_Q_HEADS = 6
SWA_KV_HEADS = 2
SWA_GROUP = SWA_Q_HEADS // SWA_KV_HEADS
SWA_HALF_WINDOW = 128
DIL_HEADS = 6
DIL_STRIDES = (1, 4, 16)
DIL_HALF = 64
NA_W = NA_HEADS * HEAD_DIM
SWA_QW = SWA_Q_HEADS * HEAD_DIM
SWA_KVW = SWA_KV_HEADS * HEAD_DIM
DIL_W = DIL_HEADS * HEAD_DIM
IN_W = 3 * NA_W + SWA_QW + 2 * SWA_KVW + 3 * DIL_W
MIX_W = NA_W + SWA_QW + DIL_W
FFN_HIDDEN = 2816
NORM_EPS = 1e-6
NEG_INF = -1e30
QK_SCALE = HEAD_DIM ** -0.5

LANES = 128
VMEM_LIMIT = 56 * 1024 * 1024

QB = 128
QBT = 256
PREP_ROWS = 256
NQB = SEQ // QB
ROW_TILE = 512
FFN_H_CHUNK = 1408
DIL_UNROLL = 4

PROJ_T_Q = NA_W + SWA_QW
PROJ_T_V = NA_W + SWA_KVW
PROJ_R_K = NA_W + SWA_KVW
PROJ_R = PROJ_R_K + 3 * DIL_W
PROJ_R_CHUNK = 768

NA_WIN = 768
NA_VARIANTS = 3
NA_LAST_START = SEQ - NA_WIN
SWA_WIN = QBT + 2 * SWA_HALF_WINDOW

F32 = jnp.float32
BF16 = jnp.bfloat16


def _dot(a, b):
    return jnp.dot(a, b, preferred_element_type=F32)


def _dot_nt(a, b):
    return lax.dot_general(a, b, (((1,), (1,)), ((), ())), preferred_element_type=F32)


def _dot_tn(a, b):
    return lax.dot_general(a, b, (((0,), (0,)), ((), ())), preferred_element_type=F32)


def _low_half(shape):
    return lax.broadcasted_iota(jnp.int32, shape, len(shape) - 1) < HEAD_DIM


def _pair_rms(x, gain):
    lo = _low_half(x.shape)
    sq = x * x
    s_lo = jnp.sum(jnp.where(lo, sq, 0.0), axis=-1, keepdims=True)
    s_hi = jnp.sum(jnp.where(lo, 0.0, sq), axis=-1, keepdims=True)
    ms = jnp.where(lo, s_lo, s_hi) * (1.0 / HEAD_DIM)
    return x * lax.rsqrt(ms + NORM_EPS) * gain


def _softmax_block(s):
    m = jnp.max(s, axis=-1, keepdims=True)
    p = jnp.exp(s - m)
    return m, p, jnp.sum(p, axis=-1, keepdims=True)


def _proj_kernel(x_ref, g_ref, wt_ref, wr_ref, gq_ref, gr_ref,
                 qa_ref, qb_ref, va_ref, vb_ref, k_ref, dil_ref):
    x = x_ref[...]
    ms = jnp.mean(x * x, axis=-1, keepdims=True)
    h = (x * lax.rsqrt(ms + NORM_EPS) * g_ref[...]).astype(BF16)
    tm = h.shape[0]

    qt = _dot_nt(wt_ref[0:PROJ_T_Q, :], h)
    q3 = qt.reshape(PROJ_T_Q // HEAD_DIM, HEAD_DIM, tm)
    q3 = q3 * lax.rsqrt(jnp.mean(q3 * q3, axis=1, keepdims=True) + NORM_EPS)
    qt = q3.reshape(PROJ_T_Q, tm) * gq_ref[...] * QK_SCALE
    qa_ref[...] = qt[0:NA_W].astype(BF16)
    qb_ref[...] = qt[NA_W:PROJ_T_Q].astype(BF16)
    vt = _dot_nt(wt_ref[PROJ_T_Q:PROJ_T_Q + PROJ_T_V, :], h)
    va_ref[...] = vt[0:NA_W].astype(BF16)
    vb_ref[...] = vt[NA_W:PROJ_T_V].astype(BF16)

    n_norm = (PROJ_R_K + 2 * DIL_W) // LANES
    n_key = PROJ_R_K // LANES
    n_dil_q = DIL_W // LANES
    for c in range(PROJ_R // PROJ_R_CHUNK):
        r = _dot(h, wr_ref[:, c * PROJ_R_CHUNK:(c + 1) * PROJ_R_CHUNK])
        for t in range(PROJ_R_CHUNK // LANES):
            blk = c * (PROJ_R_CHUNK // LANES) + t
            tile = r[:, t * LANES:(t + 1) * LANES]
            if blk < n_norm:
                tile = _pair_rms(tile, gr_ref[:, blk * LANES:(blk + 1) * LANES])
            if n_key <= blk < n_key + n_dil_q:
                tile = tile * QK_SCALE
            if blk < n_key:
                k_ref[:, blk * LANES:(blk + 1) * LANES] = tile.astype(BF16)
            else:
                dil_ref[:, (blk - n_key) * LANES:(blk - n_key + 1) * LANES] = tile.astype(BF16)


def _group_rms_t(yt, gain_col):
    yf = yt.astype(F32)
    ms = jnp.mean(yf * yf, axis=0, keepdims=True)
    return (yf * lax.rsqrt(ms + NORM_EPS) * gain_col).astype(BF16)


def _out_kernel(ya_ref, yb_ref, yc_ref, x_ref, gab_ref, gc_ref, w_ref, o_ref):
    ya = _group_rms_t(ya_ref[...], gab_ref[0:NA_W, :])
    yb = _group_rms_t(yb_ref[...], gab_ref[NA_W:NA_W + SWA_QW, :])
    yc = yc_ref[...].astype(F32)
    ms = jnp.mean(yc * yc, axis=-1, keepdims=True)
    yc = (yc * lax.rsqrt(ms + NORM_EPS) * gc_ref[...]).astype(BF16)
    acc = _dot_tn(ya, w_ref[0:NA_W, :])
    acc = acc + _dot_tn(yb, w_ref[NA_W:NA_W + SWA_QW, :])
    acc = acc + _dot(yc, w_ref[NA_W + SWA_QW:MIX_W, :])
    o_ref[...] = x_ref[...] + acc


def _ffn_kernel(x_ref, g_ref, wgu_ref, wd_ref, o_ref, h_scr):
    x = x_ref[...]
    ms = jnp.mean(x * x, axis=-1, keepdims=True)
    h_scr[...] = (x * lax.rsqrt(ms + NORM_EPS) * g_ref[...]).astype(BF16)
    o_ref[...] = x
    for c in range(FFN_HIDDEN // FFN_H_CHUNK):
        lo = c * FFN_H_CHUNK
        gate = _dot(h_scr[...], wgu_ref[:, lo:lo + FFN_H_CHUNK])
        up = _dot(h_scr[...], wgu_ref[:, FFN_HIDDEN + lo:FFN_HIDDEN + lo + FFN_H_CHUNK])
        act = (gate / (1.0 + jnp.exp(-gate)) * up).astype(BF16)
        o_ref[...] += _dot(act, wd_ref[lo:lo + FFN_H_CHUNK, :])


def _pipeline3(chains, stage_a, stage_b, stage_c):
    n = len(chains)
    a = {i: stage_a(chains[i]) for i in range(min(2, n))}
    b = {0: stage_b(chains[0], a.pop(0))}
    for i in range(n):
        if i + 2 < n:
            a[i + 2] = stage_a(chains[i + 2])
        if i + 1 < n:
            b[i + 1] = stage_b(chains[i + 1], a.pop(i + 1))
        stage_c(chains[i], b.pop(i))


def _attend_t(chains):
    def scores(c):
        qt = c["q"]()
        zeros = jnp.zeros_like(qt)
        qpad = jnp.concatenate([qt, zeros] if c["half"] == 0 else [zeros, qt], axis=0)
        return _dot(c["k"](), qpad) + c["bias"]()

    def softmax(c, s):
        sink = c["sink"]
        m = jnp.max(s, axis=0, keepdims=True)
        if sink is not None:
            m = jnp.maximum(m, sink)
        p = jnp.exp(s - m)
        den = jnp.sum(p, axis=0, keepdims=True)
        if sink is not None:
            den = den + jnp.exp(sink - m)
        return p.astype(BF16), den

    def values(c, p, den):
        o = _dot(c["v"](), p)
        half = c["half"]
        c["store"](o[half * HEAD_DIM:(half + 1) * HEAD_DIM, :] * (1.0 / den))

    _pipeline3(chains, scores, softmax, lambda c, pd: values(c, *pd))


def _na_kernel(q_ref, k_ref, v_ref, tbl_ref, o_ref):
    def store(rows, cols):
        def put(o):
            o_ref[rows, cols] = o.astype(BF16)
        return put

    chains = []
    for j in range(SEQ // QBT):
        q0 = j * QBT
        start = min(max(q0 - QBT, 0), NA_LAST_START)
        variant = (q0 - start) // QBT
        cols = slice(q0, q0 + QBT)
        keys = slice(start, start + NA_WIN)
        for h in range(NA_HEADS):
            rows = slice(h * HEAD_DIM, (h + 1) * HEAD_DIM)
            pair = slice((h // 2) * LANES, (h // 2 + 1) * LANES)
            chains.append(dict(
                half=h % 2, sink=None,
                q=lambda rows=rows, cols=cols: q_ref[rows, cols],
                k=lambda keys=keys, pair=pair: k_ref[keys, pair],
                v=lambda keys=keys, pair=pair: v_ref[pair, keys],
                bias=lambda h=h, variant=variant: tbl_ref[h, variant],
                store=store(rows, cols)))
    _attend_t(chains)


def _na_bias_table(rpb):
    rows = SEQ // GRID_W
    n_blocks = SEQ // QBT
    rep_block = np.array([0, 1, n_blocks - 1])
    start = np.clip(rep_block * QBT - QBT, 0, NA_LAST_START)
    assert np.array_equal((rep_block * QBT - start) // QBT, np.arange(NA_VARIANTS))
    qq = np.arange(QBT)
    kk = np.arange(NA_WIN)
    qrow = (rep_block[:, None] * QBT + qq[None, :]) // GRID_W
    qcol = qq % GRID_W
    krow = (start[:, None] + kk[None, :]) // GRID_W
    kcol = kk % GRID_W
    row_start = np.clip(qrow - NA_ROWS // 2, 0, rows - NA_ROWS)
    col_start = np.clip(qcol - NA_COLS // 2, 0, GRID_W - NA_COLS)
    row_ok = (krow[:, :, None] >= row_start[:, None, :]) & (krow[:, :, None] < row_start[:, None, :] + NA_ROWS)
    col_ok = (kcol[:, None] >= col_start[None, :]) & (kcol[:, None] < col_start[None, :] + NA_COLS)
    valid = row_ok & col_ok[None]
    row_idx = np.clip(krow[:, :, None] - qrow[:, None, :] + NA_ROWS - 1, 0, 2 * NA_ROWS - 2)
    row_idx = row_idx[:, ::GRID_W, ::GRID_W]
    heads, n_row_rel, n_col_rel = rpb.shape
    period = 2 * GRID_W - 1
    lead = GRID_W - NA_COLS
    seq = jnp.pad(rpb.astype(F32), ((0, 0), (0, 0), (lead, period - lead - n_col_rel)))
    skew = jnp.tile(seq, (1, 1, GRID_W + 1))[..., :GRID_W * 2 * GRID_W]
    toe = skew.reshape(heads, n_row_rel, GRID_W, 2 * GRID_W)[:, :, ::-1, :GRID_W]
    tiles = jnp.stack([toe[:, a] for a in row_idx.reshape(-1)], axis=1)
    tiles = tiles.reshape(heads, NA_VARIANTS, NA_WIN // GRID_W, QBT // GRID_W, GRID_W, GRID_W)
    bias = tiles.transpose(0, 1, 2, 5, 3, 4).reshape(heads, NA_VARIANTS, NA_WIN, QBT)
    return jnp.where(valid[None], bias, NEG_INF)


def _swa_kernel(slopes_ref, sink_ref, q_ref, k_ref, v_ref, o_ref, bias_scr):
    @pl.when(pl.program_id(0) == 0)
    def _():
        kk = lax.broadcasted_iota(jnp.int32, (SWA_WIN, QBT), 0)
        qq = lax.broadcasted_iota(jnp.int32, (SWA_WIN, QBT), 1)
        dist = jnp.abs(kk - SWA_HALF_WINDOW - qq)
        in_window = dist <= SWA_HALF_WINDOW
        dist_f = dist.astype(F32)
        for h in range(SWA_Q_HEADS):
            bias_scr[h] = jnp.where(in_window, -slopes_ref[h] * dist_f, NEG_INF)

    def store(rows, cols):
        def put(o):
            o_ref[rows, cols] = o.astype(BF16)
        return put

    chains = []
    for j in range(SEQ // QBT):
        q0 = j * QBT
        lo = max(q0 - SWA_HALF_WINDOW, 0)
        hi = min(q0 + QBT + SWA_HALF_WINDOW, SEQ)
        first = lo - (q0 - SWA_HALF_WINDOW)
        cols = slice(q0, q0 + QBT)
        keys = slice(lo, hi)
        win = slice(first, first + hi - lo)
        for h in range(SWA_Q_HEADS):
            rows = slice(h * HEAD_DIM, (h + 1) * HEAD_DIM)
            chains.append(dict(
                half=h // SWA_GROUP, sink=sink_ref[h],
                q=lambda rows=rows, cols=cols: q_ref[rows, cols],
                k=lambda keys=keys: k_ref[keys, :],
                v=lambda keys=keys: v_ref[:, keys],
                bias=lambda h=h, win=win: bias_scr[h, win, :],
                store=store(rows, cols)))
    _attend_t(chains)


DIL_WIN = QB + 2 * DIL_HALF


def _dil_kernel(slopes_ref, q_ref, k_ref, v_ref, o_ref,
                nat_scr, lay_scr, bias_scr, bias16_scr, m_scr, l_scr, acc_scr):
    pair = pl.program_id(1)
    slope = [slopes_ref[SWA_Q_HEADS + 2 * pair + h] for h in range(2)]

    qq = lax.broadcasted_iota(jnp.int32, (QB, DIL_WIN), 0)
    kk = lax.broadcasted_iota(jnp.int32, (QB, DIL_WIN), 1)
    for shift in range(3):
        dist = jnp.abs(kk - shift * DIL_HALF - qq)
        ok = dist <= DIL_HALF
        dist_f = dist.astype(F32)
        for h in range(2):
            for b, stride in enumerate(DIL_STRIDES[:2]):
                bias_scr[h, b, shift] = jnp.where(ok, -slope[h] * (dist_f * float(stride)), NEG_INF)
    qq = lax.broadcasted_iota(jnp.int32, (QB, QB), 0)
    kk = lax.broadcasted_iota(jnp.int32, (QB, QB), 1)
    dist = jnp.abs(kk - qq)
    for h in range(2):
        bias16_scr[h] = jnp.where(dist <= DIL_HALF, -slope[h] * (dist.astype(F32) * float(DIL_STRIDES[2])),
                                  NEG_INF)

    def prep(i, carry):
        rows = pl.ds(pl.multiple_of(i * PREP_ROWS, PREP_ROWS), PREP_ROWS)
        lo = _low_half((PREP_ROWS, LANES))
        q = q_ref[rows, :]
        nat_scr[0, rows, :] = q.astype(F32)
        nat_scr[1, rows, :] = k_ref[rows, :].astype(F32)
        nat_scr[2, rows, :] = v_ref[rows, :].astype(F32)
        lay_scr[0, 0, rows, :] = jnp.where(lo, q, jnp.zeros_like(q))
        lay_scr[0, 1, rows, :] = jnp.where(lo, jnp.zeros_like(q), q)
        lay_scr[0, 2, rows, :] = k_ref[rows, :]
        lay_scr[0, 3, rows, :] = v_ref[rows, :]
        return carry

    lax.fori_loop(0, SEQ // PREP_ROWS, prep, 0)

    for b, stride in enumerate(DIL_STRIDES[1:], start=1):
        sub = SEQ // stride
        for phase in range(stride):
            src = pl.ds(phase, sub, stride=stride)
            dst = slice(phase * sub, (phase + 1) * sub)
            lo = _low_half((sub, LANES))
            qn = nat_scr[0, src, :]
            lay_scr[b, 0, dst, :] = jnp.where(lo, qn, 0.0).astype(BF16)
            lay_scr[b, 1, dst, :] = jnp.where(lo, 0.0, qn).astype(BF16)
            lay_scr[b, 2, dst, :] = nat_scr[1, src, :].astype(BF16)
            lay_scr[b, 3, dst, :] = nat_scr[2, src, :].astype(BF16)

    chains = []
    for j in range(NQB):
        for b, stride in enumerate(DIL_STRIDES):
            sub = SEQ // stride
            phase, local_q = divmod(j * QB, sub)
            if sub == QB:
                key_start, n_keys, shift = j * QB, QB, None
            else:
                local_k = min(max(local_q - DIL_HALF, 0), sub - DIL_WIN)
                key_start, n_keys, shift = phase * sub + local_k, DIL_WIN, (local_q - local_k) // DIL_HALF
            if stride == 1:
                dst = slice(j * QB, (j + 1) * QB)
            else:
                dst = pl.ds(local_q * stride + phase, QB, stride=stride)
            chains.append(dict(b=b, qrows=slice(j * QB, (j + 1) * QB),
                               krows=slice(key_start, key_start + n_keys), shift=shift, dst=dst))

    def scores(c):
        kw = lay_scr[c["b"], 2, c["krows"], :]
        out = []
        for h in range(2):
            bias = bias16_scr[h] if c["shift"] is None else bias_scr[h, c["b"], c["shift"]]
            out.append(_dot_nt(lay_scr[c["b"], h, c["qrows"], :], kw) + bias)
        return out

    def softmax(c, scores_pair):
        out = []
        for s in scores_pair:
            m, p, l = _softmax_block(s)
            out.append((m, p.astype(BF16), l))
        return out

    def values(c, parts):
        vw = lay_scr[c["b"], 3, c["krows"], :]
        accs = [_dot(p, vw) for _, p, _ in parts]
        lo = _low_half((QB, LANES))
        m_scr[c["b"], c["dst"], :] = jnp.where(lo, parts[0][0], parts[1][0])
        l_scr[c["b"], c["dst"], :] = jnp.where(lo, parts[0][2], parts[1][2])
        acc_scr[c["b"], c["dst"], :] = jnp.where(lo, accs[0], accs[1])

    _pipeline3(chains, scores, softmax, values)

    def merge(i, carry):
        rows = pl.ds(pl.multiple_of(i * PREP_ROWS, PREP_ROWS), PREP_ROWS)
        m = [m_scr[b, rows, :] for b in range(3)]
        top = jnp.maximum(jnp.maximum(m[0], m[1]), m[2])
        num = jnp.zeros((PREP_ROWS, LANES), F32)
        den = jnp.zeros((PREP_ROWS, LANES), F32)
        for b in range(3):
            w = jnp.exp(m[b] - top)
            num = num + w * acc_scr[b, rows, :]
            den = den + w * l_scr[b, rows, :]
        o_ref[rows, :] = (num / den).astype(BF16)
        return carry

    lax.fori_loop(0, SEQ // PREP_ROWS, merge, 0)


def _params(semantics):
    return pltpu.CompilerParams(dimension_semantics=semantics, vmem_limit_bytes=VMEM_LIMIT)


def _resident(shape):
    return pl.BlockSpec(shape, lambda *_: (0,) * len(shape), pipeline_mode=pl.Buffered(1))


def _smem():
    return pl.BlockSpec(memory_space=pltpu.SMEM)


def _proj(x, gain, w_t, w_r, gain_q, gain_r):
    t = x.shape[0]
    feat = lambda n: pl.BlockSpec((n, ROW_TILE), lambda i: (0, i))
    tok = lambda n: pl.BlockSpec((ROW_TILE, n), lambda i: (i, 0))
    return pl.pallas_call(
        _proj_kernel,
        out_shape=[jax.ShapeDtypeStruct((NA_W, t), BF16), jax.ShapeDtypeStruct((SWA_QW, t), BF16),
                   jax.ShapeDtypeStruct((NA_W, t), BF16), jax.ShapeDtypeStruct((SWA_KVW, t), BF16),
                   jax.ShapeDtypeStruct((t, PROJ_R_K), BF16), jax.ShapeDtypeStruct((t, 3 * DIL_W), BF16)],
        grid=(t // ROW_TILE,),
        in_specs=[tok(D_MODEL), _resident((1, D_MODEL)),
                  _resident((PROJ_T_Q + PROJ_T_V, D_MODEL)), _resident((D_MODEL, PROJ_R)),
                  _resident((PROJ_T_Q, 1)), _resident((1, PROJ_R_K + 2 * DIL_W))],
        out_specs=[feat(NA_W), feat(SWA_QW), feat(NA_W), feat(SWA_KVW), tok(PROJ_R_K), tok(3 * DIL_W)],
        compiler_params=_params(("parallel",)),
        name="proj",
    )(x, gain, w_t, w_r, gain_q, gain_r)


def _na(q_t, keys, v_t, table):
    t = q_t.shape[1]
    feat = pl.BlockSpec((NA_W, SEQ), lambda b: (0, b))
    return pl.pallas_call(
        _na_kernel,
        out_shape=jax.ShapeDtypeStruct((NA_W, t), BF16),
        grid=(t // SEQ,),
        in_specs=[feat, pl.BlockSpec((SEQ, NA_W), lambda b: (b, 0)), feat,
                  _resident((NA_HEADS, NA_VARIANTS, NA_WIN, QBT))],
        out_specs=feat,
        compiler_params=_params(("parallel",)),
        name="na",
    )(q_t, keys, v_t, table)


def _swa(q_t, keys, v_t, slopes, sink):
    t = q_t.shape[1]
    return pl.pallas_call(
        _swa_kernel,
        out_shape=jax.ShapeDtypeStruct((SWA_QW, t), BF16),
        grid=(t // SEQ,),
        in_specs=[_smem(), _smem(),
                  pl.BlockSpec((SWA_QW, SEQ), lambda b: (0, b)),
                  pl.BlockSpec((SEQ, SWA_KVW), lambda b: (b, NA_W // SWA_KVW)),
                  pl.BlockSpec((SWA_KVW, SEQ), lambda b: (0, b))],
        out_specs=pl.BlockSpec((SWA_QW, SEQ), lambda b: (0, b)),
        scratch_shapes=[pltpu.VMEM((SWA_Q_HEADS, SWA_WIN, QBT), F32)],
        compiler_params=_params(("arbitrary",)),
        name="swa",
    )(slopes, sink, q_t, keys, v_t)


def _dil(qkv, slopes):
    t = qkv.shape[0]
    pairs = DIL_W // LANES
    seq_block = lambda first: pl.BlockSpec((SEQ, LANES), lambda b, p: (b, first + p))
    return pl.pallas_call(
        _dil_kernel,
        out_shape=jax.ShapeDtypeStruct((t, DIL_W), BF16),
        grid=(t // SEQ, pairs),
        in_specs=[_smem(), seq_block(0), seq_block(pairs), seq_block(2 * pairs)],
        out_specs=seq_block(0),
        scratch_shapes=[pltpu.VMEM((3, SEQ, LANES), F32),
                        pltpu.VMEM((3, 4, SEQ, LANES), BF16),
                        pltpu.VMEM((2, 2, 3, QB, DIL_WIN), F32),
                        pltpu.VMEM((2, QB, QB), F32),
                        pltpu.VMEM((3, SEQ, LANES), F32),
                        pltpu.VMEM((3, SEQ, LANES), F32),
                        pltpu.VMEM((3, SEQ, LANES), F32)],
        compiler_params=_params(("parallel", "parallel")),
        name="dil",
    )(slopes, qkv, qkv, qkv)


def _out(ya_t, yb_t, yc, x, gain_ab, gain_c, w):
    t = x.shape[0]
    feat = lambda n: pl.BlockSpec((n, ROW_TILE), lambda i: (0, i))
    tok = lambda n: pl.BlockSpec((ROW_TILE, n), lambda i: (i, 0))
    return pl.pallas_call(
        _out_kernel,
        out_shape=jax.ShapeDtypeStruct((t, D_MODEL), F32),
        grid=(t // ROW_TILE,),
        in_specs=[feat(NA_W), feat(SWA_QW), tok(DIL_W), tok(D_MODEL),
                  _resident((NA_W + SWA_QW, 1)), _resident((1, DIL_W)), _resident((MIX_W, D_MODEL))],
        out_specs=tok(D_MODEL),
        compiler_params=_params(("parallel",)),
        name="out",
    )(ya_t, yb_t, yc, x, gain_ab, gain_c, w)


def _ffn(x, gain, w_gu, w_down):
    t = x.shape[0]
    row = pl.BlockSpec((ROW_TILE, D_MODEL), lambda i: (i, 0))
    return pl.pallas_call(
        _ffn_kernel,
        out_shape=jax.ShapeDtypeStruct((t, D_MODEL), F32),
        grid=(t // ROW_TILE,),
        in_specs=[row, _resident((1, D_MODEL)),
                  _resident((D_MODEL, 2 * FFN_HIDDEN)), _resident((FFN_HIDDEN, D_MODEL))],
        out_specs=row,
        scratch_shapes=[pltpu.VMEM((ROW_TILE, D_MODEL), BF16)],
        compiler_params=_params(("parallel",)),
        name="ffn",
    )(x, gain, w_gu, w_down)


def _split_w_in(w):
    cuts = np.cumsum([NA_W, NA_W, NA_W, SWA_QW, SWA_KVW, SWA_KVW, DIL_W, DIL_W])
    qa, ka, va, qb, kb, vb, qc, kc, vc = jnp.split(w, [int(c) for c in cuts], axis=1)
    w_t = jnp.concatenate([qa, qb, va, vb], axis=1).T
    w_r = jnp.concatenate([ka, kb, qc, kc, vc], axis=1)
    return w_t.astype(BF16), w_r.astype(BF16)


def kernel(x, attn_norm, w_in, qk_gain, rpb, sink, out_gain, w_out, ffn_norm, w_gu, w_down):
    batch, seq, d = x.shape
    assert (seq, d) == (SEQ, D_MODEL)
    depth = w_in.shape[0]
    n_slopes = SWA_Q_HEADS + DIL_HEADS
    slopes = 2.0 ** (-8.0 * (jnp.arange(n_slopes, dtype=F32) + 1.0) / n_slopes)
    xf = x.reshape(batch * seq, d)
    for l in range(depth):
        g = qk_gain[l].astype(F32)
        gain_q = jnp.concatenate([jnp.tile(g[0, 0], NA_HEADS), jnp.tile(g[1, 0], SWA_Q_HEADS)])[:, None]
        gain_r = jnp.concatenate([jnp.tile(g[0, 1], NA_HEADS), jnp.tile(g[1, 1], SWA_KV_HEADS),
                                  jnp.tile(g[2, 0], DIL_HEADS), jnp.tile(g[2, 1], DIL_HEADS)])[None, :]
        w_t, w_r = _split_w_in(w_in[l])
        qa_t, qb_t, va_t, vb_t, keys, dil_qkv = _proj(xf, attn_norm[l][None, :], w_t, w_r, gain_q, gain_r)
        ya_t = _na(qa_t, keys, va_t, _na_bias_table(rpb[l]))
        yb_t = _swa(qb_t, keys, vb_t, slopes, sink[l].astype(F32))
        yc = _dil(dil_qkv, slopes)
        og = out_gain[l].astype(F32)
        xf = _out(ya_t, yb_t, yc, xf, og[:NA_W + SWA_QW, None], og[None, NA_W + SWA_QW:], w_out[l].astype(BF16))
        xf = _ffn(xf, ffn_norm[l][None, :], w_gu[l].astype(BF16), w_down[l].astype(BF16))
    return xf.reshape(batch, seq, d)
```

```python
import math

import numpy as np
import jax
import jax.numpy as jnp
from jax import lax
from jax.experimental import pallas as pl
from jax.experimental.pallas import tpu as pltpu

D_MODEL = 1024
SEQ = 2048
HEAD_DIM = 64
GRID_W = 64
NA_HEADS = 4
NA_ROWS = 8
NA_COLS = 16
SWA_Q_HEADS = 6
SWA_KV_HEADS = 2
SWA_GROUP = SWA_Q_HEADS // SWA_KV_HEADS
SWA_HALF_WINDOW = 128
DIL_HEADS = 6
DIL_STRIDES = (1, 4, 16)
DIL_HALF = 64
NA_W = NA_HEADS * HEAD_DIM
SWA_QW = SWA_Q_HEADS * HEAD_DIM
SWA_KVW = SWA_KV_HEADS * HEAD_DIM
DIL_W = DIL_HEADS * HEAD_DIM
IN_W = 3 * NA_W + SWA_QW + 2 * SWA_KVW + 3 * DIL_W
MIX_W = NA_W + SWA_QW + DIL_W
FFN_HIDDEN = 2816
NORM_EPS = 1e-6
NEG_INF = -1e30
QK_SCALE = HEAD_DIM ** -0.5
LOG2E = math.log2(math.e)
Q_SCALE = QK_SCALE * LOG2E

LANES = 128
VMEM_LIMIT = 56 * 1024 * 1024

QB = 128
QBT = 256
PREP_ROWS = 256
NQB = SEQ // QB
ROW_TILE = 512
MXU_DIM = 256
FFN_H_CHUNKS = (6 * MXU_DIM, 5 * MXU_DIM)
assert sum(FFN_H_CHUNKS) == FFN_HIDDEN

PROJ_T_Q = NA_W + SWA_QW
PROJ_T_V = NA_W + SWA_KVW
PROJ_R_K = NA_W + SWA_KVW
PROJ_R = PROJ_R_K + 3 * DIL_W
PROJ_R_CHUNK = 768

NA_WIN = 768
NA_VARIANTS = 3
NA_LAST_START = SEQ - NA_WIN
SWA_WIN = QBT + 2 * SWA_HALF_WINDOW

F32 = jnp.float32
BF16 = jnp.bfloat16


def _dot(a, b):
    return jnp.dot(a, b, preferred_element_type=F32)


def _dot_nt(a, b):
    return lax.dot_general(a, b, (((1,), (1,)), ((), ())), preferred_element_type=F32)


def _dot_tn(a, b):
    return lax.dot_general(a, b, (((0,), (0,)), ((), ())), preferred_element_type=F32)


def _low_half(shape):
    return lax.broadcasted_iota(jnp.int32, shape, len(shape) - 1) < HEAD_DIM


def _pair_rms(x, gain):
    lo = _low_half(x.shape)
    sq = x * x
    s_lo = jnp.sum(jnp.where(lo, sq, 0.0), axis=-1, keepdims=True)
    s_hi = jnp.sum(jnp.where(lo, 0.0, sq), axis=-1, keepdims=True)
    ms = jnp.where(lo, s_lo, s_hi) * (1.0 / HEAD_DIM)
    return x * lax.rsqrt(ms + NORM_EPS) * gain


def _softmax_block(s):
    m = jnp.max(s, axis=-1, keepdims=True)
    return m, jnp.exp2(s - m).astype(BF16)


def _proj_kernel(x_ref, g_ref, wt_ref, wr_ref, gq_ref, gr_ref,
                 qa_ref, qb_ref, va_ref, vb_ref, k_ref, dil_ref):
    x = x_ref[...]
    ms = jnp.mean(x * x, axis=-1, keepdims=True)
    h = (x * lax.rsqrt(ms + NORM_EPS) * g_ref[...]).astype(BF16)
    tm = h.shape[0]

    qt = _dot_nt(wt_ref[0:PROJ_T_Q, :], h)
    q3 = qt.reshape(PROJ_T_Q // HEAD_DIM, HEAD_DIM, tm)
    q3 = q3 * lax.rsqrt(jnp.mean(q3 * q3, axis=1, keepdims=True) + NORM_EPS)
    qt = q3.reshape(PROJ_T_Q, tm) * gq_ref[...] * Q_SCALE
    qa_ref[...] = qt[0:NA_W].astype(BF16)
    qb_ref[...] = qt[NA_W:PROJ_T_Q].astype(BF16)
    vt = _dot_nt(wt_ref[PROJ_T_Q:PROJ_T_Q + PROJ_T_V, :], h)
    va_ref[...] = vt[0:NA_W].astype(BF16)
    vb_ref[...] = vt[NA_W:PROJ_T_V].astype(BF16)

    n_norm = (PROJ_R_K + 2 * DIL_W) // LANES
    n_key = PROJ_R_K // LANES
    n_dil_q = DIL_W // LANES
    for c in range(PROJ_R // PROJ_R_CHUNK):
        r = _dot(h, wr_ref[:, c * PROJ_R_CHUNK:(c + 1) * PROJ_R_CHUNK])
        for t in range(PROJ_R_CHUNK // LANES):
            blk = c * (PROJ_R_CHUNK // LANES) + t
            tile = r[:, t * LANES:(t + 1) * LANES]
            if blk < n_norm:
                tile = _pair_rms(tile, gr_ref[:, blk * LANES:(blk + 1) * LANES])
            if n_key <= blk < n_key + n_dil_q:
                tile = tile * Q_SCALE
            if blk < n_key:
                k_ref[:, blk * LANES:(blk + 1) * LANES] = tile.astype(BF16)
            else:
                dil_ref[:, (blk - n_key) * LANES:(blk - n_key + 1) * LANES] = tile.astype(BF16)


def _group_rms_t(yt, gain_col):
    yf = yt.astype(F32)
    ms = jnp.mean(yf * yf, axis=0, keepdims=True)
    return (yf * lax.rsqrt(ms + NORM_EPS) * gain_col).astype(BF16)


def _out_kernel(ya_ref, yb_ref, yc_ref, x_ref, gab_ref, gc_ref, w_ref, o_ref):
    ya = _group_rms_t(ya_ref[...], gab_ref[0:NA_W, :])
    yb = _group_rms_t(yb_ref[...], gab_ref[NA_W:NA_W + SWA_QW, :])
    yc = yc_ref[...].astype(F32)
    ms = jnp.mean(yc * yc, axis=-1, keepdims=True)
    yc = (yc * lax.rsqrt(ms + NORM_EPS) * gc_ref[...]).astype(BF16)
    acc = _dot_tn(ya, w_ref[0:NA_W, :])
    acc = acc + _dot_tn(yb, w_ref[NA_W:NA_W + SWA_QW, :])
    acc = acc + _dot(yc, w_ref[NA_W + SWA_QW:MIX_W, :])
    o_ref[...] = x_ref[...] + acc


def _ffn_kernel(x_ref, g_ref, wgu_ref, wd_ref, o_ref, h_scr):
    x = x_ref[...]
    ms = jnp.mean(x * x, axis=-1, keepdims=True)
    h_scr[...] = (x * lax.rsqrt(ms + NORM_EPS) * g_ref[...]).astype(BF16)
    o_ref[...] = x
    lo = 0
    for width in FFN_H_CHUNKS:
        gate = _dot(h_scr[...], wgu_ref[:, lo:lo + width])
        up = _dot(h_scr[...], wgu_ref[:, FFN_HIDDEN + lo:FFN_HIDDEN + lo + width])
        act = (gate / (1.0 + jnp.exp(-gate)) * up).astype(BF16)
        o_ref[...] += _dot(act, wd_ref[lo:lo + width, :])
        lo += width


def _pipeline3(chains, stage_a, stage_b, stage_c):
    n = len(chains)
    a = {i: stage_a(chains[i]) for i in range(min(2, n))}
    b = {0: stage_b(chains[0], a.pop(0))}
    for i in range(n):
        if i + 2 < n:
            a[i + 2] = stage_a(chains[i + 2])
        if i + 1 < n:
            b[i + 1] = stage_b(chains[i + 1], a.pop(i + 1))
        stage_c(chains[i], b.pop(i))


def _attend_t(chains):
    def scores(c):
        qt = c["q"]()
        zeros = jnp.zeros_like(qt)
        qpad = jnp.concatenate([qt, zeros] if c["half"] == 0 else [zeros, qt], axis=0)
        return _dot(c["k"](), qpad) + c["bias"]()

    def softmax(c, s):
        m = jnp.max(s, axis=0, keepdims=True)
        if c["sink"] is not None:
            m = jnp.maximum(m, c["sink"])
        return jnp.exp2(s - m).astype(BF16), m

    def values(c, pm):
        p, m = pm
        vt = c["v"]()
        own = vt[c["half"] * HEAD_DIM:(c["half"] + 1) * HEAD_DIM, :]
        ones = jnp.ones_like(own)
        o = _dot(jnp.concatenate([own, ones], axis=0), p)
        den = o[HEAD_DIM:2 * HEAD_DIM, :]
        if c["sink"] is not None:
            den = den + jnp.exp2(c["sink"] - m)
        c["store"](o[0:HEAD_DIM, :] / den)

    _pipeline3(chains, scores, softmax, values)


def _na_kernel(q_ref, k_ref, v_ref, tbl_ref, o_ref):
    def store(rows, cols):
        def put(o):
            o_ref[rows, cols] = o.astype(BF16)
        return put

    chains = []
    for j in range(SEQ // QBT):
        q0 = j * QBT
        start = min(max(q0 - QBT, 0), NA_LAST_START)
        variant = (q0 - start) // QBT
        cols = slice(q0, q0 + QBT)
        keys = slice(start, start + NA_WIN)
        for h in range(NA_HEADS):
            rows = slice(h * HEAD_DIM, (h + 1) * HEAD_DIM)
            pair = slice((h // 2) * LANES, (h // 2 + 1) * LANES)
            chains.append(dict(
                half=h % 2, sink=None,
                q=lambda rows=rows, cols=cols: q_ref[rows, cols],
                k=lambda keys=keys, pair=pair: k_ref[keys, pair],
                v=lambda keys=keys, pair=pair: v_ref[pair, keys],
                bias=lambda h=h, variant=variant: tbl_ref[h, variant],
                store=store(rows, cols)))
    _attend_t(chains)


def _na_bias_table(rpb):
    rows = SEQ // GRID_W
    n_blocks = SEQ // QBT
    rep_block = np.array([0, 1, n_blocks - 1])
    start = np.clip(rep_block * QBT - QBT, 0, NA_LAST_START)
    assert np.array_equal((rep_block * QBT - start) // QBT, np.arange(NA_VARIANTS))
    qq = np.arange(QBT)
    kk = np.arange(NA_WIN)
    qrow = (rep_block[:, None] * QBT + qq[None, :]) // GRID_W
    qcol = qq % GRID_W
    krow = (start[:, None] + kk[None, :]) // GRID_W
    kcol = kk % GRID_W
    row_start = np.clip(qrow - NA_ROWS // 2, 0, rows - NA_ROWS)
    col_start = np.clip(qcol - NA_COLS // 2, 0, GRID_W - NA_COLS)
    row_ok = (krow[:, :, None] >= row_start[:, None, :]) & (krow[:, :, None] < row_start[:, None, :] + NA_ROWS)
    col_ok = (kcol[:, None] >= col_start[None, :]) & (kcol[:, None] < col_start[None, :] + NA_COLS)
    valid = row_ok & col_ok[None]
    heads, n_row_rel, n_col_rel = rpb.shape
    period = 2 * GRID_W - 1
    lead = GRID_W - NA_COLS
    seq = jnp.pad(rpb.astype(F32), ((0, 0), (0, 0), (lead, period - lead - n_col_rel)))
    skew = jnp.tile(seq, (1, 1, GRID_W + 1))[..., :GRID_W * 2 * GRID_W]
    toe = skew.reshape(heads, n_row_rel, GRID_W, 2 * GRID_W)[:, :, ::-1, :GRID_W]
    key_rows, query_rows = NA_WIN // GRID_W, QBT // GRID_W
    first = (start // GRID_W)[:, None] - (rep_block * query_rows)[:, None] - np.arange(query_rows)[None, :] \
        + NA_ROWS - 1
    pad_lo = max(0, -int(first.min()))
    pad_hi = max(0, int(first.max()) + key_rows - n_row_rel)
    toe = jnp.pad(toe, ((0, 0), (pad_lo, pad_hi), (0, 0), (0, 0)))
    runs = [toe[:, pad_lo + a:pad_lo + a + key_rows] for a in first.reshape(-1)]
    tiles = jnp.stack(runs, axis=1).reshape(heads, NA_VARIANTS, query_rows, key_rows, GRID_W, GRID_W)
    bias = tiles.transpose(0, 1, 3, 5, 2, 4).reshape(heads, NA_VARIANTS, NA_WIN, QBT)
    return jnp.where(valid[None], bias * LOG2E, NEG_INF)


def _swa_kernel(slopes_ref, sink_ref, q_ref, k_ref, v_ref, o_ref, bias_scr):
    @pl.when(pl.program_id(0) == 0)
    def _():
        kk = lax.broadcasted_iota(jnp.int32, (SWA_WIN, QBT), 0)
        qq = lax.broadcasted_iota(jnp.int32, (SWA_WIN, QBT), 1)
        dist = jnp.abs(kk - SWA_HALF_WINDOW - qq)
        in_window = dist <= SWA_HALF_WINDOW
        dist_f = dist.astype(F32)
        for h in range(SWA_Q_HEADS):
            bias_scr[h] = jnp.where(in_window, -slopes_ref[h] * dist_f * LOG2E, NEG_INF)

    def store(rows, cols):
        def put(o):
            o_ref[rows, cols] = o.astype(BF16)
        return put

    chains = []
    for j in range(SEQ // QBT):
        q0 = j * QBT
        lo = max(q0 - SWA_HALF_WINDOW, 0)
        hi = min(q0 + QBT + SWA_HALF_WINDOW, SEQ)
        first = lo - (q0 - SWA_HALF_WINDOW)
        cols = slice(q0, q0 + QBT)
        keys = slice(lo, hi)
        win = slice(first, first + hi - lo)
        for h in range(SWA_Q_HEADS):
            rows = slice(h * HEAD_DIM, (h + 1) * HEAD_DIM)
            chains.append(dict(
                half=h // SWA_GROUP, sink=sink_ref[h] * LOG2E,
                q=lambda rows=rows, cols=cols: q_ref[rows, cols],
                k=lambda keys=keys: k_ref[keys, :],
                v=lambda keys=keys: v_ref[:, keys],
                bias=lambda h=h, win=win: bias_scr[h, win, :],
                store=store(rows, cols)))
    _attend_t(chains)


DIL_WIN = QB + 2 * DIL_HALF


def _dil_kernel(slopes_ref, q_ref, k_ref, v_ref, o_ref,
                nat_scr, lay_scr, bias_scr, bias16_scr, m_scr, l_scr, acc_scr):
    pair = pl.program_id(1)
    slope = [slopes_ref[SWA_Q_HEADS + 2 * pair + h] for h in range(2)]

    qq = lax.broadcasted_iota(jnp.int32, (QB, DIL_WIN), 0)
    kk = lax.broadcasted_iota(jnp.int32, (QB, DIL_WIN), 1)
    for shift in range(3):
        dist = jnp.abs(kk - shift * DIL_HALF - qq)
        ok = dist <= DIL_HALF
        dist_f = dist.astype(F32)
        for h in range(2):
            for b, stride in enumerate(DIL_STRIDES[:2]):
                bias_scr[h, b, shift] = jnp.where(ok, -slope[h] * (dist_f * float(stride)) * LOG2E, NEG_INF)
    qq = lax.broadcasted_iota(jnp.int32, (QB, QB), 0)
    kk = lax.broadcasted_iota(jnp.int32, (QB, QB), 1)
    dist = jnp.abs(kk - qq)
    for h in range(2):
        bias16_scr[h] = jnp.where(dist <= DIL_HALF,
                                  -slope[h] * (dist.astype(F32) * float(DIL_STRIDES[2])) * LOG2E, NEG_INF)

    def prep(i, carry):
        rows = pl.ds(pl.multiple_of(i * PREP_ROWS, PREP_ROWS), PREP_ROWS)
        lo = _low_half((PREP_ROWS, LANES))
        q = q_ref[rows, :]
        nat_scr[0, rows, :] = q.astype(F32)
        nat_scr[1, rows, :] = k_ref[rows, :].astype(F32)
        nat_scr[2, rows, :] = v_ref[rows, :].astype(F32)
        lay_scr[0, 0, rows, :] = jnp.where(lo, q, jnp.zeros_like(q))
        lay_scr[0, 1, rows, :] = jnp.where(lo, jnp.zeros_like(q), q)
        lay_scr[0, 2, rows, :] = k_ref[rows, :]
        lay_scr[0, 3, rows, :] = v_ref[rows, :]
        return carry

    lax.fori_loop(0, SEQ // PREP_ROWS, prep, 0)

    for b, stride in enumerate(DIL_STRIDES[1:], start=1):
        sub = SEQ // stride
        for phase in range(stride):
            src = pl.ds(phase, sub, stride=stride)
            dst = slice(phase * sub, (phase + 1) * sub)
            lo = _low_half((sub, LANES))
            qn = nat_scr[0, src, :]
            lay_scr[b, 0, dst, :] = jnp.where(lo, qn, 0.0).astype(BF16)
            lay_scr[b, 1, dst, :] = jnp.where(lo, 0.0, qn).astype(BF16)
            lay_scr[b, 2, dst, :] = nat_scr[1, src, :].astype(BF16)
            lay_scr[b, 3, dst, :] = nat_scr[2, src, :].astype(BF16)

    chains = []
    for j in range(NQB):
        for b, stride in enumerate(DIL_STRIDES):
            sub = SEQ // stride
            phase, local_q = divmod(j * QB, sub)
            if sub == QB:
                key_start, n_keys, shift = j * QB, QB, None
            else:
                local_k = min(max(local_q - DIL_HALF, 0), sub - DIL_WIN)
                key_start, n_keys, shift = phase * sub + local_k, DIL_WIN, (local_q - local_k) // DIL_HALF
            if stride == 1:
                dst = slice(j * QB, (j + 1) * QB)
            else:
                dst = pl.ds(local_q * stride + phase, QB, stride=stride)
            chains.append(dict(b=b, qrows=slice(j * QB, (j + 1) * QB),
                               krows=slice(key_start, key_start + n_keys), shift=shift, dst=dst))

    def scores(c):
        kw = lay_scr[c["b"], 2, c["krows"], :]
        out = []
        for h in range(2):
            bias = bias16_scr[h] if c["shift"] is None else bias_scr[h, c["b"], c["shift"]]
            out.append(_dot_nt(lay_scr[c["b"], h, c["qrows"], :], kw) + bias)
        return out

    def softmax(c, scores_pair):
        return [_softmax_block(s) for s in scores_pair]

    def values(c, parts):
        vw = lay_scr[c["b"], 3, c["krows"], :]
        lo_k = _low_half(vw.shape)
        ones = jnp.ones_like(vw)
        acc0 = _dot(parts[0][1], jnp.where(lo_k, vw, ones))
        acc1 = _dot(parts[1][1], jnp.where(lo_k, ones, vw))
        lo = _low_half((QB, LANES))
        m_scr[c["b"], c["dst"], :] = jnp.where(lo, parts[0][0], parts[1][0])
        l_scr[c["b"], c["dst"], :] = pltpu.roll(jnp.where(lo, acc1, acc0), HEAD_DIM, axis=1)
        acc_scr[c["b"], c["dst"], :] = jnp.where(lo, acc0, acc1)

    _pipeline3(chains, scores, softmax, values)

    def merge(i, carry):
        rows = pl.ds(pl.multiple_of(i * PREP_ROWS, PREP_ROWS), PREP_ROWS)
        m = [m_scr[b, rows, :] for b in range(3)]
        top = jnp.maximum(jnp.maximum(m[0], m[1]), m[2])
        num = jnp.zeros((PREP_ROWS, LANES), F32)
        den = jnp.zeros((PREP_ROWS, LANES), F32)
        for b in range(3):
            w = jnp.exp2(m[b] - top)
            num = num + w * acc_scr[b, rows, :]
            den = den + w * l_scr[b, rows, :]
        o_ref[rows, :] = (num / den).astype(BF16)
        return carry

    lax.fori_loop(0, SEQ // PREP_ROWS, merge, 0)


def _params(semantics):
    return pltpu.CompilerParams(dimension_semantics=semantics, vmem_limit_bytes=VMEM_LIMIT)


def _resident(shape):
    return pl.BlockSpec(shape, lambda *_: (0,) * len(shape), pipeline_mode=pl.Buffered(1))


def _smem():
    return pl.BlockSpec(memory_space=pltpu.SMEM)


def _proj(x, gain, w_t, w_r, gain_q, gain_r):
    t = x.shape[0]
    feat = lambda n: pl.BlockSpec((n, ROW_TILE), lambda i: (0, i))
    tok = lambda n: pl.BlockSpec((ROW_TILE, n), lambda i: (i, 0))
    return pl.pallas_call(
        _proj_kernel,
        out_shape=[jax.ShapeDtypeStruct((NA_W, t), BF16), jax.ShapeDtypeStruct((SWA_QW, t), BF16),
                   jax.ShapeDtypeStruct((NA_W, t), BF16), jax.ShapeDtypeStruct((SWA_KVW, t), BF16),
                   jax.ShapeDtypeStruct((t, PROJ_R_K), BF16), jax.ShapeDtypeStruct((t, 3 * DIL_W), BF16)],
        grid=(t // ROW_TILE,),
        in_specs=[tok(D_MODEL), _resident((1, D_MODEL)),
                  _resident((PROJ_T_Q + PROJ_T_V, D_MODEL)), _resident((D_MODEL, PROJ_R)),
                  _resident((PROJ_T_Q, 1)), _resident((1, PROJ_R_K + 2 * DIL_W))],
        out_specs=[feat(NA_W), feat(SWA_QW), feat(NA_W), feat(SWA_KVW), tok(PROJ_R_K), tok(3 * DIL_W)],
        compiler_params=_params(("parallel",)),
        name="proj",
    )(x, gain, w_t, w_r, gain_q, gain_r)


def _na(q_t, keys, v_t, table):
    t = q_t.shape[1]
    feat = pl.BlockSpec((NA_W, SEQ), lambda b: (0, b))
    return pl.pallas_call(
        _na_kernel,
        out_shape=jax.ShapeDtypeStruct((NA_W, t), BF16),
        grid=(t // SEQ,),
        in_specs=[feat, pl.BlockSpec((SEQ, NA_W), lambda b: (b, 0)), feat,
                  _resident((NA_HEADS, NA_VARIANTS, NA_WIN, QBT))],
        out_specs=feat,
        compiler_params=_params(("parallel",)),
        name="na",
    )(q_t, keys, v_t, table)


def _swa(q_t, keys, v_t, slopes, sink):
    t = q_t.shape[1]
    return pl.pallas_call(
        _swa_kernel,
        out_shape=jax.ShapeDtypeStruct((SWA_QW, t), BF16),
        grid=(t // SEQ,),
        in_specs=[_smem(), _smem(),
                  pl.BlockSpec((SWA_QW, SEQ), lambda b: (0, b)),
                  pl.BlockSpec((SEQ, SWA_KVW), lambda b: (b, NA_W // SWA_KVW)),
                  pl.BlockSpec((SWA_KVW, SEQ), lambda b: (0, b))],
        out_specs=pl.BlockSpec((SWA_QW, SEQ), lambda b: (0, b)),
        scratch_shapes=[pltpu.VMEM((SWA_Q_HEADS, SWA_WIN, QBT), F32)],
        compiler_params=_params(("arbitrary",)),
        name="swa",
    )(slopes, sink, q_t, keys, v_t)


def _dil(qkv, slopes):
    t = qkv.shape[0]
    pairs = DIL_W // LANES
    seq_block = lambda first: pl.BlockSpec((SEQ, LANES), lambda b, p: (b, first + p))
    return pl.pallas_call(
        _dil_kernel,
        out_shape=jax.ShapeDtypeStruct((t, DIL_W), BF16),
        grid=(t // SEQ, pairs),
        in_specs=[_smem(), seq_block(0), seq_block(pairs), seq_block(2 * pairs)],
        out_specs=seq_block(0),
        scratch_shapes=[pltpu.VMEM((3, SEQ, LANES), F32),
                        pltpu.VMEM((3, 4, SEQ, LANES), BF16),
                        pltpu.VMEM((2, 2, 3, QB, DIL_WIN), F32),
                        pltpu.VMEM((2, QB, QB), F32),
                        pltpu.VMEM((3, SEQ, LANES), F32),
                        pltpu.VMEM((3, SEQ, LANES), F32),
                        pltpu.VMEM((3, SEQ, LANES), F32)],
        compiler_params=_params(("parallel", "parallel")),
        name="dil",
    )(slopes, qkv, qkv, qkv)


def _out(ya_t, yb_t, yc, x, gain_ab, gain_c, w):
    t = x.shape[0]
    feat = lambda n: pl.BlockSpec((n, ROW_TILE), lambda i: (0, i))
    tok = lambda n: pl.BlockSpec((ROW_TILE, n), lambda i: (i, 0))
    return pl.pallas_call(
        _out_kernel,
        out_shape=jax.ShapeDtypeStruct((t, D_MODEL), F32),
        grid=(t // ROW_TILE,),
        in_specs=[feat(NA_W), feat(SWA_QW), tok(DIL_W), tok(D_MODEL),
                  _resident((NA_W + SWA_QW, 1)), _resident((1, DIL_W)), _resident((MIX_W, D_MODEL))],
        out_specs=tok(D_MODEL),
        compiler_params=_params(("parallel",)),
        name="out",
    )(ya_t, yb_t, yc, x, gain_ab, gain_c, w)


def _ffn(x, gain, w_gu, w_down):
    t = x.shape[0]
    row = pl.BlockSpec((ROW_TILE, D_MODEL), lambda i: (i, 0))
    return pl.pallas_call(
        _ffn_kernel,
        out_shape=jax.ShapeDtypeStruct((t, D_MODEL), F32),
        grid=(t // ROW_TILE,),
        in_specs=[row, _resident((1, D_MODEL)),
                  _resident((D_MODEL, 2 * FFN_HIDDEN)), _resident((FFN_HIDDEN, D_MODEL))],
        out_specs=row,
        scratch_shapes=[pltpu.VMEM((ROW_TILE, D_MODEL), BF16)],
        compiler_params=_params(("parallel",)),
        name="ffn",
    )(x, gain, w_gu, w_down)


def _split_w_in(w):
    cuts = np.cumsum([NA_W, NA_W, NA_W, SWA_QW, SWA_KVW, SWA_KVW, DIL_W, DIL_W])
    qa, ka, va, qb, kb, vb, qc, kc, vc = jnp.split(w, [int(c) for c in cuts], axis=1)
    w_t = jnp.concatenate([qa, qb, va, vb], axis=1).T
    w_r = jnp.concatenate([ka, kb, qc, kc, vc], axis=1)
    return w_t.astype(BF16), w_r.astype(BF16)


def kernel(x, attn_norm, w_in, qk_gain, rpb, sink, out_gain, w_out, ffn_norm, w_gu, w_down):
    batch, seq, d = x.shape
    assert (seq, d) == (SEQ, D_MODEL)
    depth = w_in.shape[0]
    n_slopes = SWA_Q_HEADS + DIL_HEADS
    slopes = 2.0 ** (-8.0 * (jnp.arange(n_slopes, dtype=F32) + 1.0) / n_slopes)
    xf = x.reshape(batch * seq, d)
    for l in range(depth):
        g = qk_gain[l].astype(F32)
        gain_q = jnp.concatenate([jnp.tile(g[0, 0], NA_HEADS), jnp.tile(g[1, 0], SWA_Q_HEADS)])[:, None]
        gain_r = jnp.concatenate([jnp.tile(g[0, 1], NA_HEADS), jnp.tile(g[1, 1], SWA_KV_HEADS),
                                  jnp.tile(g[2, 0], DIL_HEADS), jnp.tile(g[2, 1], DIL_HEADS)])[None, :]
        w_t, w_r = _split_w_in(w_in[l])
        qa_t, qb_t, va_t, vb_t, keys, dil_qkv = _proj(xf, attn_norm[l][None, :], w_t, w_r, gain_q, gain_r)
        ya_t = _na(qa_t, keys, va_t, _na_bias_table(rpb[l]))
        yb_t = _swa(qb_t, keys, vb_t, slopes, sink[l].astype(F32))
        yc = _dil(dil_qkv, slopes)
        og = out_gain[l].astype(F32)
        xf = _out(ya_t, yb_t, yc, xf, og[:NA_W + SWA_QW, None], og[None, NA_W + SWA_QW:], w_out[l].astype(BF16))
        xf = _ffn(xf, ffn_norm[l][None, :], w_gu[l].astype(BF16), w_down[l].astype(BF16))
    return xf.reshape(batch, seq, d)
```

```python
import math

import numpy as np
import jax
import jax.numpy as jnp
from jax import lax
from jax.experimental import pallas as pl
from jax.experimental.pallas import tpu as pltpu

D_MODEL = 1024
SEQ = 2048
HEAD_DIM = 64
GRID_W = 64
NA_HEADS = 4
NA_ROWS = 8
NA_COLS = 16
SWA_Q_HEADS = 6
SWA_KV_HEADS = 2
SWA_GROUP = SWA_Q_HEADS // SWA_KV_HEADS
SWA_HALF_WINDOW = 128
DIL_HEADS = 6
DIL_STRIDES = (1, 4, 16)
DIL_HALF = 64
NA_W = NA_HEADS * HEAD_DIM
SWA_QW = SWA_Q_HEADS * HEAD_DIM
SWA_KVW = SWA_KV_HEADS * HEAD_DIM
DIL_W = DIL_HEADS * HEAD_DIM
IN_W = 3 * NA_W + SWA_QW + 2 * SWA_KVW + 3 * DIL_W
MIX_W = NA_W + SWA_QW + DIL_W
FFN_HIDDEN = 2816
NORM_EPS = 1e-6
NEG_INF = -1e30
QK_SCALE = HEAD_DIM ** -0.5
LOG2E = math.log2(math.e)
Q_SCALE = QK_SCALE * LOG2E

LANES = 128
VMEM_LIMIT = 56 * 1024 * 1024

QB = 128
QBT = 256
PREP_ROWS = 256
NQB = SEQ // QB
ROW_TILE = 512
MXU_DIM = 256
FFN_H_CHUNKS = (6 * MXU_DIM, 5 * MXU_DIM)
assert sum(FFN_H_CHUNKS) == FFN_HIDDEN
DIL_LAG = 1

PROJ_T_Q = NA_W + SWA_QW
PROJ_T_V = NA_W + SWA_KVW
PROJ_R_K = NA_W + SWA_KVW
PROJ_R = PROJ_R_K + 3 * DIL_W
PROJ_R_CHUNK = 768

NA_WIN = 768
NA_VARIANTS = 3
NA_LAST_START = SEQ - NA_WIN
SWA_WIN = QBT + 2 * SWA_HALF_WINDOW

F32 = jnp.float32
BF16 = jnp.bfloat16


def _dot(a, b):
    return jnp.dot(a, b, preferred_element_type=F32)


def _dot_nt(a, b):
    return lax.dot_general(a, b, (((1,), (1,)), ((), ())), preferred_element_type=F32)


def _dot_tn(a, b):
    return lax.dot_general(a, b, (((0,), (0,)), ((), ())), preferred_element_type=F32)


def _low_half(shape):
    return lax.broadcasted_iota(jnp.int32, shape, len(shape) - 1) < HEAD_DIM


def _pair_rms(x, gain):
    lo = _low_half(x.shape)
    sq = x * x
    s_lo = jnp.sum(jnp.where(lo, sq, 0.0), axis=-1, keepdims=True)
    s_hi = jnp.sum(jnp.where(lo, 0.0, sq), axis=-1, keepdims=True)
    ms = jnp.where(lo, s_lo, s_hi) * (1.0 / HEAD_DIM)
    return x * lax.rsqrt(ms + NORM_EPS) * gain


def _softmax_block(s):
    m = jnp.max(s, axis=-1, keepdims=True)
    return m, jnp.exp2(s - m).astype(BF16)


def _proj_kernel(x_ref, g_ref, wt_ref, wr_ref, gq_ref, gr_ref,
                 qa_ref, qb_ref, va_ref, vb_ref, k_ref, dil_ref, dil4_ref, dil16_ref, regroup_scr):
    x = x_ref[...]
    ms = jnp.mean(x * x, axis=-1, keepdims=True)
    h = (x * lax.rsqrt(ms + NORM_EPS) * g_ref[...]).astype(BF16)
    tm = h.shape[0]

    qt = _dot_nt(wt_ref[0:PROJ_T_Q, :], h)
    q3 = qt.reshape(PROJ_T_Q // HEAD_DIM, HEAD_DIM, tm)
    q3 = q3 * lax.rsqrt(jnp.mean(q3 * q3, axis=1, keepdims=True) + NORM_EPS)
    qt = q3.reshape(PROJ_T_Q, tm) * gq_ref[...] * Q_SCALE
    qa_ref[...] = qt[0:NA_W].astype(BF16)
    qb_ref[...] = qt[NA_W:PROJ_T_Q].astype(BF16)
    vt = _dot_nt(wt_ref[PROJ_T_Q:PROJ_T_Q + PROJ_T_V, :], h)
    va_ref[...] = vt[0:NA_W].astype(BF16)
    vb_ref[...] = vt[NA_W:PROJ_T_V].astype(BF16)

    n_norm = (PROJ_R_K + 2 * DIL_W) // LANES
    n_key = PROJ_R_K // LANES
    n_dil_q = DIL_W // LANES
    for c in range(PROJ_R // PROJ_R_CHUNK):
        r = _dot(h, wr_ref[:, c * PROJ_R_CHUNK:(c + 1) * PROJ_R_CHUNK])
        for t in range(PROJ_R_CHUNK // LANES):
            blk = c * (PROJ_R_CHUNK // LANES) + t
            tile = r[:, t * LANES:(t + 1) * LANES]
            if blk < n_norm:
                tile = _pair_rms(tile, gr_ref[:, blk * LANES:(blk + 1) * LANES])
            if n_key <= blk < n_key + n_dil_q:
                tile = tile * Q_SCALE
            if blk < n_key:
                k_ref[:, blk * LANES:(blk + 1) * LANES] = tile.astype(BF16)
                continue
            d = blk - n_key
            cols = slice(d * LANES, (d + 1) * LANES)
            dil_ref[:, cols] = tile.astype(BF16)
            fine, coarse = DIL_STRIDES[1], DIL_STRIDES[2] // DIL_STRIDES[1]
            per_fine = tm // fine
            regroup_scr[d, 0] = tile
            for ph in range(fine):
                part = regroup_scr[d, 0, pl.ds(ph, per_fine, stride=fine), :]
                dil4_ref[0, ph, :, cols] = part.astype(BF16)
                regroup_scr[d, 1, ph * per_fine:(ph + 1) * per_fine, :] = part
            for ph in range(fine):
                for sub_ph in range(coarse):
                    part = regroup_scr[d, 1, pl.ds(ph * per_fine + sub_ph, per_fine // coarse, stride=coarse), :]
                    dil16_ref[0, ph + fine * sub_ph, :, cols] = part.astype(BF16)


def _group_rms_t(yt, gain_col):
    yf = yt.astype(F32)
    ms = jnp.mean(yf * yf, axis=0, keepdims=True)
    return (yf * lax.rsqrt(ms + NORM_EPS) * gain_col).astype(BF16)


def _out_kernel(ya_ref, yb_ref, yc_ref, x_ref, gab_ref, gc_ref, w_ref, o_ref):
    ya = _group_rms_t(ya_ref[...], gab_ref[0:NA_W, :])
    yb = _group_rms_t(yb_ref[...], gab_ref[NA_W:NA_W + SWA_QW, :])
    yc = yc_ref[...].astype(F32)
    ms = jnp.mean(yc * yc, axis=-1, keepdims=True)
    yc = (yc * lax.rsqrt(ms + NORM_EPS) * gc_ref[...]).astype(BF16)
    acc = _dot_tn(ya, w_ref[0:NA_W, :])
    acc = acc + _dot_tn(yb, w_ref[NA_W:NA_W + SWA_QW, :])
    acc = acc + _dot(yc, w_ref[NA_W + SWA_QW:MIX_W, :])
    o_ref[...] = x_ref[...] + acc


def _ffn_kernel(x_ref, g_ref, wgu_ref, wd_ref, o_ref, h_scr):
    x = x_ref[...]
    ms = jnp.mean(x * x, axis=-1, keepdims=True)
    h_scr[...] = (x * lax.rsqrt(ms + NORM_EPS) * g_ref[...]).astype(BF16)
    o_ref[...] = x
    lo = 0
    for width in FFN_H_CHUNKS:
        gate = _dot(h_scr[...], wgu_ref[:, lo:lo + width])
        up = _dot(h_scr[...], wgu_ref[:, FFN_HIDDEN + lo:FFN_HIDDEN + lo + width])
        act = (gate / (1.0 + jnp.exp(-gate)) * up).astype(BF16)
        o_ref[...] += _dot(act, wd_ref[lo:lo + width, :])
        lo += width


def _pipeline3(chains, stage_a, stage_b, stage_c, lag=1):
    n = len(chains)
    a, b = {}, {}
    for i in range(-2 * lag, n):
        if 0 <= i + 2 * lag < n:
            a[i + 2 * lag] = stage_a(chains[i + 2 * lag])
        if 0 <= i + lag < n:
            b[i + lag] = stage_b(chains[i + lag], a.pop(i + lag))
        if i >= 0:
            stage_c(chains[i], b.pop(i))


def _attend_t(chains):
    def scores(c):
        qt = c["q"]()
        zeros = jnp.zeros_like(qt)
        qpad = jnp.concatenate([qt, zeros] if c["half"] == 0 else [zeros, qt], axis=0)
        return _dot(c["k"](), qpad) + c["bias"]()

    def softmax(c, s):
        m = jnp.max(s, axis=0, keepdims=True)
        if c["sink"] is not None:
            m = jnp.maximum(m, c["sink"])
        return jnp.exp2(s - m).astype(BF16), m

    def values(c, pm):
        p, m = pm
        vt = c["v"]()
        own = vt[c["half"] * HEAD_DIM:(c["half"] + 1) * HEAD_DIM, :]
        ones = jnp.ones_like(own)
        o = _dot(jnp.concatenate([own, ones], axis=0), p)
        den = o[HEAD_DIM:2 * HEAD_DIM, :]
        if c["sink"] is not None:
            den = den + jnp.exp2(c["sink"] - m)
        c["store"](o[0:HEAD_DIM, :] / den)

    _pipeline3(chains, scores, softmax, values)


def _na_kernel(q_ref, k_ref, v_ref, tbl_ref, o_ref):
    def store(rows, cols):
        def put(o):
            o_ref[rows, cols] = o.astype(BF16)
        return put

    chains = []
    for j in range(SEQ // QBT):
        q0 = j * QBT
        start = min(max(q0 - QBT, 0), NA_LAST_START)
        variant = (q0 - start) // QBT
        cols = slice(q0, q0 + QBT)
        keys = slice(start, start + NA_WIN)
        for h in range(NA_HEADS):
            rows = slice(h * HEAD_DIM, (h + 1) * HEAD_DIM)
            pair = slice((h // 2) * LANES, (h // 2 + 1) * LANES)
            chains.append(dict(
                half=h % 2, sink=None,
                q=lambda rows=rows, cols=cols: q_ref[rows, cols],
                k=lambda keys=keys, pair=pair: k_ref[keys, pair],
                v=lambda keys=keys, pair=pair: v_ref[pair, keys],
                bias=lambda h=h, variant=variant: tbl_ref[h, variant],
                store=store(rows, cols)))
    _attend_t(chains)


def _na_bias_table(rpb):
    rows = SEQ // GRID_W
    n_blocks = SEQ // QBT
    rep_block = np.array([0, 1, n_blocks - 1])
    start = np.clip(rep_block * QBT - QBT, 0, NA_LAST_START)
    assert np.array_equal((rep_block * QBT - start) // QBT, np.arange(NA_VARIANTS))
    qq = np.arange(QBT)
    kk = np.arange(NA_WIN)
    qrow = (rep_block[:, None] * QBT + qq[None, :]) // GRID_W
    qcol = qq % GRID_W
    krow = (start[:, None] + kk[None, :]) // GRID_W
    kcol = kk % GRID_W
    row_start = np.clip(qrow - NA_ROWS // 2, 0, rows - NA_ROWS)
    col_start = np.clip(qcol - NA_COLS // 2, 0, GRID_W - NA_COLS)
    row_ok = (krow[:, :, None] >= row_start[:, None, :]) & (krow[:, :, None] < row_start[:, None, :] + NA_ROWS)
    col_ok = (kcol[:, None] >= col_start[None, :]) & (kcol[:, None] < col_start[None, :] + NA_COLS)
    valid = row_ok & col_ok[None]
    heads, n_row_rel, n_col_rel = rpb.shape
    period = 2 * GRID_W - 1
    lead = GRID_W - NA_COLS
    seq = jnp.pad(rpb.astype(F32)[:, :, ::-1], ((0, 0), (0, 0), (lead, period - lead - n_col_rel)))
    skew = jnp.tile(seq, (1, 1, GRID_W + 1))[..., :GRID_W * 2 * GRID_W]
    toe = skew.reshape(heads, n_row_rel, GRID_W, 2 * GRID_W)[:, :, ::-1, :GRID_W]
    key_rows, query_rows = NA_WIN // GRID_W, QBT // GRID_W
    first = (start // GRID_W)[:, None] - (rep_block * query_rows)[:, None] - np.arange(query_rows)[None, :] \
        + NA_ROWS - 1
    pad_lo = max(0, -int(first.min()))
    pad_hi = max(0, int(first.max()) + key_rows - n_row_rel)
    toe = jnp.pad(toe, ((0, 0), (pad_lo, pad_hi), (0, 0), (0, 0)))
    bias = jnp.stack([
        jnp.concatenate([toe[:, pad_lo + a:pad_lo + a + key_rows] for a in first[v]], axis=-1)
        for v in range(NA_VARIANTS)], axis=1)
    bias = bias.reshape(heads, NA_VARIANTS, NA_WIN, QBT)
    return jnp.where(valid[None], bias * LOG2E, NEG_INF)


def _swa_kernel(slopes_ref, sink_ref, q_ref, k_ref, v_ref, o_ref, bias_scr):
    @pl.when(pl.program_id(0) == 0)
    def _():
        kk = lax.broadcasted_iota(jnp.int32, (SWA_WIN, QBT), 0)
        qq = lax.broadcasted_iota(jnp.int32, (SWA_WIN, QBT), 1)
        dist = jnp.abs(kk - SWA_HALF_WINDOW - qq)
        in_window = dist <= SWA_HALF_WINDOW
        dist_f = dist.astype(F32)
        for h in range(SWA_Q_HEADS):
            bias_scr[h] = jnp.where(in_window, -slopes_ref[h] * dist_f * LOG2E, NEG_INF)

    def store(rows, cols):
        def put(o):
            o_ref[rows, cols] = o.astype(BF16)
        return put

    chains = []
    for j in range(SEQ // QBT):
        q0 = j * QBT
        lo = max(q0 - SWA_HALF_WINDOW, 0)
        hi = min(q0 + QBT + SWA_HALF_WINDOW, SEQ)
        first = lo - (q0 - SWA_HALF_WINDOW)
        cols = slice(q0, q0 + QBT)
        keys = slice(lo, hi)
        win = slice(first, first + hi - lo)
        for h in range(SWA_Q_HEADS):
            rows = slice(h * HEAD_DIM, (h + 1) * HEAD_DIM)
            chains.append(dict(
                half=h // SWA_GROUP, sink=sink_ref[h] * LOG2E,
                q=lambda rows=rows, cols=cols: q_ref[rows, cols],
                k=lambda keys=keys: k_ref[keys, :],
                v=lambda keys=keys: v_ref[:, keys],
                bias=lambda h=h, win=win: bias_scr[h, win, :],
                store=store(rows, cols)))
    _attend_t(chains)


DIL_WIN = QB + 2 * DIL_HALF


def _dil_kernel(slopes_ref, q1_ref, k1_ref, v1_ref, q4_ref, k4_ref, v4_ref, q16_ref, k16_ref, v16_ref,
                o_ref, bias_scr, bias16_scr, m_scr, l_scr, acc_scr):
    pair = pl.program_id(1)

    @pl.when(pl.program_id(0) == 0)
    def _():
        slope = [slopes_ref[SWA_Q_HEADS + 2 * pair + h] for h in range(2)]
        qq = lax.broadcasted_iota(jnp.int32, (QB, DIL_WIN), 0)
        kk = lax.broadcasted_iota(jnp.int32, (QB, DIL_WIN), 1)
        for shift in range(3):
            dist = jnp.abs(kk - shift * DIL_HALF - qq)
            ok = dist <= DIL_HALF
            dist_f = dist.astype(F32)
            for h in range(2):
                for b, stride in enumerate(DIL_STRIDES[:2]):
                    bias_scr[pair, b, shift, h * QB:(h + 1) * QB, :] = jnp.where(
                        ok, -slope[h] * (dist_f * float(stride)) * LOG2E, NEG_INF)
        qq = lax.broadcasted_iota(jnp.int32, (QB, QB), 0)
        kk = lax.broadcasted_iota(jnp.int32, (QB, QB), 1)
        dist = jnp.abs(kk - qq)
        for h in range(2):
            bias16_scr[pair, h * QB:(h + 1) * QB, :] = jnp.where(
                dist <= DIL_HALF, -slope[h] * (dist.astype(F32) * float(DIL_STRIDES[2])) * LOG2E, NEG_INF)

    chains = []
    for j in range(NQB):
        for b, stride in enumerate(DIL_STRIDES):
            sub = SEQ // stride
            phase, local_q = divmod(j * QB, sub)
            if sub == QB:
                local_k, n_keys, shift = 0, QB, None
            else:
                local_k = min(max(local_q - DIL_HALF, 0), sub - DIL_WIN)
                n_keys, shift = DIL_WIN, (local_q - local_k) // DIL_HALF
            qrows, krows = slice(local_q, local_q + QB), slice(local_k, local_k + n_keys)
            if stride == 1:
                refs = (q1_ref, k1_ref, v1_ref)
                load = lambda ref, rows: ref[rows, :]
                dst = qrows
            else:
                refs = (q4_ref, k4_ref, v4_ref) if stride == DIL_STRIDES[1] else (q16_ref, k16_ref, v16_ref)
                load = lambda ref, rows, phase=phase: ref[0, phase, rows, :]
                dst = pl.ds(local_q * stride + phase, QB, stride=stride)
            chains.append(dict(b=b, shift=shift, dst=dst,
                               q=lambda refs=refs, load=load, rows=qrows: load(refs[0], rows),
                               k=lambda refs=refs, load=load, rows=krows: load(refs[1], rows),
                               v=lambda refs=refs, load=load, rows=krows: load(refs[2], rows)))

    def scores(c):
        q, kw = c["q"](), c["k"]()
        lo = _low_half(q.shape)
        zeros = jnp.zeros_like(q)
        q_both = jnp.concatenate([jnp.where(lo, q, zeros), jnp.where(lo, zeros, q)], axis=0)
        bias = bias16_scr[pair] if c["shift"] is None else bias_scr[pair, c["b"], c["shift"]]
        return _dot_nt(q_both, kw) + bias

    def softmax(c, s):
        return _softmax_block(s)

    def values(c, mp):
        m, p = mp
        vw = c["v"]()
        res = _dot(p, jnp.concatenate([vw, jnp.ones_like(vw)], axis=1))
        lo = _low_half((QB, LANES))
        m_scr[c["b"], c["dst"], :] = jnp.where(lo, m[0:QB], m[QB:2 * QB])
        l_scr[c["b"], c["dst"], :] = jnp.where(lo, res[0:QB, LANES:2 * LANES], res[QB:2 * QB, LANES:2 * LANES])
        acc_scr[c["b"], c["dst"], :] = jnp.where(lo, res[0:QB, 0:LANES], res[QB:2 * QB, 0:LANES])

    _pipeline3(chains, scores, softmax, values, lag=DIL_LAG)

    def merge(i, carry):
        rows = pl.ds(pl.multiple_of(i * PREP_ROWS, PREP_ROWS), PREP_ROWS)
        m = [m_scr[b, rows, :] for b in range(3)]
        top = jnp.maximum(jnp.maximum(m[0], m[1]), m[2])
        num = jnp.zeros((PREP_ROWS, LANES), F32)
        den = jnp.zeros((PREP_ROWS, LANES), F32)
        for b in range(3):
            w = jnp.exp2(m[b] - top)
            num = num + w * acc_scr[b, rows, :]
            den = den + w * l_scr[b, rows, :]
        o_ref[rows, :] = (num / den).astype(BF16)
        return carry

    lax.fori_loop(0, SEQ // PREP_ROWS, merge, 0)


def _params(semantics):
    return pltpu.CompilerParams(dimension_semantics=semantics, vmem_limit_bytes=VMEM_LIMIT)


def _resident(shape):
    return pl.BlockSpec(shape, lambda *_: (0,) * len(shape), pipeline_mode=pl.Buffered(1))


def _smem():
    return pl.BlockSpec(memory_space=pltpu.SMEM)


def _proj(x, gain, w_t, w_r, gain_q, gain_r):
    t = x.shape[0]
    feat = lambda n: pl.BlockSpec((n, ROW_TILE), lambda i: (0, i))
    tok = lambda n: pl.BlockSpec((ROW_TILE, n), lambda i: (i, 0))
    tiles_per_seq = SEQ // ROW_TILE

    def phased(stride):
        return pl.BlockSpec((1, stride, ROW_TILE // stride, 3 * DIL_W),
                            lambda i: (i // tiles_per_seq, 0, i % tiles_per_seq, 0))

    def phased_shape(stride):
        return jax.ShapeDtypeStruct((t // SEQ, stride, SEQ // stride, 3 * DIL_W), BF16)

    return pl.pallas_call(
        _proj_kernel,
        out_shape=[jax.ShapeDtypeStruct((NA_W, t), BF16), jax.ShapeDtypeStruct((SWA_QW, t), BF16),
                   jax.ShapeDtypeStruct((NA_W, t), BF16), jax.ShapeDtypeStruct((SWA_KVW, t), BF16),
                   jax.ShapeDtypeStruct((t, PROJ_R_K), BF16), jax.ShapeDtypeStruct((t, 3 * DIL_W), BF16),
                   phased_shape(DIL_STRIDES[1]), phased_shape(DIL_STRIDES[2])],
        grid=(t // ROW_TILE,),
        in_specs=[tok(D_MODEL), _resident((1, D_MODEL)),
                  _resident((PROJ_T_Q + PROJ_T_V, D_MODEL)), _resident((D_MODEL, PROJ_R)),
                  _resident((PROJ_T_Q, 1)), _resident((1, PROJ_R_K + 2 * DIL_W))],
        out_specs=[feat(NA_W), feat(SWA_QW), feat(NA_W), feat(SWA_KVW), tok(PROJ_R_K), tok(3 * DIL_W),
                   phased(DIL_STRIDES[1]), phased(DIL_STRIDES[2])],
        scratch_shapes=[pltpu.VMEM((3 * DIL_W // LANES, 2, ROW_TILE, LANES), F32)],
        compiler_params=_params(("parallel",)),
        name="proj",
    )(x, gain, w_t, w_r, gain_q, gain_r)


def _na(q_t, keys, v_t, table):
    t = q_t.shape[1]
    feat = pl.BlockSpec((NA_W, SEQ), lambda b: (0, b))
    return pl.pallas_call(
        _na_kernel,
        out_shape=jax.ShapeDtypeStruct((NA_W, t), BF16),
        grid=(t // SEQ,),
        in_specs=[feat, pl.BlockSpec((SEQ, NA_W), lambda b: (b, 0)), feat,
                  _resident((NA_HEADS, NA_VARIANTS, NA_WIN, QBT))],
        out_specs=feat,
        compiler_params=_params(("parallel",)),
        name="na",
    )(q_t, keys, v_t, table)


def _swa(q_t, keys, v_t, slopes, sink):
    t = q_t.shape[1]
    return pl.pallas_call(
        _swa_kernel,
        out_shape=jax.ShapeDtypeStruct((SWA_QW, t), BF16),
        grid=(t // SEQ,),
        in_specs=[_smem(), _smem(),
                  pl.BlockSpec((SWA_QW, SEQ), lambda b: (0, b)),
                  pl.BlockSpec((SEQ, SWA_KVW), lambda b: (b, NA_W // SWA_KVW)),
                  pl.BlockSpec((SWA_KVW, SEQ), lambda b: (0, b))],
        out_specs=pl.BlockSpec((SWA_QW, SEQ), lambda b: (0, b)),
        scratch_shapes=[pltpu.VMEM((SWA_Q_HEADS, SWA_WIN, QBT), F32)],
        compiler_params=_params(("arbitrary",)),
        name="swa",
    )(slopes, sink, q_t, keys, v_t)


def _dil(qkv, qkv4, qkv16, slopes):
    t = qkv.shape[0]
    pairs = DIL_W // LANES
    seq_block = lambda first: pl.BlockSpec((SEQ, LANES), lambda b, p: (b, first + p))

    def phased(stride, first):
        return pl.BlockSpec((1, stride, SEQ // stride, LANES), lambda b, p: (b, 0, 0, first + p))

    operands, specs = [], []
    for arr, spec in ((qkv, seq_block), (qkv4, lambda f: phased(DIL_STRIDES[1], f)),
                      (qkv16, lambda f: phased(DIL_STRIDES[2], f))):
        for part in range(3):
            operands.append(arr)
            specs.append(spec(part * pairs))
    return pl.pallas_call(
        _dil_kernel,
        out_shape=jax.ShapeDtypeStruct((t, DIL_W), BF16),
        grid=(t // SEQ, pairs),
        in_specs=[_smem()] + specs,
        out_specs=seq_block(0),
        scratch_shapes=[pltpu.VMEM((pairs, 2, 3, 2 * QB, DIL_WIN), F32),
                        pltpu.VMEM((pairs, 2 * QB, QB), F32),
                        pltpu.VMEM((3, SEQ, LANES), F32),
                        pltpu.VMEM((3, SEQ, LANES), F32),
                        pltpu.VMEM((3, SEQ, LANES), F32)],
        compiler_params=_params(("arbitrary", "arbitrary")),
        name="dil",
    )(slopes, *operands)


def _out(ya_t, yb_t, yc, x, gain_ab, gain_c, w):
    t = x.shape[0]
    feat = lambda n: pl.BlockSpec((n, ROW_TILE), lambda i: (0, i))
    tok = lambda n: pl.BlockSpec((ROW_TILE, n), lambda i: (i, 0))
    return pl.pallas_call(
        _out_kernel,
        out_shape=jax.ShapeDtypeStruct((t, D_MODEL), F32),
        grid=(t // ROW_TILE,),
        in_specs=[feat(NA_W), feat(SWA_QW), tok(DIL_W), tok(D_MODEL),
                  _resident((NA_W + SWA_QW, 1)), _resident((1, DIL_W)), _resident((MIX_W, D_MODEL))],
        out_specs=tok(D_MODEL),
        compiler_params=_params(("parallel",)),
        name="out",
    )(ya_t, yb_t, yc, x, gain_ab, gain_c, w)


def _ffn(x, gain, w_gu, w_down):
    t = x.shape[0]
    row = pl.BlockSpec((ROW_TILE, D_MODEL), lambda i: (i, 0))
    return pl.pallas_call(
        _ffn_kernel,
        out_shape=jax.ShapeDtypeStruct((t, D_MODEL), F32),
        grid=(t // ROW_TILE,),
        in_specs=[row, _resident((1, D_MODEL)),
                  _resident((D_MODEL, 2 * FFN_HIDDEN)), _resident((FFN_HIDDEN, D_MODEL))],
        out_specs=row,
        scratch_shapes=[pltpu.VMEM((ROW_TILE, D_MODEL), BF16)],
        compiler_params=_params(("parallel",)),
        name="ffn",
    )(x, gain, w_gu, w_down)


def _split_w_in(w):
    cuts = np.cumsum([NA_W, NA_W, NA_W, SWA_QW, SWA_KVW, SWA_KVW, DIL_W, DIL_W])
    qa, ka, va, qb, kb, vb, qc, kc, vc = jnp.split(w, [int(c) for c in cuts], axis=1)
    w_t = jnp.concatenate([qa, qb, va, vb], axis=1).T
    w_r = jnp.concatenate([ka, kb, qc, kc, vc], axis=1)
    return w_t.astype(BF16), w_r.astype(BF16)


def kernel(x, attn_norm, w_in, qk_gain, rpb, sink, out_gain, w_out, ffn_norm, w_gu, w_down):
    batch, seq, d = x.shape
    assert (seq, d) == (SEQ, D_MODEL)
    depth = w_in.shape[0]
    n_slopes = SWA_Q_HEADS + DIL_HEADS
    slopes = 2.0 ** (-8.0 * (jnp.arange(n_slopes, dtype=F32) + 1.0) / n_slopes)
    xf = x.reshape(batch * seq, d)
    for l in range(depth):
        g = qk_gain[l].astype(F32)
        gain_q = jnp.concatenate([jnp.tile(g[0, 0], NA_HEADS), jnp.tile(g[1, 0], SWA_Q_HEADS)])[:, None]
        gain_r = jnp.concatenate([jnp.tile(g[0, 1], NA_HEADS), jnp.tile(g[1, 1], SWA_KV_HEADS),
                                  jnp.tile(g[2, 0], DIL_HEADS), jnp.tile(g[2, 1], DIL_HEADS)])[None, :]
        w_t, w_r = _split_w_in(w_in[l])
        qa_t, qb_t, va_t, vb_t, keys, dil1, dil4, dil16 = _proj(
            xf, attn_norm[l][None, :], w_t, w_r, gain_q, gain_r)
        ya_t = _na(qa_t, keys, va_t, _na_bias_table(rpb[l]))
        yb_t = _swa(qb_t, keys, vb_t, slopes, sink[l].astype(F32))
        yc = _dil(dil1, dil4, dil16, slopes)
        og = out_gain[l].astype(F32)
        xf = _out(ya_t, yb_t, yc, xf, og[:NA_W + SWA_QW, None], og[None, NA_W + SWA_QW:], w_out[l].astype(BF16))
        xf = _ffn(xf, ffn_norm[l][None, :], w_gu[l].astype(BF16), w_down[l].astype(BF16))
    return xf.reshape(batch, seq, d)
```

```python
import math

import numpy as np
import jax
import jax.numpy as jnp
from jax import lax
from jax.experimental import pallas as pl
from jax.experimental.pallas import tpu as pltpu

D_MODEL = 1024
SEQ = 2048
HEAD_DIM = 64
GRID_W = 64
NA_HEADS = 4
NA_ROWS = 8
NA_COLS = 16
SWA_Q_HEADS = 6
SWA_KV_HEADS = 2
SWA_GROUP = SWA_Q_HEADS // SWA_KV_HEADS
SWA_HALF_WINDOW = 128
DIL_HEADS = 6
DIL_STRIDES = (1, 4, 16)
DIL_HALF = 64
NA_W = NA_HEADS * HEAD_DIM
SWA_QW = SWA_Q_HEADS * HEAD_DIM
SWA_KVW = SWA_KV_HEADS * HEAD_DIM
DIL_W = DIL_HEADS * HEAD_DIM
IN_W = 3 * NA_W + SWA_QW + 2 * SWA_KVW + 3 * DIL_W
MIX_W = NA_W + SWA_QW + DIL_W
FFN_HIDDEN = 2816
NORM_EPS = 1e-6
NEG_INF = -1e30
QK_SCALE = HEAD_DIM ** -0.5
LOG2E = math.log2(math.e)
Q_SCALE = QK_SCALE * LOG2E

LANES = 128
VMEM_LIMIT = 56 * 1024 * 1024

QB = 128
QBT = 256
PREP_ROWS = 256
NQB = SEQ // QB
ROW_TILE = 512
MXU_DIM = 256
FFN_H_CHUNKS = (6 * MXU_DIM, 5 * MXU_DIM)
assert sum(FFN_H_CHUNKS) == FFN_HIDDEN
DIL_LAG = 1

PROJ_T_Q = NA_W + SWA_QW
PROJ_T_V = NA_W + SWA_KVW
PROJ_R_K = NA_W + SWA_KVW
PROJ_R = PROJ_R_K + 3 * DIL_W
PROJ_R_CHUNK = 768

NA_WIN = 768
NA_VARIANTS = 3
NA_LAST_START = SEQ - NA_WIN
SWA_WIN = QBT + 2 * SWA_HALF_WINDOW

F32 = jnp.float32
BF16 = jnp.bfloat16


def _dot(a, b):
    return jnp.dot(a, b, preferred_element_type=F32)


def _dot_nt(a, b):
    return lax.dot_general(a, b, (((1,), (1,)), ((), ())), preferred_element_type=F32)


def _dot_tn(a, b):
    return lax.dot_general(a, b, (((0,), (0,)), ((), ())), preferred_element_type=F32)


def _low_half(shape):
    return lax.broadcasted_iota(jnp.int32, shape, len(shape) - 1) < HEAD_DIM


def _pair_rms(x, gain):
    lo = _low_half(x.shape)
    sq = x * x
    s_lo = jnp.sum(jnp.where(lo, sq, 0.0), axis=-1, keepdims=True)
    s_hi = jnp.sum(jnp.where(lo, 0.0, sq), axis=-1, keepdims=True)
    ms = jnp.where(lo, s_lo, s_hi) * (1.0 / HEAD_DIM)
    return x * lax.rsqrt(ms + NORM_EPS) * gain


def _softmax_block(s):
    m = jnp.max(s, axis=-1, keepdims=True)
    return m, jnp.exp2(s - m).astype(BF16)


def _proj_kernel(x_ref, g_ref, wt_ref, wr_ref, gq_ref, gr_ref,
                 qa_ref, qb_ref, va_ref, vb_ref, k_ref, dil_ref, dil4_ref, dil16_ref, regroup_scr):
    x = x_ref[...]
    ms = jnp.mean(x * x, axis=-1, keepdims=True)
    h = (x * lax.rsqrt(ms + NORM_EPS) * g_ref[...]).astype(BF16)
    tm = h.shape[0]

    qt = _dot_nt(wt_ref[0:PROJ_T_Q, :], h)
    q3 = qt.reshape(PROJ_T_Q // HEAD_DIM, HEAD_DIM, tm)
    q3 = q3 * lax.rsqrt(jnp.mean(q3 * q3, axis=1, keepdims=True) + NORM_EPS)
    qt = q3.reshape(PROJ_T_Q, tm) * gq_ref[...] * Q_SCALE
    qa_ref[...] = qt[0:NA_W].astype(BF16)
    qb_ref[...] = qt[NA_W:PROJ_T_Q].astype(BF16)
    vt = _dot_nt(wt_ref[PROJ_T_Q:PROJ_T_Q + PROJ_T_V, :], h)
    va_ref[...] = vt[0:NA_W].astype(BF16)
    vb_ref[...] = vt[NA_W:PROJ_T_V].astype(BF16)

    n_norm = (PROJ_R_K + 2 * DIL_W) // LANES
    n_key = PROJ_R_K // LANES
    n_dil_q = DIL_W // LANES
    for c in range(PROJ_R // PROJ_R_CHUNK):
        r = _dot(h, wr_ref[:, c * PROJ_R_CHUNK:(c + 1) * PROJ_R_CHUNK])
        for t in range(PROJ_R_CHUNK // LANES):
            blk = c * (PROJ_R_CHUNK // LANES) + t
            tile = r[:, t * LANES:(t + 1) * LANES]
            if blk < n_norm:
                tile = _pair_rms(tile, gr_ref[:, blk * LANES:(blk + 1) * LANES])
            if n_key <= blk < n_key + n_dil_q:
                tile = tile * Q_SCALE
            if blk < n_key:
                k_ref[:, blk * LANES:(blk + 1) * LANES] = tile.astype(BF16)
                continue
            d = blk - n_key
            cols = slice(d * LANES, (d + 1) * LANES)
            dil_ref[:, cols] = tile.astype(BF16)
            fine, coarse = DIL_STRIDES[1], DIL_STRIDES[2] // DIL_STRIDES[1]
            per_fine = tm // fine
            regroup_scr[d, 0] = tile
            for ph in range(fine):
                part = regroup_scr[d, 0, pl.ds(ph, per_fine, stride=fine), :]
                dil4_ref[0, ph, :, cols] = part.astype(BF16)
                regroup_scr[d, 1, ph * per_fine:(ph + 1) * per_fine, :] = part
            for ph in range(fine):
                for sub_ph in range(coarse):
                    part = regroup_scr[d, 1, pl.ds(ph * per_fine + sub_ph, per_fine // coarse, stride=coarse), :]
                    dil16_ref[0, ph + fine * sub_ph, :, cols] = part.astype(BF16)


def _group_rms_t(yt, gain_col):
    yf = yt.astype(F32)
    ms = jnp.mean(yf * yf, axis=0, keepdims=True)
    return (yf * lax.rsqrt(ms + NORM_EPS) * gain_col).astype(BF16)


def _out_ffn_kernel(ya_ref, yb_ref, yc_ref, x_ref, gab_ref, gc_ref, wo_ref, gf_ref, wgu_ref, wd_ref,
                    o_ref, h_scr):
    ya = _group_rms_t(ya_ref[...], gab_ref[0:NA_W, :])
    yb = _group_rms_t(yb_ref[...], gab_ref[NA_W:NA_W + SWA_QW, :])
    yc = yc_ref[...].astype(F32)
    ms = jnp.mean(yc * yc, axis=-1, keepdims=True)
    yc = (yc * lax.rsqrt(ms + NORM_EPS) * gc_ref[...]).astype(BF16)
    acc = _dot_tn(ya, wo_ref[0:NA_W, :])
    acc = acc + _dot_tn(yb, wo_ref[NA_W:NA_W + SWA_QW, :])
    acc = acc + _dot(yc, wo_ref[NA_W + SWA_QW:MIX_W, :])
    x = x_ref[...] + acc

    ms = jnp.mean(x * x, axis=-1, keepdims=True)
    h_scr[...] = (x * lax.rsqrt(ms + NORM_EPS) * gf_ref[...]).astype(BF16)
    o_ref[...] = x
    lo = 0
    for width in FFN_H_CHUNKS:
        gate = _dot(h_scr[...], wgu_ref[:, lo:lo + width])
        up = _dot(h_scr[...], wgu_ref[:, FFN_HIDDEN + lo:FFN_HIDDEN + lo + width])
        act = (gate / (1.0 + jnp.exp(-gate)) * up).astype(BF16)
        o_ref[...] += _dot(act, wd_ref[lo:lo + width, :])
        lo += width


def _pipeline3(chains, stage_a, stage_b, stage_c, lag=1):
    n = len(chains)
    a, b = {}, {}
    for i in range(-2 * lag, n):
        if 0 <= i + 2 * lag < n:
            a[i + 2 * lag] = stage_a(chains[i + 2 * lag])
        if 0 <= i + lag < n:
            b[i + lag] = stage_b(chains[i + lag], a.pop(i + lag))
        if i >= 0:
            stage_c(chains[i], b.pop(i))


def _attend_t(chains):
    def scores(c):
        qt = c["q"]()
        zeros = jnp.zeros_like(qt)
        qpad = jnp.concatenate([qt, zeros] if c["half"] == 0 else [zeros, qt], axis=0)
        kw = c["k"]()
        mid = kw.shape[0] // 2
        s = jnp.concatenate([_dot(kw[:mid], qpad), _dot(kw[mid:], qpad)], axis=0)
        return s + c["bias"]()

    def softmax(c, s):
        m = jnp.max(s, axis=0, keepdims=True)
        if c["sink"] is not None:
            m = jnp.maximum(m, c["sink"])
        return jnp.exp2(s - m).astype(BF16), m

    def values(c, pm):
        p, m = pm
        vt = c["v"]()
        own = vt[c["half"] * HEAD_DIM:(c["half"] + 1) * HEAD_DIM, :]
        ones = jnp.ones_like(own)
        o = _dot(jnp.concatenate([own, ones], axis=0), p)
        den = o[HEAD_DIM:2 * HEAD_DIM, :]
        if c["sink"] is not None:
            den = den + jnp.exp2(c["sink"] - m)
        c["store"](o[0:HEAD_DIM, :] / den)

    _pipeline3(chains, scores, softmax, values)


def _na_kernel(q_ref, k_ref, v_ref, tbl_ref, o_ref):
    def store(rows, cols):
        def put(o):
            o_ref[rows, cols] = o.astype(BF16)
        return put

    chains = []
    for j in range(SEQ // QBT):
        q0 = j * QBT
        start = min(max(q0 - QBT, 0), NA_LAST_START)
        variant = (q0 - start) // QBT
        cols = slice(q0, q0 + QBT)
        keys = slice(start, start + NA_WIN)
        for h in range(NA_HEADS):
            rows = slice(h * HEAD_DIM, (h + 1) * HEAD_DIM)
            pair = slice((h // 2) * LANES, (h // 2 + 1) * LANES)
            chains.append(dict(
                half=h % 2, sink=None,
                q=lambda rows=rows, cols=cols: q_ref[rows, cols],
                k=lambda keys=keys, pair=pair: k_ref[keys, pair],
                v=lambda keys=keys, pair=pair: v_ref[pair, keys],
                bias=lambda h=h, variant=variant: tbl_ref[h, variant],
                store=store(rows, cols)))
    _attend_t(chains)


def _na_bias_table(rpb):
    rows = SEQ // GRID_W
    n_blocks = SEQ // QBT
    rep_block = np.array([0, 1, n_blocks - 1])
    start = np.clip(rep_block * QBT - QBT, 0, NA_LAST_START)
    assert np.array_equal((rep_block * QBT - start) // QBT, np.arange(NA_VARIANTS))
    qq = np.arange(QBT)
    kk = np.arange(NA_WIN)
    qrow = (rep_block[:, None] * QBT + qq[None, :]) // GRID_W
    qcol = qq % GRID_W
    krow = (start[:, None] + kk[None, :]) // GRID_W
    kcol = kk % GRID_W
    row_start = np.clip(qrow - NA_ROWS // 2, 0, rows - NA_ROWS)
    col_start = np.clip(qcol - NA_COLS // 2, 0, GRID_W - NA_COLS)
    row_ok = (krow[:, :, None] >= row_start[:, None, :]) & (krow[:, :, None] < row_start[:, None, :] + NA_ROWS)
    col_ok = (kcol[:, None] >= col_start[None, :]) & (kcol[:, None] < col_start[None, :] + NA_COLS)
    valid = row_ok & col_ok[None]
    heads, n_row_rel, n_col_rel = rpb.shape
    period = 2 * GRID_W - 1
    lead = GRID_W - NA_COLS
    seq = jnp.pad(rpb.astype(F32)[:, :, ::-1], ((0, 0), (0, 0), (lead, period - lead - n_col_rel)))
    skew = jnp.tile(seq, (1, 1, GRID_W + 1))[..., :GRID_W * 2 * GRID_W]
    toe = skew.reshape(heads, n_row_rel, GRID_W, 2 * GRID_W)[:, :, ::-1, :GRID_W]
    key_rows, query_rows = NA_WIN // GRID_W, QBT // GRID_W
    first = (start // GRID_W)[:, None] - (rep_block * query_rows)[:, None] - np.arange(query_rows)[None, :] \
        + NA_ROWS - 1
    pad_lo = max(0, -int(first.min()))
    pad_hi = max(0, int(first.max()) + key_rows - n_row_rel)
    toe = jnp.pad(toe, ((0, 0), (pad_lo, pad_hi), (0, 0), (0, 0)))
    bias = jnp.stack([
        jnp.concatenate([toe[:, pad_lo + a:pad_lo + a + key_rows] for a in first[v]], axis=-1)
        for v in range(NA_VARIANTS)], axis=1)
    bias = bias.reshape(heads, NA_VARIANTS, NA_WIN, QBT)
    return jnp.where(valid[None], bias * LOG2E, NEG_INF)


def _swa_kernel(slopes_ref, sink_ref, q_ref, k_ref, v_ref, o_ref, bias_scr):
    @pl.when(pl.program_id(0) == 0)
    def _():
        kk = lax.broadcasted_iota(jnp.int32, (SWA_WIN, QBT), 0)
        qq = lax.broadcasted_iota(jnp.int32, (SWA_WIN, QBT), 1)
        dist = jnp.abs(kk - SWA_HALF_WINDOW - qq)
        in_window = dist <= SWA_HALF_WINDOW
        dist_f = dist.astype(F32)
        for h in range(SWA_Q_HEADS):
            bias_scr[h] = jnp.where(in_window, -slopes_ref[h] * dist_f * LOG2E, NEG_INF)

    def store(rows, cols):
        def put(o):
            o_ref[rows, cols] = o.astype(BF16)
        return put

    chains = []
    for j in range(SEQ // QBT):
        q0 = j * QBT
        lo = max(q0 - SWA_HALF_WINDOW, 0)
        hi = min(q0 + QBT + SWA_HALF_WINDOW, SEQ)
        first = lo - (q0 - SWA_HALF_WINDOW)
        cols = slice(q0, q0 + QBT)
        keys = slice(lo, hi)
        win = slice(first, first + hi - lo)
        for h in range(SWA_Q_HEADS):
            rows = slice(h * HEAD_DIM, (h + 1) * HEAD_DIM)
            chains.append(dict(
                half=h // SWA_GROUP, sink=sink_ref[h] * LOG2E,
                q=lambda rows=rows, cols=cols: q_ref[rows, cols],
                k=lambda keys=keys: k_ref[keys, :],
                v=lambda keys=keys: v_ref[:, keys],
                bias=lambda h=h, win=win: bias_scr[h, win, :],
                store=store(rows, cols)))
    _attend_t(chains)


DIL_WIN = QB + 2 * DIL_HALF


def _dil_kernel(slopes_ref, q1_ref, k1_ref, v1_ref, q4_ref, k4_ref, v4_ref, q16_ref, k16_ref, v16_ref,
                o_ref, bias_scr, bias16_scr, m_scr, l_scr, acc_scr):
    pair = pl.program_id(1)

    @pl.when(pl.program_id(0) == 0)
    def _():
        slope = [slopes_ref[SWA_Q_HEADS + 2 * pair + h] for h in range(2)]
        qq = lax.broadcasted_iota(jnp.int32, (QB, DIL_WIN), 0)
        kk = lax.broadcasted_iota(jnp.int32, (QB, DIL_WIN), 1)
        for shift in range(3):
            dist = jnp.abs(kk - shift * DIL_HALF - qq)
            ok = dist <= DIL_HALF
            dist_f = dist.astype(F32)
            for h in range(2):
                for b, stride in enumerate(DIL_STRIDES[:2]):
                    bias_scr[pair, b, shift, h * QB:(h + 1) * QB, :] = jnp.where(
                        ok, -slope[h] * (dist_f * float(stride)) * LOG2E, NEG_INF)
        qq = lax.broadcasted_iota(jnp.int32, (QB, QB), 0)
        kk = lax.broadcasted_iota(jnp.int32, (QB, QB), 1)
        dist = jnp.abs(kk - qq)
        for h in range(2):
            bias16_scr[pair, h * QB:(h + 1) * QB, :] = jnp.where(
                dist <= DIL_HALF, -slope[h] * (dist.astype(F32) * float(DIL_STRIDES[2])) * LOG2E, NEG_INF)

    chains = []
    for j in range(NQB):
        for b, stride in enumerate(DIL_STRIDES):
            sub = SEQ // stride
            phase, local_q = divmod(j * QB, sub)
            if sub == QB:
                local_k, n_keys, shift = 0, QB, None
            else:
                local_k = min(max(local_q - DIL_HALF, 0), sub - DIL_WIN)
                n_keys, shift = DIL_WIN, (local_q - local_k) // DIL_HALF
            qrows, krows = slice(local_q, local_q + QB), slice(local_k, local_k + n_keys)
            if stride == 1:
                refs = (q1_ref, k1_ref, v1_ref)
                load = lambda ref, rows: ref[rows, :]
                dst = qrows
            else:
                refs = (q4_ref, k4_ref, v4_ref) if stride == DIL_STRIDES[1] else (q16_ref, k16_ref, v16_ref)
                load = lambda ref, rows, phase=phase: ref[0, phase, rows, :]
                dst = pl.ds(local_q * stride + phase, QB, stride=stride)
            chains.append(dict(b=b, shift=shift, dst=dst,
                               q=lambda refs=refs, load=load, rows=qrows: load(refs[0], rows),
                               k=lambda refs=refs, load=load, rows=krows: load(refs[1], rows),
                               v=lambda refs=refs, load=load, rows=krows: load(refs[2], rows)))

    def scores(c):
        q, kw = c["q"](), c["k"]()
        lo = _low_half(q.shape)
        zeros = jnp.zeros_like(q)
        q_both = jnp.concatenate([jnp.where(lo, q, zeros), jnp.where(lo, zeros, q)], axis=0)
        bias = bias16_scr[pair] if c["shift"] is None else bias_scr[pair, c["b"], c["shift"]]
        return _dot_nt(q_both, kw) + bias

    def softmax(c, s):
        return _softmax_block(s)

    def values(c, mp):
        m, p = mp
        vw = c["v"]()
        res = _dot(p, jnp.concatenate([vw, jnp.ones_like(vw)], axis=1))
        lo = _low_half((QB, LANES))
        m_scr[c["b"], c["dst"], :] = jnp.where(lo, m[0:QB], m[QB:2 * QB])
        l_scr[c["b"], c["dst"], :] = jnp.where(lo, res[0:QB, LANES:2 * LANES], res[QB:2 * QB, LANES:2 * LANES])
        acc_scr[c["b"], c["dst"], :] = jnp.where(lo, res[0:QB, 0:LANES], res[QB:2 * QB, 0:LANES])

    _pipeline3(chains, scores, softmax, values, lag=DIL_LAG)

    def merge(i, carry):
        rows = pl.ds(pl.multiple_of(i * PREP_ROWS, PREP_ROWS), PREP_ROWS)
        m = [m_scr[b, rows, :] for b in range(3)]
        top = jnp.maximum(jnp.maximum(m[0], m[1]), m[2])
        num = jnp.zeros((PREP_ROWS, LANES), F32)
        den = jnp.zeros((PREP_ROWS, LANES), F32)
        for b in range(3):
            w = jnp.exp2(m[b] - top)
            num = num + w * acc_scr[b, rows, :]
            den = den + w * l_scr[b, rows, :]
        o_ref[rows, :] = (num / den).astype(BF16)
        return carry

    lax.fori_loop(0, SEQ // PREP_ROWS, merge, 0)


def _params(semantics):
    return pltpu.CompilerParams(dimension_semantics=semantics, vmem_limit_bytes=VMEM_LIMIT)


def _resident(shape):
    return pl.BlockSpec(shape, lambda *_: (0,) * len(shape), pipeline_mode=pl.Buffered(1))


def _smem():
    return pl.BlockSpec(memory_space=pltpu.SMEM)


def _proj(x, gain, w_t, w_r, gain_q, gain_r):
    t = x.shape[0]
    feat = lambda n: pl.BlockSpec((n, ROW_TILE), lambda i: (0, i))
    tok = lambda n: pl.BlockSpec((ROW_TILE, n), lambda i: (i, 0))
    tiles_per_seq = SEQ // ROW_TILE

    def phased(stride):
        return pl.BlockSpec((1, stride, ROW_TILE // stride, 3 * DIL_W),
                            lambda i: (i // tiles_per_seq, 0, i % tiles_per_seq, 0))

    def phased_shape(stride):
        return jax.ShapeDtypeStruct((t // SEQ, stride, SEQ // stride, 3 * DIL_W), BF16)

    return pl.pallas_call(
        _proj_kernel,
        out_shape=[jax.ShapeDtypeStruct((NA_W, t), BF16), jax.ShapeDtypeStruct((SWA_QW, t), BF16),
                   jax.ShapeDtypeStruct((NA_W, t), BF16), jax.ShapeDtypeStruct((SWA_KVW, t), BF16),
                   jax.ShapeDtypeStruct((t, PROJ_R_K), BF16), jax.ShapeDtypeStruct((t, 3 * DIL_W), BF16),
                   phased_shape(DIL_STRIDES[1]), phased_shape(DIL_STRIDES[2])],
        grid=(t // ROW_TILE,),
        in_specs=[tok(D_MODEL), _resident((1, D_MODEL)),
                  _resident((PROJ_T_Q + PROJ_T_V, D_MODEL)), _resident((D_MODEL, PROJ_R)),
                  _resident((PROJ_T_Q, 1)), _resident((1, PROJ_R_K + 2 * DIL_W))],
        out_specs=[feat(NA_W), feat(SWA_QW), feat(NA_W), feat(SWA_KVW), tok(PROJ_R_K), tok(3 * DIL_W),
                   phased(DIL_STRIDES[1]), phased(DIL_STRIDES[2])],
        scratch_shapes=[pltpu.VMEM((3 * DIL_W // LANES, 2, ROW_TILE, LANES), F32)],
        compiler_params=_params(("parallel",)),
        name="proj",
    )(x, gain, w_t, w_r, gain_q, gain_r)


def _na(q_t, keys, v_t, table):
    t = q_t.shape[1]
    feat = pl.BlockSpec((NA_W, SEQ), lambda b: (0, b))
    return pl.pallas_call(
        _na_kernel,
        out_shape=jax.ShapeDtypeStruct((NA_W, t), BF16),
        grid=(t // SEQ,),
        in_specs=[feat, pl.BlockSpec((SEQ, NA_W), lambda b: (b, 0)), feat,
                  _resident((NA_HEADS, NA_VARIANTS, NA_WIN, QBT))],
        out_specs=feat,
        compiler_params=_params(("parallel",)),
        name="na",
    )(q_t, keys, v_t, table)


def _swa(q_t, keys, v_t, slopes, sink):
    t = q_t.shape[1]
    return pl.pallas_call(
        _swa_kernel,
        out_shape=jax.ShapeDtypeStruct((SWA_QW, t), BF16),
        grid=(t // SEQ,),
        in_specs=[_smem(), _smem(),
                  pl.BlockSpec((SWA_QW, SEQ), lambda b: (0, b)),
                  pl.BlockSpec((SEQ, SWA_KVW), lambda b: (b, NA_W // SWA_KVW)),
                  pl.BlockSpec((SWA_KVW, SEQ), lambda b: (0, b))],
        out_specs=pl.BlockSpec((SWA_QW, SEQ), lambda b: (0, b)),
        scratch_shapes=[pltpu.VMEM((SWA_Q_HEADS, SWA_WIN, QBT), F32)],
        compiler_params=_params(("arbitrary",)),
        name="swa",
    )(slopes, sink, q_t, keys, v_t)


def _dil(qkv, qkv4, qkv16, slopes):
    t = qkv.shape[0]
    pairs = DIL_W // LANES
    seq_block = lambda first: pl.BlockSpec((SEQ, LANES), lambda b, p: (b, first + p))

    def phased(stride, first):
        return pl.BlockSpec((1, stride, SEQ // stride, LANES), lambda b, p: (b, 0, 0, first + p))

    operands, specs = [], []
    for arr, spec in ((qkv, seq_block), (qkv4, lambda f: phased(DIL_STRIDES[1], f)),
                      (qkv16, lambda f: phased(DIL_STRIDES[2], f))):
        for part in range(3):
            operands.append(arr)
            specs.append(spec(part * pairs))
    return pl.pallas_call(
        _dil_kernel,
        out_shape=jax.ShapeDtypeStruct((t, DIL_W), BF16),
        grid=(t // SEQ, pairs),
        in_specs=[_smem()] + specs,
        out_specs=seq_block(0),
        scratch_shapes=[pltpu.VMEM((pairs, 2, 3, 2 * QB, DIL_WIN), F32),
                        pltpu.VMEM((pairs, 2 * QB, QB), F32),
                        pltpu.VMEM((3, SEQ, LANES), F32),
                        pltpu.VMEM((3, SEQ, LANES), F32),
                        pltpu.VMEM((3, SEQ, LANES), F32)],
        compiler_params=_params(("arbitrary", "arbitrary")),
        name="dil",
    )(slopes, *operands)


def _out_ffn(ya_t, yb_t, yc, x, gain_ab, gain_c, w_out, gain_ffn, w_gu, w_down):
    t = x.shape[0]
    feat = lambda n: pl.BlockSpec((n, ROW_TILE), lambda i: (0, i))
    tok = lambda n: pl.BlockSpec((ROW_TILE, n), lambda i: (i, 0))
    return pl.pallas_call(
        _out_ffn_kernel,
        out_shape=jax.ShapeDtypeStruct((t, D_MODEL), F32),
        grid=(t // ROW_TILE,),
        in_specs=[feat(NA_W), feat(SWA_QW), tok(DIL_W), tok(D_MODEL),
                  _resident((NA_W + SWA_QW, 1)), _resident((1, DIL_W)), _resident((MIX_W, D_MODEL)),
                  _resident((1, D_MODEL)),
                  _resident((D_MODEL, 2 * FFN_HIDDEN)), _resident((FFN_HIDDEN, D_MODEL))],
        out_specs=tok(D_MODEL),
        scratch_shapes=[pltpu.VMEM((ROW_TILE, D_MODEL), BF16)],
        compiler_params=_params(("parallel",)),
        name="out_ffn",
    )(ya_t, yb_t, yc, x, gain_ab, gain_c, w_out, gain_ffn, w_gu, w_down)


def _split_w_in(w):
    cuts = np.cumsum([NA_W, NA_W, NA_W, SWA_QW, SWA_KVW, SWA_KVW, DIL_W, DIL_W])
    qa, ka, va, qb, kb, vb, qc, kc, vc = jnp.split(w, [int(c) for c in cuts], axis=1)
    w_t = jnp.concatenate([qa, qb, va, vb], axis=1).T
    w_r = jnp.concatenate([ka, kb, qc, kc, vc], axis=1)
    return w_t.astype(BF16), w_r.astype(BF16)


def kernel(x, attn_norm, w_in, qk_gain, rpb, sink, out_gain, w_out, ffn_norm, w_gu, w_down):
    batch, seq, d = x.shape
    assert (seq, d) == (SEQ, D_MODEL)
    depth = w_in.shape[0]
    n_slopes = SWA_Q_HEADS + DIL_HEADS
    slopes = 2.0 ** (-8.0 * (jnp.arange(n_slopes, dtype=F32) + 1.0) / n_slopes)
    xf = x.reshape(batch * seq, d)
    for l in range(depth):
        g = qk_gain[l].astype(F32)
        gain_q = jnp.concatenate([jnp.tile(g[0, 0], NA_HEADS), jnp.tile(g[1, 0], SWA_Q_HEADS)])[:, None]
        gain_r = jnp.concatenate([jnp.tile(g[0, 1], NA_HEADS), jnp.tile(g[1, 1], SWA_KV_HEADS),
                                  jnp.tile(g[2, 0], DIL_HEADS), jnp.tile(g[2, 1], DIL_HEADS)])[None, :]
        w_t, w_r = _split_w_in(w_in[l])
        qa_t, qb_t, va_t, vb_t, keys, dil1, dil4, dil16 = _proj(
            xf, attn_norm[l][None, :], w_t, w_r, gain_q, gain_r)
        ya_t = _na(qa_t, keys, va_t, _na_bias_table(rpb[l]))
        yb_t = _swa(qb_t, keys, vb_t, slopes, sink[l].astype(F32))
        yc = _dil(dil1, dil4, dil16, slopes)
        og = out_gain[l].astype(F32)
        xf = _out_ffn(ya_t, yb_t, yc, xf, og[:NA_W + SWA_QW, None], og[None, NA_W + SWA_QW:],
                      w_out[l].astype(BF16), ffn_norm[l][None, :], w_gu[l].astype(BF16), w_down[l].astype(BF16))
    return xf.reshape(batch, seq, d)
```

```python
import functools
import math

import numpy as np
import jax
import jax.numpy as jnp
from jax import lax
from jax.experimental import pallas as pl
from jax.experimental.pallas import tpu as pltpu

D_MODEL = 1024
SEQ = 2048
HEAD_DIM = 64
GRID_W = 64
NA_HEADS = 4
NA_ROWS = 8
NA_COLS = 16
SWA_Q_HEADS = 6
SWA_KV_HEADS = 2
SWA_GROUP = SWA_Q_HEADS // SWA_KV_HEADS
SWA_HALF_WINDOW = 128
DIL_HEADS = 6
DIL_STRIDES = (1, 4, 16)
DIL_HALF = 64
NA_W = NA_HEADS * HEAD_DIM
SWA_QW = SWA_Q_HEADS * HEAD_DIM
SWA_KVW = SWA_KV_HEADS * HEAD_DIM
DIL_W = DIL_HEADS * HEAD_DIM
IN_W = 3 * NA_W + SWA_QW + 2 * SWA_KVW + 3 * DIL_W
MIX_W = NA_W + SWA_QW + DIL_W
FFN_HIDDEN = 2816
NORM_EPS = 1e-6
NEG_INF = -1e30
QK_SCALE = HEAD_DIM ** -0.5
LOG2E = math.log2(math.e)
Q_SCALE = QK_SCALE * LOG2E

LANES = 128
VMEM_LIMIT = 56 * 1024 * 1024

QB = 128
QBT = 256
PREP_ROWS = 256
NQB = SEQ // QB
ROW_TILE = 512
PROJ_ROW_TILE = 1024
MXU_DIM = 256
FFN_H_CHUNKS = (6 * MXU_DIM, 5 * MXU_DIM)
assert sum(FFN_H_CHUNKS) == FFN_HIDDEN
DIL_LAG = 1

PROJ_T_Q = NA_W + SWA_QW
PROJ_T_V = NA_W + SWA_KVW
PROJ_R_K = NA_W + SWA_KVW
PROJ_R = PROJ_R_K + 3 * DIL_W
PROJ_R_CHUNK = 768

NA_WIN = 768
NA_VARIANTS = 3
NA_LAST_START = SEQ - NA_WIN
SWA_WIN = QBT + 2 * SWA_HALF_WINDOW

F32 = jnp.float32
BF16 = jnp.bfloat16


def _dot(a, b):
    return jnp.dot(a, b, preferred_element_type=F32)


def _dot_nt(a, b):
    return lax.dot_general(a, b, (((1,), (1,)), ((), ())), preferred_element_type=F32)


def _dot_tn(a, b):
    return lax.dot_general(a, b, (((0,), (0,)), ((), ())), preferred_element_type=F32)


def _low_half(shape):
    return lax.broadcasted_iota(jnp.int32, shape, len(shape) - 1) < HEAD_DIM


def _pair_rms(x, gain):
    lo = _low_half(x.shape)
    sq = x * x
    s_lo = jnp.sum(jnp.where(lo, sq, 0.0), axis=-1, keepdims=True)
    s_hi = jnp.sum(jnp.where(lo, 0.0, sq), axis=-1, keepdims=True)
    ms = jnp.where(lo, s_lo, s_hi) * (1.0 / HEAD_DIM)
    return x * lax.rsqrt(ms + NORM_EPS) * gain


def _softmax_block(s):
    m = jnp.max(s, axis=-1, keepdims=True)
    return m, jnp.exp2(s - m).astype(BF16)


def _proj_kernel(x_ref, g_ref, wt_ref, wr_ref, gq_ref, gr_ref,
                 qa_ref, qb_ref, va_ref, vb_ref, k_ref, dil_ref, dil4_ref, dil16_ref, regroup_scr):
    x = x_ref[...]
    ms = jnp.mean(x * x, axis=-1, keepdims=True)
    h = (x * lax.rsqrt(ms + NORM_EPS) * g_ref[...]).astype(BF16)
    tm = h.shape[0]

    qt = _dot_nt(wt_ref[0:PROJ_T_Q, :], h)
    q3 = qt.reshape(PROJ_T_Q // HEAD_DIM, HEAD_DIM, tm)
    q3 = q3 * lax.rsqrt(jnp.mean(q3 * q3, axis=1, keepdims=True) + NORM_EPS)
    qt = q3.reshape(PROJ_T_Q, tm) * gq_ref[...] * Q_SCALE
    qa_ref[...] = qt[0:NA_W].astype(BF16)
    qb_ref[...] = qt[NA_W:PROJ_T_Q].astype(BF16)
    vt = _dot_nt(wt_ref[PROJ_T_Q:PROJ_T_Q + PROJ_T_V, :], h)
    va_ref[...] = vt[0:NA_W].astype(BF16)
    vb_ref[...] = vt[NA_W:PROJ_T_V].astype(BF16)

    n_norm = (PROJ_R_K + 2 * DIL_W) // LANES
    n_key = PROJ_R_K // LANES
    n_dil_q = DIL_W // LANES
    for c in range(PROJ_R // PROJ_R_CHUNK):
        r = _dot(h, wr_ref[:, c * PROJ_R_CHUNK:(c + 1) * PROJ_R_CHUNK])
        for t in range(PROJ_R_CHUNK // LANES):
            blk = c * (PROJ_R_CHUNK // LANES) + t
            tile = r[:, t * LANES:(t + 1) * LANES]
            if blk < n_norm:
                tile = _pair_rms(tile, gr_ref[:, blk * LANES:(blk + 1) * LANES])
            if n_key <= blk < n_key + n_dil_q:
                tile = tile * Q_SCALE
            if blk < n_key:
                k_ref[:, blk * LANES:(blk + 1) * LANES] = tile.astype(BF16)
                continue
            d = blk - n_key
            cols = slice(d * LANES, (d + 1) * LANES)
            dil_ref[:, cols] = tile.astype(BF16)
            fine, coarse = DIL_STRIDES[1], DIL_STRIDES[2] // DIL_STRIDES[1]
            per_fine = tm // fine
            regroup_scr[d, 0] = tile
            for ph in range(fine):
                part = regroup_scr[d, 0, pl.ds(ph, per_fine, stride=fine), :]
                dil4_ref[0, ph, :, cols] = part.astype(BF16)
                regroup_scr[d, 1, ph * per_fine:(ph + 1) * per_fine, :] = part
            for ph in range(fine):
                for sub_ph in range(coarse):
                    part = regroup_scr[d, 1, pl.ds(ph * per_fine + sub_ph, per_fine // coarse, stride=coarse), :]
                    dil16_ref[0, ph + fine * sub_ph, :, cols] = part.astype(BF16)


def _group_rms_t(yt, gain_col):
    yf = yt.astype(F32)
    ms = jnp.mean(yf * yf, axis=0, keepdims=True)
    return (yf * lax.rsqrt(ms + NORM_EPS) * gain_col).astype(BF16)


def _out_ffn_kernel(ya_ref, yb_ref, yc_ref, x_ref, gab_ref, gc_ref, wo_ref, gf_ref, wgu_ref, wd_ref,
                    o_ref, h_scr):
    ya = _group_rms_t(ya_ref[...], gab_ref[0:NA_W, :])
    yb = _group_rms_t(yb_ref[...], gab_ref[NA_W:NA_W + SWA_QW, :])
    yc = yc_ref[...].astype(F32)
    ms = jnp.mean(yc * yc, axis=-1, keepdims=True)
    yc = (yc * lax.rsqrt(ms + NORM_EPS) * gc_ref[...]).astype(BF16)
    acc = _dot_tn(ya, wo_ref[0:NA_W, :])
    acc = acc + _dot_tn(yb, wo_ref[NA_W:NA_W + SWA_QW, :])
    acc = acc + _dot(yc, wo_ref[NA_W + SWA_QW:MIX_W, :])
    x = x_ref[...] + acc

    ms = jnp.mean(x * x, axis=-1, keepdims=True)
    h_scr[...] = (x * lax.rsqrt(ms + NORM_EPS) * gf_ref[...]).astype(BF16)
    o_ref[...] = x
    lo = 0
    for width in FFN_H_CHUNKS:
        gate = _dot(h_scr[...], wgu_ref[:, lo:lo + width])
        up = _dot(h_scr[...], wgu_ref[:, FFN_HIDDEN + lo:FFN_HIDDEN + lo + width])
        act = (gate / (1.0 + jnp.exp(-gate)) * up).astype(BF16)
        o_ref[...] += _dot(act, wd_ref[lo:lo + width, :])
        lo += width


def _pipeline3(chains, stage_a, stage_b, stage_c, lag=1):
    n = len(chains)
    a, b = {}, {}
    for i in range(-2 * lag, n):
        if 0 <= i + 2 * lag < n:
            a[i + 2 * lag] = stage_a(chains[i + 2 * lag])
        if 0 <= i + lag < n:
            b[i + lag] = stage_b(chains[i + lag], a.pop(i + lag))
        if i >= 0:
            stage_c(chains[i], b.pop(i))


def _attend_t(chains):
    def scores(c):
        qt = c["q"]()
        zeros = jnp.zeros_like(qt)
        qpad = jnp.concatenate([qt, zeros] if c["half"] == 0 else [zeros, qt], axis=0)
        kw = c["k"]()
        mid = kw.shape[0] // 2
        s = jnp.concatenate([_dot(kw[:mid], qpad), _dot(kw[mid:], qpad)], axis=0)
        return s + c["bias"]()

    def softmax(c, s):
        m = jnp.max(s, axis=0, keepdims=True)
        if c["sink"] is not None:
            m = jnp.maximum(m, c["sink"])
        return jnp.exp2(s - m).astype(BF16), m

    def values(c, pm):
        p, m = pm
        vt = c["v"]()
        own = vt[c["half"] * HEAD_DIM:(c["half"] + 1) * HEAD_DIM, :]
        ones = jnp.ones_like(own)
        o = _dot(jnp.concatenate([own, ones], axis=0), p)
        den = o[HEAD_DIM:2 * HEAD_DIM, :]
        if c["sink"] is not None:
            den = den + jnp.exp2(c["sink"] - m)
        c["store"](o[0:HEAD_DIM, :] / den)

    _pipeline3(chains, scores, softmax, values)


def _na_kernel(q_ref, k_ref, v_ref, tbl_ref, o_ref):
    def store(rows, cols):
        def put(o):
            o_ref[rows, cols] = o.astype(BF16)
        return put

    chains = []
    for j in range(SEQ // QBT):
        q0 = j * QBT
        start = min(max(q0 - QBT, 0), NA_LAST_START)
        variant = (q0 - start) // QBT
        cols = slice(q0, q0 + QBT)
        keys = slice(start, start + NA_WIN)
        for h in range(NA_HEADS):
            rows = slice(h * HEAD_DIM, (h + 1) * HEAD_DIM)
            pair = slice((h // 2) * LANES, (h // 2 + 1) * LANES)
            chains.append(dict(
                half=h % 2, sink=None,
                q=lambda rows=rows, cols=cols: q_ref[rows, cols],
                k=lambda keys=keys, pair=pair: k_ref[keys, pair],
                v=lambda keys=keys, pair=pair: v_ref[pair, keys],
                bias=lambda h=h, variant=variant: tbl_ref[h, variant],
                store=store(rows, cols)))
    _attend_t(chains)


def _na_bias_table(rpb):
    rows = SEQ // GRID_W
    n_blocks = SEQ // QBT
    rep_block = np.array([0, 1, n_blocks - 1])
    start = np.clip(rep_block * QBT - QBT, 0, NA_LAST_START)
    assert np.array_equal((rep_block * QBT - start) // QBT, np.arange(NA_VARIANTS))
    qq = np.arange(QBT)
    kk = np.arange(NA_WIN)
    qrow = (rep_block[:, None] * QBT + qq[None, :]) // GRID_W
    qcol = qq % GRID_W
    krow = (start[:, None] + kk[None, :]) // GRID_W
    kcol = kk % GRID_W
    row_start = np.clip(qrow - NA_ROWS // 2, 0, rows - NA_ROWS)
    col_start = np.clip(qcol - NA_COLS // 2, 0, GRID_W - NA_COLS)
    row_ok = (krow[:, :, None] >= row_start[:, None, :]) & (krow[:, :, None] < row_start[:, None, :] + NA_ROWS)
    col_ok = (kcol[:, None] >= col_start[None, :]) & (kcol[:, None] < col_start[None, :] + NA_COLS)
    valid = row_ok & col_ok[None]
    heads, n_row_rel, n_col_rel = rpb.shape
    period = 2 * GRID_W - 1
    lead = GRID_W - NA_COLS
    seq = jnp.pad(rpb.astype(F32)[:, :, ::-1], ((0, 0), (0, 0), (lead, period - lead - n_col_rel)))
    skew = jnp.tile(seq, (1, 1, GRID_W + 1))[..., :GRID_W * 2 * GRID_W]
    toe = skew.reshape(heads, n_row_rel, GRID_W, 2 * GRID_W)[:, :, ::-1, :GRID_W]
    key_rows, query_rows = NA_WIN // GRID_W, QBT // GRID_W
    first = (start // GRID_W)[:, None] - (rep_block * query_rows)[:, None] - np.arange(query_rows)[None, :] \
        + NA_ROWS - 1
    pad_lo = max(0, -int(first.min()))
    pad_hi = max(0, int(first.max()) + key_rows - n_row_rel)
    toe = jnp.pad(toe, ((0, 0), (pad_lo, pad_hi), (0, 0), (0, 0)))
    bias = jnp.stack([
        jnp.concatenate([toe[:, pad_lo + a:pad_lo + a + key_rows] for a in first[v]], axis=-1)
        for v in range(NA_VARIANTS)], axis=1)
    bias = bias.reshape(heads, NA_VARIANTS, NA_WIN, QBT)
    return jnp.where(valid[None], bias * LOG2E, NEG_INF)


def _swa_kernel(layer, slopes_ref, sink_ref, q_ref, k_ref, v_ref, o_ref, bias_scr):
    @pl.when(pl.program_id(0) == 0)
    def _():
        kk = lax.broadcasted_iota(jnp.int32, (SWA_WIN, QBT), 0)
        qq = lax.broadcasted_iota(jnp.int32, (SWA_WIN, QBT), 1)
        dist = jnp.abs(kk - SWA_HALF_WINDOW - qq)
        in_window = dist <= SWA_HALF_WINDOW
        dist_f = dist.astype(F32)
        for h in range(SWA_Q_HEADS):
            bias_scr[h] = jnp.where(in_window, -slopes_ref[h] * dist_f * LOG2E, NEG_INF)

    def store(rows, cols):
        def put(o):
            o_ref[rows, cols] = o.astype(BF16)
        return put

    chains = []
    for j in range(SEQ // QBT):
        q0 = j * QBT
        lo = max(q0 - SWA_HALF_WINDOW, 0)
        hi = min(q0 + QBT + SWA_HALF_WINDOW, SEQ)
        first = lo - (q0 - SWA_HALF_WINDOW)
        cols = slice(q0, q0 + QBT)
        keys = slice(lo, hi)
        win = slice(first, first + hi - lo)
        for h in range(SWA_Q_HEADS):
            rows = slice(h * HEAD_DIM, (h + 1) * HEAD_DIM)
            chains.append(dict(
                half=h // SWA_GROUP, sink=sink_ref[layer, h] * LOG2E,
                q=lambda rows=rows, cols=cols: q_ref[rows, cols],
                k=lambda keys=keys: k_ref[keys, :],
                v=lambda keys=keys: v_ref[:, keys],
                bias=lambda h=h, win=win: bias_scr[h, win, :],
                store=store(rows, cols)))
    _attend_t(chains)


DIL_QB = 128
DIL_WIN = DIL_QB + 2 * DIL_HALF


def _dil_kernel(slopes_ref, q1_ref, k1_ref, v1_ref, q4_ref, k4_ref, v4_ref, q16_ref, k16_ref, v16_ref,
                o_ref, bias_scr, bias16_scr, m_scr, l_scr, acc_scr):
    pair = pl.program_id(1)

    @pl.when(pl.program_id(0) == 0)
    def _():
        slope = [slopes_ref[SWA_Q_HEADS + 2 * pair + h] for h in range(2)]
        qq = lax.broadcasted_iota(jnp.int32, (DIL_QB, DIL_WIN), 0)
        kk = lax.broadcasted_iota(jnp.int32, (DIL_QB, DIL_WIN), 1)
        for shift in range(3):
            dist = jnp.abs(kk - shift * DIL_HALF - qq)
            ok = dist <= DIL_HALF
            dist_f = dist.astype(F32)
            for h in range(2):
                for b, stride in enumerate(DIL_STRIDES[:2]):
                    bias_scr[pair, b, shift, h * DIL_QB:(h + 1) * DIL_QB, :] = jnp.where(
                        ok, -slope[h] * (dist_f * float(stride)) * LOG2E, NEG_INF)
        qq = lax.broadcasted_iota(jnp.int32, (QB, QB), 0)
        kk = lax.broadcasted_iota(jnp.int32, (QB, QB), 1)
        dist = jnp.abs(kk - qq)
        for h in range(2):
            bias16_scr[pair, h * QB:(h + 1) * QB, :] = jnp.where(
                dist <= DIL_HALF, -slope[h] * (dist.astype(F32) * float(DIL_STRIDES[2])) * LOG2E, NEG_INF)

    chains = []
    for b, stride in enumerate(DIL_STRIDES):
        sub = SEQ // stride
        qb = QB if sub == QB else DIL_QB
        for start in range(0, SEQ, qb):
            phase, local_q = divmod(start, sub)
            if sub == QB:
                local_k, n_keys, shift = 0, QB, None
            else:
                local_k = min(max(local_q - DIL_HALF, 0), sub - DIL_WIN)
                n_keys, shift = DIL_WIN, (local_q - local_k) // DIL_HALF
            qrows, krows = slice(local_q, local_q + qb), slice(local_k, local_k + n_keys)
            if stride == 1:
                refs = (q1_ref, k1_ref, v1_ref)
                load = lambda ref, rows: ref[rows, :]
                dst = qrows
            else:
                refs = (q4_ref, k4_ref, v4_ref) if stride == DIL_STRIDES[1] else (q16_ref, k16_ref, v16_ref)
                load = lambda ref, rows, phase=phase: ref[0, phase, rows, :]
                dst = pl.ds(local_q * stride + phase, qb, stride=stride)
            chains.append(dict(b=b, shift=shift, dst=dst, qb=qb, order=start + qb,
                               q=lambda refs=refs, load=load, rows=qrows: load(refs[0], rows),
                               k=lambda refs=refs, load=load, rows=krows: load(refs[1], rows),
                               v=lambda refs=refs, load=load, rows=krows: load(refs[2], rows)))
    chains.sort(key=lambda c: c["order"])

    def scores(c):
        q, kw = c["q"](), c["k"]()
        lo = _low_half(q.shape)
        zeros = jnp.zeros_like(q)
        q_both = jnp.concatenate([jnp.where(lo, q, zeros), jnp.where(lo, zeros, q)], axis=0)
        bias = bias16_scr[pair] if c["shift"] is None else bias_scr[pair, c["b"], c["shift"]]
        return _dot_nt(q_both, kw) + bias

    def softmax(c, s):
        return _softmax_block(s)

    def values(c, mp):
        m, p = mp
        vw = c["v"]()
        res = _dot(p, jnp.concatenate([vw, jnp.ones_like(vw)], axis=1))
        qb = c["qb"]
        lo = _low_half((qb, LANES))
        m_scr[c["b"], c["dst"], :] = jnp.where(lo, m[0:qb], m[qb:2 * qb])
        l_scr[c["b"], c["dst"], :] = jnp.where(lo, res[0:qb, LANES:2 * LANES], res[qb:2 * qb, LANES:2 * LANES])
        acc_scr[c["b"], c["dst"], :] = jnp.where(lo, res[0:qb, 0:LANES], res[qb:2 * qb, 0:LANES])

    _pipeline3(chains, scores, softmax, values, lag=DIL_LAG)

    def merge(i, carry):
        rows = pl.ds(pl.multiple_of(i * PREP_ROWS, PREP_ROWS), PREP_ROWS)
        m = [m_scr[b, rows, :] for b in range(3)]
        top = jnp.maximum(jnp.maximum(m[0], m[1]), m[2])
        num = jnp.zeros((PREP_ROWS, LANES), F32)
        den = jnp.zeros((PREP_ROWS, LANES), F32)
        for b in range(3):
            w = jnp.exp2(m[b] - top)
            num = num + w * acc_scr[b, rows, :]
            den = den + w * l_scr[b, rows, :]
        o_ref[rows, :] = (num / den).astype(BF16)
        return carry

    lax.fori_loop(0, SEQ // PREP_ROWS, merge, 0)


def _params(semantics):
    return pltpu.CompilerParams(dimension_semantics=semantics, vmem_limit_bytes=VMEM_LIMIT)


def _resident(shape, layer):
    return pl.BlockSpec((None,) + tuple(shape), lambda *_: (layer,) + (0,) * len(shape),
                        pipeline_mode=pl.Buffered(1))


def _smem():
    return pl.BlockSpec(memory_space=pltpu.SMEM)


def _proj(x, layer, gain, w_t, w_r, gain_q, gain_r):
    t = x.shape[0]
    rows = PROJ_ROW_TILE
    feat = lambda n: pl.BlockSpec((n, rows), lambda i: (0, i))
    tok = lambda n: pl.BlockSpec((rows, n), lambda i: (i, 0))
    tiles_per_seq = SEQ // rows

    def phased(stride):
        return pl.BlockSpec((1, stride, rows // stride, 3 * DIL_W),
                            lambda i: (i // tiles_per_seq, 0, i % tiles_per_seq, 0))

    def phased_shape(stride):
        return jax.ShapeDtypeStruct((t // SEQ, stride, SEQ // stride, 3 * DIL_W), BF16)

    return pl.pallas_call(
        _proj_kernel,
        out_shape=[jax.ShapeDtypeStruct((NA_W, t), BF16), jax.ShapeDtypeStruct((SWA_QW, t), BF16),
                   jax.ShapeDtypeStruct((NA_W, t), BF16), jax.ShapeDtypeStruct((SWA_KVW, t), BF16),
                   jax.ShapeDtypeStruct((t, PROJ_R_K), BF16), jax.ShapeDtypeStruct((t, 3 * DIL_W), BF16),
                   phased_shape(DIL_STRIDES[1]), phased_shape(DIL_STRIDES[2])],
        grid=(t // rows,),
        in_specs=[tok(D_MODEL), _resident((1, D_MODEL), layer),
                  _resident((PROJ_T_Q + PROJ_T_V, D_MODEL), layer), _resident((D_MODEL, PROJ_R), layer),
                  _resident((PROJ_T_Q, 1), layer), _resident((1, PROJ_R_K + 2 * DIL_W), layer)],
        out_specs=[feat(NA_W), feat(SWA_QW), feat(NA_W), feat(SWA_KVW), tok(PROJ_R_K), tok(3 * DIL_W),
                   phased(DIL_STRIDES[1]), phased(DIL_STRIDES[2])],
        scratch_shapes=[pltpu.VMEM((3 * DIL_W // LANES, 2, rows, LANES), F32)],
        compiler_params=_params(("parallel",)),
        name="proj",
    )(x, gain, w_t, w_r, gain_q, gain_r)


def _na(q_t, keys, v_t, layer, table):
    t = q_t.shape[1]
    feat = pl.BlockSpec((NA_W, SEQ), lambda b: (0, b))
    return pl.pallas_call(
        _na_kernel,
        out_shape=jax.ShapeDtypeStruct((NA_W, t), BF16),
        grid=(t // SEQ,),
        in_specs=[feat, pl.BlockSpec((SEQ, NA_W), lambda b: (b, 0)), feat,
                  _resident((NA_HEADS, NA_VARIANTS, NA_WIN, QBT), layer)],
        out_specs=feat,
        compiler_params=_params(("parallel",)),
        name="na",
    )(q_t, keys, v_t, table)


def _swa(q_t, keys, v_t, layer, slopes, sink):
    t = q_t.shape[1]
    return pl.pallas_call(
        functools.partial(_swa_kernel, layer),
        out_shape=jax.ShapeDtypeStruct((SWA_QW, t), BF16),
        grid=(t // SEQ,),
        in_specs=[_smem(), _smem(),
                  pl.BlockSpec((SWA_QW, SEQ), lambda b: (0, b)),
                  pl.BlockSpec((SEQ, SWA_KVW), lambda b: (b, NA_W // SWA_KVW)),
                  pl.BlockSpec((SWA_KVW, SEQ), lambda b: (0, b))],
        out_specs=pl.BlockSpec((SWA_QW, SEQ), lambda b: (0, b)),
        scratch_shapes=[pltpu.VMEM((SWA_Q_HEADS, SWA_WIN, QBT), F32)],
        compiler_params=_params(("arbitrary",)),
        name="swa",
    )(slopes, sink, q_t, keys, v_t)


def _dil(qkv, qkv4, qkv16, slopes):
    t = qkv.shape[0]
    pairs = DIL_W // LANES
    seq_block = lambda first: pl.BlockSpec((SEQ, LANES), lambda b, p: (b, first + p))

    def phased(stride, first):
        return pl.BlockSpec((1, stride, SEQ // stride, LANES), lambda b, p: (b, 0, 0, first + p))

    operands, specs = [], []
    for arr, spec in ((qkv, seq_block), (qkv4, lambda f: phased(DIL_STRIDES[1], f)),
                      (qkv16, lambda f: phased(DIL_STRIDES[2], f))):
        for part in range(3):
            operands.append(arr)
            specs.append(spec(part * pairs))
    return pl.pallas_call(
        _dil_kernel,
        out_shape=jax.ShapeDtypeStruct((t, DIL_W), BF16),
        grid=(t // SEQ, pairs),
        in_specs=[_smem()] + specs,
        out_specs=seq_block(0),
        scratch_shapes=[pltpu.VMEM((pairs, 2, 3, 2 * DIL_QB, DIL_WIN), F32),
                        pltpu.VMEM((pairs, 2 * QB, QB), F32),
                        pltpu.VMEM((3, SEQ, LANES), F32),
                        pltpu.VMEM((3, SEQ, LANES), F32),
                        pltpu.VMEM((3, SEQ, LANES), F32)],
        compiler_params=_params(("arbitrary", "arbitrary")),
        name="dil",
    )(slopes, *operands)


def _out_ffn(ya_t, yb_t, yc, x, layer, gain_ab, gain_c, w_out, gain_ffn, w_gu, w_down):
    t = x.shape[0]
    feat = lambda n: pl.BlockSpec((n, ROW_TILE), lambda i: (0, i))
    tok = lambda n: pl.BlockSpec((ROW_TILE, n), lambda i: (i, 0))
    return pl.pallas_call(
        _out_ffn_kernel,
        out_shape=jax.ShapeDtypeStruct((t, D_MODEL), F32),
        grid=(t // ROW_TILE,),
        in_specs=[feat(NA_W), feat(SWA_QW), tok(DIL_W), tok(D_MODEL),
                  _resident((NA_W + SWA_QW, 1), layer), _resident((1, DIL_W), layer),
                  _resident((MIX_W, D_MODEL), layer), _resident((1, D_MODEL), layer),
                  _resident((D_MODEL, 2 * FFN_HIDDEN), layer), _resident((FFN_HIDDEN, D_MODEL), layer)],
        out_specs=tok(D_MODEL),
        scratch_shapes=[pltpu.VMEM((ROW_TILE, D_MODEL), BF16)],
        compiler_params=_params(("parallel",)),
        name="out_ffn",
    )(ya_t, yb_t, yc, x, gain_ab, gain_c, w_out, gain_ffn, w_gu, w_down)


def _split_w_in(w):
    cuts = np.cumsum([NA_W, NA_W, NA_W, SWA_QW, SWA_KVW, SWA_KVW, DIL_W, DIL_W])
    qa, ka, va, qb, kb, vb, qc, kc, vc = jnp.split(w, [int(c) for c in cuts], axis=2)
    w_t = jnp.concatenate([qa, qb, va, vb], axis=2).astype(BF16).transpose(0, 2, 1)
    w_r = jnp.concatenate([ka, kb, qc, kc, vc], axis=2).astype(BF16)
    return w_t, w_r


def kernel(x, attn_norm, w_in, qk_gain, rpb, sink, out_gain, w_out, ffn_norm, w_gu, w_down):
    batch, seq, d = x.shape
    assert (seq, d) == (SEQ, D_MODEL)
    depth = w_in.shape[0]
    n_slopes = SWA_Q_HEADS + DIL_HEADS
    slopes = 2.0 ** (-8.0 * (jnp.arange(n_slopes, dtype=F32) + 1.0) / n_slopes)
    xf = x.reshape(batch * seq, d)

    g = qk_gain.astype(F32)
    tile = lambda v, n: jnp.tile(v, (1, n))
    gain_q = jnp.concatenate([tile(g[:, 0, 0], NA_HEADS), tile(g[:, 1, 0], SWA_Q_HEADS)], axis=1)[:, :, None]
    gain_r = jnp.concatenate([tile(g[:, 0, 1], NA_HEADS), tile(g[:, 1, 1], SWA_KV_HEADS),
                              tile(g[:, 2, 0], DIL_HEADS), tile(g[:, 2, 1], DIL_HEADS)], axis=1)[:, None, :]
    w_t, w_r = _split_w_in(w_in)
    table = _na_bias_table(rpb.reshape((depth * NA_HEADS,) + rpb.shape[2:]))
    table = table.reshape((depth, NA_HEADS) + table.shape[1:])
    og = out_gain.astype(F32)
    gain_ab, gain_c = og[:, :NA_W + SWA_QW, None], og[:, None, NA_W + SWA_QW:]
    w_out_b, w_gu_b, w_down_b = w_out.astype(BF16), w_gu.astype(BF16), w_down.astype(BF16)
    attn_gain, ffn_gain = attn_norm.astype(F32)[:, None, :], ffn_norm.astype(F32)[:, None, :]
    sink = sink.astype(F32)

    for l in range(depth):
        qa_t, qb_t, va_t, vb_t, keys, dil1, dil4, dil16 = _proj(xf, l, attn_gain, w_t, w_r, gain_q, gain_r)
        ya_t = _na(qa_t, keys, va_t, l, table)
        yb_t = _swa(qb_t, keys, vb_t, l, slopes, sink)
        yc = _dil(dil1, dil4, dil16, slopes)
        xf = _out_ffn(ya_t, yb_t, yc, xf, l, gain_ab, gain_c, w_out_b, ffn_gain, w_gu_b, w_down_b)
    return xf.reshape(batch, seq, d)
```

```python
import functools
import math

import numpy as np
import jax
import jax.numpy as jnp
from jax import lax
from jax.experimental import pallas as pl
from jax.experimental.pallas import tpu as pltpu

D_MODEL = 1024
SEQ = 2048
HEAD_DIM = 64
GRID_W = 64
NA_HEADS = 4
NA_ROWS = 8
NA_COLS = 16
SWA_Q_HEADS = 6
SWA_KV_HEADS = 2
SWA_GROUP = SWA_Q_HEADS // SWA_KV_HEADS
SWA_HALF_WINDOW = 128
DIL_HEADS = 6
DIL_STRIDES = (1, 4, 16)
DIL_HALF = 64
NA_W = NA_HEADS * HEAD_DIM
SWA_QW = SWA_Q_HEADS * HEAD_DIM
SWA_KVW = SWA_KV_HEADS * HEAD_DIM
DIL_W = DIL_HEADS * HEAD_DIM
IN_W = 3 * NA_W + SWA_QW + 2 * SWA_KVW + 3 * DIL_W
MIX_W = NA_W + SWA_QW + DIL_W
FFN_HIDDEN = 2816
NORM_EPS = 1e-6
NEG_INF = -1e30
QK_SCALE = HEAD_DIM ** -0.5
LOG2E = math.log2(math.e)
Q_SCALE = QK_SCALE * LOG2E

LANES = 128
VMEM_LIMIT = 56 * 1024 * 1024

QB = 128
QBT = 256
PREP_ROWS = 256
NQB = SEQ // QB
ROW_TILE = 512
PROJ_ROW_TILE = 1024
MXU_DIM = 256
FFN_H_CHUNKS = (6 * MXU_DIM, 5 * MXU_DIM)
assert sum(FFN_H_CHUNKS) == FFN_HIDDEN
DIL_LAG = 1

PROJ_T_Q = NA_W + SWA_QW
PROJ_T_V = NA_W + SWA_KVW
PROJ_R_K = NA_W + SWA_KVW
PROJ_R = PROJ_R_K + 3 * DIL_W
PROJ_R_CHUNK = 768

NA_WIN = 768
NA_VARIANTS = 3
NA_LAST_START = SEQ - NA_WIN
SWA_WIN = QBT + 2 * SWA_HALF_WINDOW

F32 = jnp.float32
BF16 = jnp.bfloat16


def _dot(a, b):
    return jnp.dot(a, b, preferred_element_type=F32)


def _dot_nt(a, b):
    return lax.dot_general(a, b, (((1,), (1,)), ((), ())), preferred_element_type=F32)


def _dot_tn(a, b):
    return lax.dot_general(a, b, (((0,), (0,)), ((), ())), preferred_element_type=F32)


def _low_half(shape):
    return lax.broadcasted_iota(jnp.int32, shape, len(shape) - 1) < HEAD_DIM


def _pair_rms(x, gain):
    lo = _low_half(x.shape)
    sq = x * x
    s_lo = jnp.sum(jnp.where(lo, sq, 0.0), axis=-1, keepdims=True)
    s_hi = jnp.sum(jnp.where(lo, 0.0, sq), axis=-1, keepdims=True)
    ms = jnp.where(lo, s_lo, s_hi) * (1.0 / HEAD_DIM)
    return x * lax.rsqrt(ms + NORM_EPS) * gain


def _softmax_block(s):
    m = jnp.max(s, axis=-1, keepdims=True)
    return m, jnp.exp2(s - m).astype(BF16)


def _proj_kernel(x_ref, g_ref, wt_ref, wr_ref, gq_ref, gr_ref,
                 qa_ref, qb_ref, va_ref, vb_ref, k_ref, dil_ref, dil4_ref, dil16_ref, regroup_scr):
    x = x_ref[...]
    ms = jnp.mean(x * x, axis=-1, keepdims=True)
    h = (x * lax.rsqrt(ms + NORM_EPS) * g_ref[...]).astype(BF16)
    tm = h.shape[0]

    qt = _dot_nt(wt_ref[0:PROJ_T_Q, :], h)
    q3 = qt.reshape(PROJ_T_Q // HEAD_DIM, HEAD_DIM, tm)
    q3 = q3 * lax.rsqrt(jnp.mean(q3 * q3, axis=1, keepdims=True) + NORM_EPS)
    qt = q3.reshape(PROJ_T_Q, tm) * gq_ref[...] * Q_SCALE
    qa_ref[...] = qt[0:NA_W].astype(BF16)
    qb_ref[...] = qt[NA_W:PROJ_T_Q].astype(BF16)
    vt = _dot_nt(wt_ref[PROJ_T_Q:PROJ_T_Q + PROJ_T_V, :], h)
    va_ref[...] = vt[0:NA_W].astype(BF16)
    vb_ref[...] = vt[NA_W:PROJ_T_V].astype(BF16)

    n_norm = (PROJ_R_K + 2 * DIL_W) // LANES
    n_key = PROJ_R_K // LANES
    n_dil_q = DIL_W // LANES
    for c in range(PROJ_R // PROJ_R_CHUNK):
        r = _dot(h, wr_ref[:, c * PROJ_R_CHUNK:(c + 1) * PROJ_R_CHUNK])
        for t in range(PROJ_R_CHUNK // LANES):
            blk = c * (PROJ_R_CHUNK // LANES) + t
            tile = r[:, t * LANES:(t + 1) * LANES]
            if blk < n_norm:
                tile = _pair_rms(tile, gr_ref[:, blk * LANES:(blk + 1) * LANES])
            if n_key <= blk < n_key + n_dil_q:
                tile = tile * Q_SCALE
            if blk < n_key:
                k_ref[:, blk * LANES:(blk + 1) * LANES] = tile.astype(BF16)
                continue
            d = blk - n_key
            cols = slice(d * LANES, (d + 1) * LANES)
            dil_ref[:, cols] = tile.astype(BF16)
            fine, coarse = DIL_STRIDES[1], DIL_STRIDES[2] // DIL_STRIDES[1]
            per_fine = tm // fine
            regroup_scr[d, 0] = tile
            for ph in range(fine):
                part = regroup_scr[d, 0, pl.ds(ph, per_fine, stride=fine), :]
                dil4_ref[0, ph, :, cols] = part.astype(BF16)
                regroup_scr[d, 1, ph * per_fine:(ph + 1) * per_fine, :] = part
            for ph in range(fine):
                for sub_ph in range(coarse):
                    part = regroup_scr[d, 1, pl.ds(ph * per_fine + sub_ph, per_fine // coarse, stride=coarse), :]
                    dil16_ref[0, ph + fine * sub_ph, :, cols] = part.astype(BF16)


def _group_rms_t(yt, gain_col):
    yf = yt.astype(F32)
    ms = jnp.mean(yf * yf, axis=0, keepdims=True)
    return (yf * lax.rsqrt(ms + NORM_EPS) * gain_col).astype(BF16)


def _out_ffn_kernel(ya_ref, yb_ref, yc_ref, x_ref, gab_ref, gc_ref, wo_ref, gf_ref, wgu_ref, wd_ref,
                    o_ref, h_scr):
    ya = _group_rms_t(ya_ref[...], gab_ref[0:NA_W, :])
    yb = _group_rms_t(yb_ref[...], gab_ref[NA_W:NA_W + SWA_QW, :])
    yc = yc_ref[...].astype(F32)
    ms = jnp.mean(yc * yc, axis=-1, keepdims=True)
    yc = (yc * lax.rsqrt(ms + NORM_EPS) * gc_ref[...]).astype(BF16)
    acc = _dot_tn(ya, wo_ref[0:NA_W, :])
    acc = acc + _dot_tn(yb, wo_ref[NA_W:NA_W + SWA_QW, :])
    acc = acc + _dot(yc, wo_ref[NA_W + SWA_QW:MIX_W, :])
    x = x_ref[...] + acc

    ms = jnp.mean(x * x, axis=-1, keepdims=True)
    h_scr[...] = (x * lax.rsqrt(ms + NORM_EPS) * gf_ref[...]).astype(BF16)
    o_ref[...] = x
    lo = 0
    for width in FFN_H_CHUNKS:
        gate = _dot(h_scr[...], wgu_ref[:, lo:lo + width])
        up = _dot(h_scr[...], wgu_ref[:, FFN_HIDDEN + lo:FFN_HIDDEN + lo + width])
        act = (gate / (1.0 + jnp.exp(-gate)) * up).astype(BF16)
        o_ref[...] += _dot(act, wd_ref[lo:lo + width, :])
        lo += width


def _pipeline3(chains, stage_a, stage_b, stage_c, lag=1):
    n = len(chains)
    a, b = {}, {}
    for i in range(-2 * lag, n):
        if 0 <= i + 2 * lag < n:
            a[i + 2 * lag] = stage_a(chains[i + 2 * lag])
        if 0 <= i + lag < n:
            b[i + lag] = stage_b(chains[i + lag], a.pop(i + lag))
        if i >= 0:
            stage_c(chains[i], b.pop(i))


def _attend_t(chains):
    def scores(c):
        qt = c["q"]()
        zeros = jnp.zeros_like(qt)
        qpad = jnp.concatenate([qt, zeros] if c["half"] == 0 else [zeros, qt], axis=0)
        kw = c["k"]()
        mid = kw.shape[0] // 2
        s = jnp.concatenate([_dot(kw[:mid], qpad), _dot(kw[mid:], qpad)], axis=0)
        return s + c["bias"]()

    def softmax(c, s):
        m = jnp.max(s, axis=0, keepdims=True)
        if c["sink"] is not None:
            m = jnp.maximum(m, c["sink"])
        return jnp.exp2(s - m).astype(BF16), m

    def values(c, pm):
        p, m = pm
        vt = c["v"]()
        own = vt[c["half"] * HEAD_DIM:(c["half"] + 1) * HEAD_DIM, :]
        ones = jnp.ones_like(own)
        o = _dot(jnp.concatenate([own, ones], axis=0), p)
        den = o[HEAD_DIM:2 * HEAD_DIM, :]
        if c["sink"] is not None:
            den = den + jnp.exp2(c["sink"] - m)
        c["store"](o[0:HEAD_DIM, :] / den)

    _pipeline3(chains, scores, softmax, values)


def _na_kernel(q_ref, k_ref, v_ref, tbl_ref, o_ref):
    def store(rows, cols):
        def put(o):
            o_ref[rows, cols] = o.astype(BF16)
        return put

    chains = []
    for j in range(SEQ // QBT):
        q0 = j * QBT
        start = min(max(q0 - QBT, 0), NA_LAST_START)
        variant = (q0 - start) // QBT
        cols = slice(q0, q0 + QBT)
        keys = slice(start, start + NA_WIN)
        for h in range(NA_HEADS):
            rows = slice(h * HEAD_DIM, (h + 1) * HEAD_DIM)
            pair = slice((h // 2) * LANES, (h // 2 + 1) * LANES)
            chains.append(dict(
                half=h % 2, sink=None,
                q=lambda rows=rows, cols=cols: q_ref[rows, cols],
                k=lambda keys=keys, pair=pair: k_ref[keys, pair],
                v=lambda keys=keys, pair=pair: v_ref[pair, keys],
                bias=lambda h=h, variant=variant: tbl_ref[h, variant],
                store=store(rows, cols)))
    _attend_t(chains)


def _na_bias_table(rpb):
    rows = SEQ // GRID_W
    n_blocks = SEQ // QBT
    rep_block = np.array([0, 1, n_blocks - 1])
    start = np.clip(rep_block * QBT - QBT, 0, NA_LAST_START)
    assert np.array_equal((rep_block * QBT - start) // QBT, np.arange(NA_VARIANTS))
    qq = np.arange(QBT)
    kk = np.arange(NA_WIN)
    qrow = (rep_block[:, None] * QBT + qq[None, :]) // GRID_W
    qcol = qq % GRID_W
    krow = (start[:, None] + kk[None, :]) // GRID_W
    kcol = kk % GRID_W
    row_start = np.clip(qrow - NA_ROWS // 2, 0, rows - NA_ROWS)
    col_start = np.clip(qcol - NA_COLS // 2, 0, GRID_W - NA_COLS)
    row_ok = (krow[:, :, None] >= row_start[:, None, :]) & (krow[:, :, None] < row_start[:, None, :] + NA_ROWS)
    col_ok = (kcol[:, None] >= col_start[None, :]) & (kcol[:, None] < col_start[None, :] + NA_COLS)
    valid = row_ok & col_ok[None]
    heads, n_row_rel, n_col_rel = rpb.shape
    period = 2 * GRID_W - 1
    lead = GRID_W - NA_COLS
    seq = jnp.pad(rpb.astype(F32)[:, :, ::-1], ((0, 0), (0, 0), (lead, period - lead - n_col_rel)))
    skew = jnp.tile(seq, (1, 1, GRID_W + 1))[..., :GRID_W * 2 * GRID_W]
    toe = skew.reshape(heads, n_row_rel, GRID_W, 2 * GRID_W)[:, :, ::-1, :GRID_W]
    key_rows, query_rows = NA_WIN // GRID_W, QBT // GRID_W
    first = (start // GRID_W)[:, None] - (rep_block * query_rows)[:, None] - np.arange(query_rows)[None, :] \
        + NA_ROWS - 1
    pad_lo = max(0, -int(first.min()))
    pad_hi = max(0, int(first.max()) + key_rows - n_row_rel)
    toe = jnp.pad(toe, ((0, 0), (pad_lo, pad_hi), (0, 0), (0, 0)))
    bias = jnp.stack([
        jnp.concatenate([toe[:, pad_lo + a:pad_lo + a + key_rows] for a in first[v]], axis=-1)
        for v in range(NA_VARIANTS)], axis=1)
    bias = bias.reshape(heads, NA_VARIANTS, NA_WIN, QBT)
    return jnp.where(valid[None], bias * LOG2E, NEG_INF)


def _swa_kernel(layer, slopes_ref, sink_ref, q_ref, k_ref, v_ref, o_ref, bias_scr):
    @pl.when(pl.program_id(0) == 0)
    def _():
        kk = lax.broadcasted_iota(jnp.int32, (SWA_WIN, QBT), 0)
        qq = lax.broadcasted_iota(jnp.int32, (SWA_WIN, QBT), 1)
        dist = jnp.abs(kk - SWA_HALF_WINDOW - qq)
        in_window = dist <= SWA_HALF_WINDOW
        dist_f = dist.astype(F32)
        for h in range(SWA_Q_HEADS):
            bias_scr[h] = jnp.where(in_window, -slopes_ref[h] * dist_f * LOG2E, NEG_INF)

    def store(rows, cols):
        def put(o):
            o_ref[rows, cols] = o.astype(BF16)
        return put

    chains = []
    for j in range(SEQ // QBT):
        q0 = j * QBT
        lo = max(q0 - SWA_HALF_WINDOW, 0)
        hi = min(q0 + QBT + SWA_HALF_WINDOW, SEQ)
        first = lo - (q0 - SWA_HALF_WINDOW)
        cols = slice(q0, q0 + QBT)
        keys = slice(lo, hi)
        win = slice(first, first + hi - lo)
        for h in range(SWA_Q_HEADS):
            rows = slice(h * HEAD_DIM, (h + 1) * HEAD_DIM)
            chains.append(dict(
                half=h // SWA_GROUP, sink=sink_ref[layer, h] * LOG2E,
                q=lambda rows=rows, cols=cols: q_ref[rows, cols],
                k=lambda keys=keys: k_ref[keys, :],
                v=lambda keys=keys: v_ref[:, keys],
                bias=lambda h=h, win=win: bias_scr[h, win, :],
                store=store(rows, cols)))
    _attend_t(chains)


DIL_QB = 128
DIL_WIN = DIL_QB + 2 * DIL_HALF


def _dil_kernel(slopes_ref, q1_ref, k1_ref, v1_ref, q4_ref, k4_ref, v4_ref, q16_ref, k16_ref, v16_ref,
                o_ref, bias_scr, bias16_scr, o_scr, lse_scr, out_scr):
    pair = pl.program_id(1)

    @pl.when(pl.program_id(0) == 0)
    def _():
        slope = [slopes_ref[SWA_Q_HEADS + 2 * pair + h] for h in range(2)]
        qq = lax.broadcasted_iota(jnp.int32, (DIL_QB, DIL_WIN), 0)
        kk = lax.broadcasted_iota(jnp.int32, (DIL_QB, DIL_WIN), 1)
        for shift in range(3):
            dist = jnp.abs(kk - shift * DIL_HALF - qq)
            ok = dist <= DIL_HALF
            dist_f = dist.astype(F32)
            for h in range(2):
                for b, stride in enumerate(DIL_STRIDES[:2]):
                    bias_scr[pair, b, shift, h * DIL_QB:(h + 1) * DIL_QB, :] = jnp.where(
                        ok, -slope[h] * (dist_f * float(stride)) * LOG2E, NEG_INF)
        qq = lax.broadcasted_iota(jnp.int32, (QB, QB), 0)
        kk = lax.broadcasted_iota(jnp.int32, (QB, QB), 1)
        dist = jnp.abs(kk - qq)
        for h in range(2):
            bias16_scr[pair, h * QB:(h + 1) * QB, :] = jnp.where(
                dist <= DIL_HALF, -slope[h] * (dist.astype(F32) * float(DIL_STRIDES[2])) * LOG2E, NEG_INF)

    chains = []
    for b, stride in enumerate(DIL_STRIDES):
        sub = SEQ // stride
        qb = QB if sub == QB else DIL_QB
        for start in range(0, SEQ, qb):
            phase, local_q = divmod(start, sub)
            if sub == QB:
                local_k, n_keys, shift = 0, QB, None
            else:
                local_k = min(max(local_q - DIL_HALF, 0), sub - DIL_WIN)
                n_keys, shift = DIL_WIN, (local_q - local_k) // DIL_HALF
            qrows, krows = slice(local_q, local_q + qb), slice(local_k, local_k + n_keys)
            fine = DIL_STRIDES[1]
            if stride == 1:
                refs = (q1_ref, k1_ref, v1_ref)
                load = lambda ref, rows: ref[rows, :]
                dst = qrows
            elif stride == fine:
                refs = (q4_ref, k4_ref, v4_ref)
                load = lambda ref, rows, phase=phase: ref[0, phase, rows, :]
                dst = slice(start, start + qb)
            else:
                refs = (q16_ref, k16_ref, v16_ref)
                load = lambda ref, rows, phase=phase: ref[0, phase, rows, :]
                coarse = stride // fine
                dst = pl.ds((phase % fine) * (SEQ // fine) + local_q * coarse + phase // fine, qb, stride=coarse)
            chains.append(dict(b=b, shift=shift, dst=dst, qb=qb, order=start + qb,
                               q=lambda refs=refs, load=load, rows=qrows: load(refs[0], rows),
                               k=lambda refs=refs, load=load, rows=krows: load(refs[1], rows),
                               v=lambda refs=refs, load=load, rows=krows: load(refs[2], rows)))
    chains.sort(key=lambda c: c["order"])

    def scores(c):
        q, kw = c["q"](), c["k"]()
        lo = _low_half(q.shape)
        zeros = jnp.zeros_like(q)
        q_both = jnp.concatenate([jnp.where(lo, q, zeros), jnp.where(lo, zeros, q)], axis=0)
        bias = bias16_scr[pair] if c["shift"] is None else bias_scr[pair, c["b"], c["shift"]]
        return _dot_nt(q_both, kw) + bias

    def softmax(c, s):
        return _softmax_block(s)

    def values(c, mp):
        m, p = mp
        vw = c["v"]()
        res = _dot(p, jnp.concatenate([vw, jnp.ones_like(vw)], axis=1))
        qb = c["qb"]
        lo = _low_half((qb, LANES))
        den = jnp.where(lo, res[0:qb, LANES:2 * LANES], res[qb:2 * qb, LANES:2 * LANES])
        o_scr[c["b"], c["dst"], :] = jnp.where(lo, res[0:qb, 0:LANES], res[qb:2 * qb, 0:LANES]) / den
        lse_scr[c["b"], c["dst"], :] = jnp.where(lo, m[0:qb], m[qb:2 * qb]) + jnp.log2(den)

    _pipeline3(chains, scores, softmax, values, lag=DIL_LAG)

    fine = DIL_STRIDES[1]
    per_phase = SEQ // fine

    def merge(i, carry):
        start = pl.multiple_of(i * PREP_ROWS, PREP_ROWS)
        rows = pl.ds(start, PREP_ROWS)
        natural = pl.ds((start % per_phase) * fine + start // per_phase, PREP_ROWS, stride=fine)
        lse = [lse_scr[0, natural, :], lse_scr[1, rows, :], lse_scr[2, rows, :]]
        outs = [o_scr[0, natural, :], o_scr[1, rows, :], o_scr[2, rows, :]]
        top = jnp.maximum(jnp.maximum(lse[0], lse[1]), lse[2])
        num = jnp.zeros((PREP_ROWS, LANES), F32)
        den = jnp.zeros((PREP_ROWS, LANES), F32)
        for b in range(3):
            w = jnp.exp2(lse[b] - top)
            num = num + w * outs[b]
            den = den + w
        out_scr[natural, :] = num / den
        return carry

    lax.fori_loop(0, SEQ // PREP_ROWS, merge, 0)

    def emit(i, carry):
        rows = pl.ds(pl.multiple_of(i * PREP_ROWS, PREP_ROWS), PREP_ROWS)
        o_ref[rows, :] = out_scr[rows, :].astype(BF16)
        return carry

    lax.fori_loop(0, SEQ // PREP_ROWS, emit, 0)


def _params(semantics):
    return pltpu.CompilerParams(dimension_semantics=semantics, vmem_limit_bytes=VMEM_LIMIT)


def _resident(shape, layer):
    return pl.BlockSpec((None,) + tuple(shape), lambda *_: (layer,) + (0,) * len(shape),
                        pipeline_mode=pl.Buffered(1))


def _smem():
    return pl.BlockSpec(memory_space=pltpu.SMEM)


def _proj(x, layer, gain, w_t, w_r, gain_q, gain_r):
    t = x.shape[0]
    rows = PROJ_ROW_TILE
    feat = lambda n: pl.BlockSpec((n, rows), lambda i: (0, i))
    tok = lambda n: pl.BlockSpec((rows, n), lambda i: (i, 0))
    tiles_per_seq = SEQ // rows

    def phased(stride):
        return pl.BlockSpec((1, stride, rows // stride, 3 * DIL_W),
                            lambda i: (i // tiles_per_seq, 0, i % tiles_per_seq, 0))

    def phased_shape(stride):
        return jax.ShapeDtypeStruct((t // SEQ, stride, SEQ // stride, 3 * DIL_W), BF16)

    return pl.pallas_call(
        _proj_kernel,
        out_shape=[jax.ShapeDtypeStruct((NA_W, t), BF16), jax.ShapeDtypeStruct((SWA_QW, t), BF16),
                   jax.ShapeDtypeStruct((NA_W, t), BF16), jax.ShapeDtypeStruct((SWA_KVW, t), BF16),
                   jax.ShapeDtypeStruct((t, PROJ_R_K), BF16), jax.ShapeDtypeStruct((t, 3 * DIL_W), BF16),
                   phased_shape(DIL_STRIDES[1]), phased_shape(DIL_STRIDES[2])],
        grid=(t // rows,),
        in_specs=[tok(D_MODEL), _resident((1, D_MODEL), layer),
                  _resident((PROJ_T_Q + PROJ_T_V, D_MODEL), layer), _resident((D_MODEL, PROJ_R), layer),
                  _resident((PROJ_T_Q, 1), layer), _resident((1, PROJ_R_K + 2 * DIL_W), layer)],
        out_specs=[feat(NA_W), feat(SWA_QW), feat(NA_W), feat(SWA_KVW), tok(PROJ_R_K), tok(3 * DIL_W),
                   phased(DIL_STRIDES[1]), phased(DIL_STRIDES[2])],
        scratch_shapes=[pltpu.VMEM((3 * DIL_W // LANES, 2, rows, LANES), F32)],
        compiler_params=_params(("parallel",)),
        name="proj",
    )(x, gain, w_t, w_r, gain_q, gain_r)


def _na(q_t, keys, v_t, layer, table):
    t = q_t.shape[1]
    feat = pl.BlockSpec((NA_W, SEQ), lambda b: (0, b))
    return pl.pallas_call(
        _na_kernel,
        out_shape=jax.ShapeDtypeStruct((NA_W, t), BF16),
        grid=(t // SEQ,),
        in_specs=[feat, pl.BlockSpec((SEQ, NA_W), lambda b: (b, 0)), feat,
                  _resident((NA_HEADS, NA_VARIANTS, NA_WIN, QBT), layer)],
        out_specs=feat,
        compiler_params=_params(("parallel",)),
        name="na",
    )(q_t, keys, v_t, table)


def _swa(q_t, keys, v_t, layer, slopes, sink):
    t = q_t.shape[1]
    return pl.pallas_call(
        functools.partial(_swa_kernel, layer),
        out_shape=jax.ShapeDtypeStruct((SWA_QW, t), BF16),
        grid=(t // SEQ,),
        in_specs=[_smem(), _smem(),
                  pl.BlockSpec((SWA_QW, SEQ), lambda b: (0, b)),
                  pl.BlockSpec((SEQ, SWA_KVW), lambda b: (b, NA_W // SWA_KVW)),
                  pl.BlockSpec((SWA_KVW, SEQ), lambda b: (0, b))],
        out_specs=pl.BlockSpec((SWA_QW, SEQ), lambda b: (0, b)),
        scratch_shapes=[pltpu.VMEM((SWA_Q_HEADS, SWA_WIN, QBT), F32)],
        compiler_params=_params(("arbitrary",)),
        name="swa",
    )(slopes, sink, q_t, keys, v_t)


def _dil(qkv, qkv4, qkv16, slopes):
    t = qkv.shape[0]
    pairs = DIL_W // LANES
    seq_block = lambda first: pl.BlockSpec((SEQ, LANES), lambda b, p: (b, first + p))

    def phased(stride, first):
        return pl.BlockSpec((1, stride, SEQ // stride, LANES), lambda b, p: (b, 0, 0, first + p))

    operands, specs = [], []
    for arr, spec in ((qkv, seq_block), (qkv4, lambda f: phased(DIL_STRIDES[1], f)),
                      (qkv16, lambda f: phased(DIL_STRIDES[2], f))):
        for part in range(3):
            operands.append(arr)
            specs.append(spec(part * pairs))
    return pl.pallas_call(
        _dil_kernel,
        out_shape=jax.ShapeDtypeStruct((t, DIL_W), BF16),
        grid=(t // SEQ, pairs),
        in_specs=[_smem()] + specs,
        out_specs=seq_block(0),
        scratch_shapes=[pltpu.VMEM((pairs, 2, 3, 2 * DIL_QB, DIL_WIN), F32),
                        pltpu.VMEM((pairs, 2 * QB, QB), F32),
                        pltpu.VMEM((3, SEQ, LANES), F32),
                        pltpu.VMEM((3, SEQ, LANES), F32),
                        pltpu.VMEM((SEQ, LANES), F32)],
        compiler_params=_params(("arbitrary", "arbitrary")),
        name="dil",
    )(slopes, *operands)


def _out_ffn(ya_t, yb_t, yc, x, layer, gain_ab, gain_c, w_out, gain_ffn, w_gu, w_down):
    t = x.shape[0]
    feat = lambda n: pl.BlockSpec((n, ROW_TILE), lambda i: (0, i))
    tok = lambda n: pl.BlockSpec((ROW_TILE, n), lambda i: (i, 0))
    return pl.pallas_call(
        _out_ffn_kernel,
        out_shape=jax.ShapeDtypeStruct((t, D_MODEL), F32),
        grid=(t // ROW_TILE,),
        in_specs=[feat(NA_W), feat(SWA_QW), tok(DIL_W), tok(D_MODEL),
                  _resident((NA_W + SWA_QW, 1), layer), _resident((1, DIL_W), layer),
                  _resident((MIX_W, D_MODEL), layer), _resident((1, D_MODEL), layer),
                  _resident((D_MODEL, 2 * FFN_HIDDEN), layer), _resident((FFN_HIDDEN, D_MODEL), layer)],
        out_specs=tok(D_MODEL),
        scratch_shapes=[pltpu.VMEM((ROW_TILE, D_MODEL), BF16)],
        compiler_params=_params(("parallel",)),
        name="out_ffn",
    )(ya_t, yb_t, yc, x, gain_ab, gain_c, w_out, gain_ffn, w_gu, w_down)


def _split_w_in(w):
    cuts = np.cumsum([NA_W, NA_W, NA_W, SWA_QW, SWA_KVW, SWA_KVW, DIL_W, DIL_W])
    qa, ka, va, qb, kb, vb, qc, kc, vc = jnp.split(w, [int(c) for c in cuts], axis=2)
    w_t = jnp.concatenate([qa, qb, va, vb], axis=2).astype(BF16).transpose(0, 2, 1)
    w_r = jnp.concatenate([ka, kb, qc, kc, vc], axis=2).astype(BF16)
    return w_t, w_r


def kernel(x, attn_norm, w_in, qk_gain, rpb, sink, out_gain, w_out, ffn_norm, w_gu, w_down):
    batch, seq, d = x.shape
    assert (seq, d) == (SEQ, D_MODEL)
    depth = w_in.shape[0]
    n_slopes = SWA_Q_HEADS + DIL_HEADS
    slopes = 2.0 ** (-8.0 * (jnp.arange(n_slopes, dtype=F32) + 1.0) / n_slopes)
    xf = x.reshape(batch * seq, d)

    g = qk_gain.astype(F32)
    tile = lambda v, n: jnp.tile(v, (1, n))
    gain_q = jnp.concatenate([tile(g[:, 0, 0], NA_HEADS), tile(g[:, 1, 0], SWA_Q_HEADS)], axis=1)[:, :, None]
    gain_r = jnp.concatenate([tile(g[:, 0, 1], NA_HEADS), tile(g[:, 1, 1], SWA_KV_HEADS),
                              tile(g[:, 2, 0], DIL_HEADS), tile(g[:, 2, 1], DIL_HEADS)], axis=1)[:, None, :]
    w_t, w_r = _split_w_in(w_in)
    table = _na_bias_table(rpb.reshape((depth * NA_HEADS,) + rpb.shape[2:]))
    table = table.reshape((depth, NA_HEADS) + table.shape[1:])
    og = out_gain.astype(F32)
    gain_ab, gain_c = og[:, :NA_W + SWA_QW, None], og[:, None, NA_W + SWA_QW:]
    w_out_b, w_gu_b, w_down_b = w_out.astype(BF16), w_gu.astype(BF16), w_down.astype(BF16)
    attn_gain, ffn_gain = attn_norm.astype(F32)[:, None, :], ffn_norm.astype(F32)[:, None, :]
    sink = sink.astype(F32)

    for l in range(depth):
        qa_t, qb_t, va_t, vb_t, keys, dil1, dil4, dil16 = _proj(xf, l, attn_gain, w_t, w_r, gain_q, gain_r)
        ya_t = _na(qa_t, keys, va_t, l, table)
        yb_t = _swa(qb_t, keys, vb_t, l, slopes, sink)
        yc = _dil(dil1, dil4, dil16, slopes)
        xf = _out_ffn(ya_t, yb_t, yc, xf, l, gain_ab, gain_c, w_out_b, ffn_gain, w_gu_b, w_down_b)
    return xf.reshape(batch, seq, d)
```

```python
import functools
import math

import numpy as np
import jax
import jax.numpy as jnp
from jax import lax
from jax.experimental import pallas as pl
from jax.experimental.pallas import tpu as pltpu

D_MODEL = 1024
SEQ = 2048
HEAD_DIM = 64
GRID_W = 64
NA_HEADS = 4
NA_ROWS = 8
NA_COLS = 16
SWA_Q_HEADS = 6
SWA_KV_HEADS = 2
SWA_GROUP = SWA_Q_HEADS // SWA_KV_HEADS
SWA_HALF_WINDOW = 128
DIL_HEADS = 6
DIL_STRIDES = (1, 4, 16)
DIL_HALF = 64
NA_W = NA_HEADS * HEAD_DIM
SWA_QW = SWA_Q_HEADS * HEAD_DIM
SWA_KVW = SWA_KV_HEADS * HEAD_DIM
DIL_W = DIL_HEADS * HEAD_DIM
IN_W = 3 * NA_W + SWA_QW + 2 * SWA_KVW + 3 * DIL_W
MIX_W = NA_W + SWA_QW + DIL_W
FFN_HIDDEN = 2816
NORM_EPS = 1e-6
NEG_INF = -1e30
QK_SCALE = HEAD_DIM ** -0.5
LOG2E = math.log2(math.e)
Q_SCALE = QK_SCALE * LOG2E

LANES = 128
VMEM_LIMIT = 56 * 1024 * 1024

QB = 128
QBT = 256
PREP_ROWS = 256
NQB = SEQ // QB
ROW_TILE = 512
PROJ_ROW_TILE = 1024
MXU_DIM = 256
FFN_H_CHUNKS = (6 * MXU_DIM, 5 * MXU_DIM)
assert sum(FFN_H_CHUNKS) == FFN_HIDDEN
DIL_LAG = 1

PROJ_T_Q = NA_W + SWA_QW
PROJ_T_V = NA_W + SWA_KVW
PROJ_R_K = NA_W + SWA_KVW
PROJ_R = PROJ_R_K + 3 * DIL_W
PROJ_R_CHUNK = 768

NA_WIN = 768
NA_VARIANTS = 3
NA_LAST_START = SEQ - NA_WIN
SWA_WIN = QBT + 2 * SWA_HALF_WINDOW

F32 = jnp.float32
BF16 = jnp.bfloat16


def _dot(a, b):
    return jnp.dot(a, b, preferred_element_type=F32)


def _dot_nt(a, b):
    return lax.dot_general(a, b, (((1,), (1,)), ((), ())), preferred_element_type=F32)


def _dot_tn(a, b):
    return lax.dot_general(a, b, (((0,), (0,)), ((), ())), preferred_element_type=F32)


def _low_half(shape):
    return lax.broadcasted_iota(jnp.int32, shape, len(shape) - 1) < HEAD_DIM


def _pair_rms(x, gain):
    lo = _low_half(x.shape)
    sq = x * x
    s_lo = jnp.sum(jnp.where(lo, sq, 0.0), axis=-1, keepdims=True)
    s_hi = jnp.sum(jnp.where(lo, 0.0, sq), axis=-1, keepdims=True)
    ms = jnp.where(lo, s_lo, s_hi) * (1.0 / HEAD_DIM)
    return x * lax.rsqrt(ms + NORM_EPS) * gain


def _softmax_block(s):
    m = jnp.max(s, axis=-1, keepdims=True)
    return m, jnp.exp2(s - m).astype(BF16)


def _proj_kernel(x_ref, g_ref, wt_ref, wr_ref, gq_ref, gr_ref,
                 qa_ref, qb_ref, va_ref, vb_ref, k_ref, dil_ref, dil4_ref, dil16_ref, regroup_scr):
    x = x_ref[...]
    ms = jnp.mean(x * x, axis=-1, keepdims=True)
    h = (x * lax.rsqrt(ms + NORM_EPS) * g_ref[...]).astype(BF16)
    tm = h.shape[0]

    def finish_q(qt):
        q3 = qt.reshape(PROJ_T_Q // HEAD_DIM, HEAD_DIM, tm)
        q3 = q3 * lax.rsqrt(jnp.mean(q3 * q3, axis=1, keepdims=True) + NORM_EPS)
        qt = q3.reshape(PROJ_T_Q, tm) * gq_ref[...] * Q_SCALE
        qa_ref[...] = qt[0:NA_W].astype(BF16)
        qb_ref[...] = qt[NA_W:PROJ_T_Q].astype(BF16)

    def finish_v(vt):
        va_ref[...] = vt[0:NA_W].astype(BF16)
        vb_ref[...] = vt[NA_W:PROJ_T_V].astype(BF16)

    n_norm = (PROJ_R_K + 2 * DIL_W) // LANES
    n_key = PROJ_R_K // LANES
    n_dil_q = DIL_W // LANES

    def finish_r(c, r):
        for t in range(PROJ_R_CHUNK // LANES):
            blk = c * (PROJ_R_CHUNK // LANES) + t
            tile = r[:, t * LANES:(t + 1) * LANES]
            if blk < n_norm:
                tile = _pair_rms(tile, gr_ref[:, blk * LANES:(blk + 1) * LANES])
            if n_key <= blk < n_key + n_dil_q:
                tile = tile * Q_SCALE
            if blk < n_key:
                k_ref[:, blk * LANES:(blk + 1) * LANES] = tile.astype(BF16)
                continue
            d = blk - n_key
            cols = slice(d * LANES, (d + 1) * LANES)
            dil_ref[:, cols] = tile.astype(BF16)
            fine, coarse = DIL_STRIDES[1], DIL_STRIDES[2] // DIL_STRIDES[1]
            per_fine = tm // fine
            regroup_scr[d, 0] = tile
            for ph in range(fine):
                part = regroup_scr[d, 0, pl.ds(ph, per_fine, stride=fine), :]
                dil4_ref[0, ph, :, cols] = part.astype(BF16)
                regroup_scr[d, 1, ph * per_fine:(ph + 1) * per_fine, :] = part
            for ph in range(fine):
                for sub_ph in range(coarse):
                    part = regroup_scr[d, 1, pl.ds(ph * per_fine + sub_ph, per_fine // coarse, stride=coarse), :]
                    dil16_ref[0, ph + fine * sub_ph, :, cols] = part.astype(BF16)

    units = [(lambda c=c: _dot(h, wr_ref[:, c * PROJ_R_CHUNK:(c + 1) * PROJ_R_CHUNK]),
              functools.partial(finish_r, c)) for c in range(PROJ_R // PROJ_R_CHUNK)]
    units.append((lambda: _dot_nt(wt_ref[0:PROJ_T_Q, :], h), finish_q))
    units.append((lambda: _dot_nt(wt_ref[PROJ_T_Q:PROJ_T_Q + PROJ_T_V, :], h), finish_v))
    pending = None
    for matmul, finish in units:
        result = matmul()
        if pending is not None:
            pending[0](pending[1])
        pending = (finish, result)
    pending[0](pending[1])


def _group_rms_t(yt, gain_col):
    yf = yt.astype(F32)
    ms = jnp.mean(yf * yf, axis=0, keepdims=True)
    return (yf * lax.rsqrt(ms + NORM_EPS) * gain_col).astype(BF16)


def _out_ffn_kernel(ya_ref, yb_ref, yc_ref, x_ref, gab_ref, gc_ref, wo_ref, gf_ref, wgu_ref, wd_ref,
                    o_ref, h_scr):
    ya = _group_rms_t(ya_ref[...], gab_ref[0:NA_W, :])
    yb = _group_rms_t(yb_ref[...], gab_ref[NA_W:NA_W + SWA_QW, :])
    yc = yc_ref[...].astype(F32)
    ms = jnp.mean(yc * yc, axis=-1, keepdims=True)
    yc = (yc * lax.rsqrt(ms + NORM_EPS) * gc_ref[...]).astype(BF16)
    acc = _dot_tn(ya, wo_ref[0:NA_W, :])
    acc = acc + _dot_tn(yb, wo_ref[NA_W:NA_W + SWA_QW, :])
    acc = acc + _dot(yc, wo_ref[NA_W + SWA_QW:MIX_W, :])
    x = x_ref[...] + acc

    ms = jnp.mean(x * x, axis=-1, keepdims=True)
    h_scr[...] = (x * lax.rsqrt(ms + NORM_EPS) * gf_ref[...]).astype(BF16)
    o_ref[...] = x
    lo = 0
    for width in FFN_H_CHUNKS:
        gate = _dot(h_scr[...], wgu_ref[:, lo:lo + width])
        up = _dot(h_scr[...], wgu_ref[:, FFN_HIDDEN + lo:FFN_HIDDEN + lo + width])
        act = (gate / (1.0 + jnp.exp(-gate)) * up).astype(BF16)
        o_ref[...] += _dot(act, wd_ref[lo:lo + width, :])
        lo += width


def _pipeline3(chains, stage_a, stage_b, stage_c, lag=1):
    n = len(chains)
    a, b = {}, {}
    for i in range(-2 * lag, n):
        if 0 <= i + 2 * lag < n:
            a[i + 2 * lag] = stage_a(chains[i + 2 * lag])
        if 0 <= i + lag < n:
            b[i + lag] = stage_b(chains[i + lag], a.pop(i + lag))
        if i >= 0:
            stage_c(chains[i], b.pop(i))


def _attend_t(chains):
    def scores(c):
        qt = c["q"]()
        zeros = jnp.zeros_like(qt)
        qpad = jnp.concatenate([qt, zeros] if c["half"] == 0 else [zeros, qt], axis=0)
        kw = c["k"]()
        mid = kw.shape[0] // 2
        s = jnp.concatenate([_dot(kw[:mid], qpad), _dot(kw[mid:], qpad)], axis=0)
        return s + c["bias"]()

    def softmax(c, s):
        m = jnp.max(s, axis=0, keepdims=True)
        if c["sink"] is not None:
            m = jnp.maximum(m, c["sink"])
        return jnp.exp2(s - m).astype(BF16), m

    def values(c, pm):
        p, m = pm
        vt = c["v"]()
        own = vt[c["half"] * HEAD_DIM:(c["half"] + 1) * HEAD_DIM, :]
        ones = jnp.ones_like(own)
        o = _dot(jnp.concatenate([own, ones], axis=0), p)
        den = o[HEAD_DIM:2 * HEAD_DIM, :]
        if c["sink"] is not None:
            den = den + jnp.exp2(c["sink"] - m)
        c["store"](o[0:HEAD_DIM, :] / den)

    _pipeline3(chains, scores, softmax, values)


NA_KEY_ROWS = NA_WIN // GRID_W
NA_Q_ROWS = QBT // GRID_W
NA_ROW_REL = 2 * NA_ROWS - 1


def _na_geometry():
    n_blocks = SEQ // QBT
    out = []
    for variant, block in enumerate((0, 1, n_blocks - 1)):
        start = min(max(block * QBT - QBT, 0), NA_LAST_START)
        assert (block * QBT - start) // QBT == variant
        out.append((start // GRID_W, block * NA_Q_ROWS))
    return out


def _na_kernel(q_ref, k_ref, v_ref, pair_ref, o_ref, tbl_scr):
    @pl.when(pl.program_id(0) == 0)
    def _():
        grid_rows = SEQ // GRID_W
        lo = _low_half((GRID_W, LANES))
        masked = jnp.full((GRID_W, LANES), NEG_INF, F32)
        for variant, (key_row0, query_row0) in enumerate(_na_geometry()):
            for kr in range(NA_KEY_ROWS):
                for pair in range(NA_Q_ROWS // 2):
                    krow = key_row0 + kr
                    rel, ok = [], []
                    for qrow in (query_row0 + 2 * pair, query_row0 + 2 * pair + 1):
                        first = min(max(qrow - NA_ROWS // 2, 0), grid_rows - NA_ROWS)
                        ok.append(first <= krow < first + NA_ROWS)
                        rel.append(krow - qrow + NA_ROWS - 1)
                    for h in range(NA_HEADS):
                        if not (ok[0] or ok[1]):
                            tile = masked
                        else:
                            tile = pair_ref[h, rel[0]]
                            if not ok[1]:
                                tile = jnp.where(lo, tile, masked)
                            elif not ok[0]:
                                tile = jnp.where(lo, masked, tile)
                        tbl_scr[h, variant, kr * GRID_W:(kr + 1) * GRID_W, pair * LANES:(pair + 1) * LANES] = tile

    def store(rows, cols):
        def put(o):
            o_ref[rows, cols] = o.astype(BF16)
        return put

    chains = []
    for j in range(SEQ // QBT):
        q0 = j * QBT
        start = min(max(q0 - QBT, 0), NA_LAST_START)
        variant = (q0 - start) // QBT
        cols = slice(q0, q0 + QBT)
        keys = slice(start, start + NA_WIN)
        for h in range(NA_HEADS):
            rows = slice(h * HEAD_DIM, (h + 1) * HEAD_DIM)
            pair = slice((h // 2) * LANES, (h // 2 + 1) * LANES)
            chains.append(dict(
                half=h % 2, sink=None,
                q=lambda rows=rows, cols=cols: q_ref[rows, cols],
                k=lambda keys=keys, pair=pair: k_ref[keys, pair],
                v=lambda keys=keys, pair=pair: v_ref[pair, keys],
                bias=lambda h=h, variant=variant: tbl_scr[h, variant],
                store=store(rows, cols)))
    _attend_t(chains)


def _na_pair_table(rpb):
    cols = np.arange(GRID_W)
    col_start = np.clip(cols - NA_COLS // 2, 0, GRID_W - NA_COLS)
    col_ok = (cols[:, None] >= col_start[None, :]) & (cols[:, None] < col_start[None, :] + NA_COLS)
    heads, n_row_rel, n_col_rel = rpb.shape
    assert n_row_rel == NA_ROW_REL
    period = 2 * GRID_W - 1
    lead = GRID_W - NA_COLS
    seq = jnp.pad(rpb.astype(F32)[:, :, ::-1], ((0, 0), (0, 0), (lead, period - lead - n_col_rel)))
    skew = jnp.tile(seq, (1, 1, GRID_W + 1))[..., :GRID_W * 2 * GRID_W]
    toe = skew.reshape(heads, n_row_rel, GRID_W, 2 * GRID_W)[:, :, ::-1, :GRID_W]
    toe = jnp.where(col_ok[None, None], toe * LOG2E, NEG_INF)
    toe = jnp.pad(toe, ((0, 0), (1, 1), (0, 0), (0, 0)), constant_values=NEG_INF)
    return jnp.concatenate([toe[:, 1:], toe[:, :-1]], axis=-1)


def _swa_kernel(layer, slopes_ref, sink_ref, q_ref, k_ref, v_ref, o_ref, bias_scr):
    @pl.when(pl.program_id(0) == 0)
    def _():
        kk = lax.broadcasted_iota(jnp.int32, (SWA_WIN, QBT), 0)
        qq = lax.broadcasted_iota(jnp.int32, (SWA_WIN, QBT), 1)
        dist = jnp.abs(kk - SWA_HALF_WINDOW - qq)
        in_window = dist <= SWA_HALF_WINDOW
        dist_f = dist.astype(F32)
        for h in range(SWA_Q_HEADS):
            bias_scr[h] = jnp.where(in_window, -slopes_ref[h] * dist_f * LOG2E, NEG_INF)

    def store(rows, cols):
        def put(o):
            o_ref[rows, cols] = o.astype(BF16)
        return put

    chains = []
    for j in range(SEQ // QBT):
        q0 = j * QBT
        lo = max(q0 - SWA_HALF_WINDOW, 0)
        hi = min(q0 + QBT + SWA_HALF_WINDOW, SEQ)
        first = lo - (q0 - SWA_HALF_WINDOW)
        cols = slice(q0, q0 + QBT)
        keys = slice(lo, hi)
        win = slice(first, first + hi - lo)
        for h in range(SWA_Q_HEADS):
            rows = slice(h * HEAD_DIM, (h + 1) * HEAD_DIM)
            chains.append(dict(
                half=h // SWA_GROUP, sink=sink_ref[layer, h] * LOG2E,
                q=lambda rows=rows, cols=cols: q_ref[rows, cols],
                k=lambda keys=keys: k_ref[keys, :],
                v=lambda keys=keys: v_ref[:, keys],
                bias=lambda h=h, win=win: bias_scr[h, win, :],
                store=store(rows, cols)))
    _attend_t(chains)


DIL_QB = 128
DIL_WIN = DIL_QB + 2 * DIL_HALF


def _dil_kernel(slopes_ref, q1_ref, k1_ref, v1_ref, q4_ref, k4_ref, v4_ref, q16_ref, k16_ref, v16_ref,
                o_ref, bias_scr, bias16_scr, o_scr, lse_scr, out_scr):
    pair = pl.program_id(1)

    @pl.when(pl.program_id(0) == 0)
    def _():
        slope = [slopes_ref[SWA_Q_HEADS + 2 * pair + h] for h in range(2)]
        qq = lax.broadcasted_iota(jnp.int32, (DIL_QB, DIL_WIN), 0)
        kk = lax.broadcasted_iota(jnp.int32, (DIL_QB, DIL_WIN), 1)
        for shift in range(3):
            dist = jnp.abs(kk - shift * DIL_HALF - qq)
            ok = dist <= DIL_HALF
            dist_f = dist.astype(F32)
            for h in range(2):
                for b, stride in enumerate(DIL_STRIDES[:2]):
                    bias_scr[pair, b, shift, h * DIL_QB:(h + 1) * DIL_QB, :] = jnp.where(
                        ok, -slope[h] * (dist_f * float(stride)) * LOG2E, NEG_INF)
        qq = lax.broadcasted_iota(jnp.int32, (QB, QB), 0)
        kk = lax.broadcasted_iota(jnp.int32, (QB, QB), 1)
        dist = jnp.abs(kk - qq)
        for h in range(2):
            bias16_scr[pair, h * QB:(h + 1) * QB, :] = jnp.where(
                dist <= DIL_HALF, -slope[h] * (dist.astype(F32) * float(DIL_STRIDES[2])) * LOG2E, NEG_INF)

    chains = []
    for b, stride in enumerate(DIL_STRIDES):
        sub = SEQ // stride
        qb = QB if sub == QB else DIL_QB
        for start in range(0, SEQ, qb):
            phase, local_q = divmod(start, sub)
            if sub == QB:
                local_k, n_keys, shift = 0, QB, None
            else:
                local_k = min(max(local_q - DIL_HALF, 0), sub - DIL_WIN)
                n_keys, shift = DIL_WIN, (local_q - local_k) // DIL_HALF
            qrows, krows = slice(local_q, local_q + qb), slice(local_k, local_k + n_keys)
            fine = DIL_STRIDES[1]
            if stride == 1:
                refs = (q1_ref, k1_ref, v1_ref)
                load = lambda ref, rows: ref[rows, :]
                dst = qrows
            elif stride == fine:
                refs = (q4_ref, k4_ref, v4_ref)
                load = lambda ref, rows, phase=phase: ref[0, phase, rows, :]
                dst = slice(start, start + qb)
            else:
                refs = (q16_ref, k16_ref, v16_ref)
                load = lambda ref, rows, phase=phase: ref[0, phase, rows, :]
                coarse = stride // fine
                dst = pl.ds((phase % fine) * (SEQ // fine) + local_q * coarse + phase // fine, qb, stride=coarse)
            chains.append(dict(b=b, shift=shift, dst=dst, qb=qb, order=start + qb,
                               q=lambda refs=refs, load=load, rows=qrows: load(refs[0], rows),
                               k=lambda refs=refs, load=load, rows=krows: load(refs[1], rows),
                               v=lambda refs=refs, load=load, rows=krows: load(refs[2], rows)))
    chains.sort(key=lambda c: c["order"])

    def scores(c):
        q, kw = c["q"](), c["k"]()
        lo = _low_half(q.shape)
        zeros = jnp.zeros_like(q)
        q_both = jnp.concatenate([jnp.where(lo, q, zeros), jnp.where(lo, zeros, q)], axis=0)
        bias = bias16_scr[pair] if c["shift"] is None else bias_scr[pair, c["b"], c["shift"]]
        return _dot_nt(q_both, kw) + bias

    def softmax(c, s):
        return _softmax_block(s)

    def values(c, mp):
        m, p = mp
        vw = c["v"]()
        res = _dot(p, jnp.concatenate([vw, jnp.ones_like(vw)], axis=1))
        qb = c["qb"]
        lo = _low_half((qb, LANES))
        den = jnp.where(lo, res[0:qb, LANES:2 * LANES], res[qb:2 * qb, LANES:2 * LANES])
        o_scr[c["b"], c["dst"], :] = jnp.where(lo, res[0:qb, 0:LANES], res[qb:2 * qb, 0:LANES]) / den
        lse_scr[c["b"], c["dst"], :] = jnp.where(lo, m[0:qb], m[qb:2 * qb]) + jnp.log2(den)

    _pipeline3(chains, scores, softmax, values, lag=DIL_LAG)

    fine = DIL_STRIDES[1]
    per_phase = SEQ // fine

    def merge(i, carry):
        start = pl.multiple_of(i * PREP_ROWS, PREP_ROWS)
        rows = pl.ds(start, PREP_ROWS)
        natural = pl.ds((start % per_phase) * fine + start // per_phase, PREP_ROWS, stride=fine)
        lse = [lse_scr[0, natural, :], lse_scr[1, rows, :], lse_scr[2, rows, :]]
        outs = [o_scr[0, natural, :], o_scr[1, rows, :], o_scr[2, rows, :]]
        top = jnp.maximum(jnp.maximum(lse[0], lse[1]), lse[2])
        num = jnp.zeros((PREP_ROWS, LANES), F32)
        den = jnp.zeros((PREP_ROWS, LANES), F32)
        for b in range(3):
            w = jnp.exp2(lse[b] - top)
            num = num + w * outs[b]
            den = den + w
        out_scr[natural, :] = num / den
        return carry

    lax.fori_loop(0, SEQ // PREP_ROWS, merge, 0)

    def emit(i, carry):
        rows = pl.ds(pl.multiple_of(i * PREP_ROWS, PREP_ROWS), PREP_ROWS)
        o_ref[rows, :] = out_scr[rows, :].astype(BF16)
        return carry

    lax.fori_loop(0, SEQ // PREP_ROWS, emit, 0)


def _params(semantics):
    return pltpu.CompilerParams(dimension_semantics=semantics, vmem_limit_bytes=VMEM_LIMIT)


def _resident(shape, layer):
    return pl.BlockSpec((None,) + tuple(shape), lambda *_: (layer,) + (0,) * len(shape),
                        pipeline_mode=pl.Buffered(1))


def _smem():
    return pl.BlockSpec(memory_space=pltpu.SMEM)


def _proj(x, layer, gain, w_t, w_r, gain_q, gain_r):
    t = x.shape[0]
    rows = PROJ_ROW_TILE
    feat = lambda n: pl.BlockSpec((n, rows), lambda i: (0, i))
    tok = lambda n: pl.BlockSpec((rows, n), lambda i: (i, 0))
    tiles_per_seq = SEQ // rows

    def phased(stride):
        return pl.BlockSpec((1, stride, rows // stride, 3 * DIL_W),
                            lambda i: (i // tiles_per_seq, 0, i % tiles_per_seq, 0))

    def phased_shape(stride):
        return jax.ShapeDtypeStruct((t // SEQ, stride, SEQ // stride, 3 * DIL_W), BF16)

    return pl.pallas_call(
        _proj_kernel,
        out_shape=[jax.ShapeDtypeStruct((NA_W, t), BF16), jax.ShapeDtypeStruct((SWA_QW, t), BF16),
                   jax.ShapeDtypeStruct((NA_W, t), BF16), jax.ShapeDtypeStruct((SWA_KVW, t), BF16),
                   jax.ShapeDtypeStruct((t, PROJ_R_K), BF16), jax.ShapeDtypeStruct((t, 3 * DIL_W), BF16),
                   phased_shape(DIL_STRIDES[1]), phased_shape(DIL_STRIDES[2])],
        grid=(t // rows,),
        in_specs=[tok(D_MODEL), _resident((1, D_MODEL), layer),
                  _resident((PROJ_T_Q + PROJ_T_V, D_MODEL), layer), _resident((D_MODEL, PROJ_R), layer),
                  _resident((PROJ_T_Q, 1), layer), _resident((1, PROJ_R_K + 2 * DIL_W), layer)],
        out_specs=[feat(NA_W), feat(SWA_QW), feat(NA_W), feat(SWA_KVW), tok(PROJ_R_K), tok(3 * DIL_W),
                   phased(DIL_STRIDES[1]), phased(DIL_STRIDES[2])],
        scratch_shapes=[pltpu.VMEM((3 * DIL_W // LANES, 2, rows, LANES), F32)],
        compiler_params=_params(("parallel",)),
        name="proj",
    )(x, gain, w_t, w_r, gain_q, gain_r)


def _na(q_t, keys, v_t, layer, table):
    t = q_t.shape[1]
    feat = pl.BlockSpec((NA_W, SEQ), lambda b: (0, b))
    return pl.pallas_call(
        _na_kernel,
        out_shape=jax.ShapeDtypeStruct((NA_W, t), BF16),
        grid=(t // SEQ,),
        in_specs=[feat, pl.BlockSpec((SEQ, NA_W), lambda b: (b, 0)), feat,
                  _resident((NA_HEADS, NA_ROW_REL + 1, GRID_W, 2 * GRID_W), layer)],
        out_specs=feat,
        scratch_shapes=[pltpu.VMEM((NA_HEADS, NA_VARIANTS, NA_WIN, QBT), F32)],
        compiler_params=_params(("arbitrary",)),
        name="na",
    )(q_t, keys, v_t, table)


def _swa(q_t, keys, v_t, layer, slopes, sink):
    t = q_t.shape[1]
    return pl.pallas_call(
        functools.partial(_swa_kernel, layer),
        out_shape=jax.ShapeDtypeStruct((SWA_QW, t), BF16),
        grid=(t // SEQ,),
        in_specs=[_smem(), _smem(),
                  pl.BlockSpec((SWA_QW, SEQ), lambda b: (0, b)),
                  pl.BlockSpec((SEQ, SWA_KVW), lambda b: (b, NA_W // SWA_KVW)),
                  pl.BlockSpec((SWA_KVW, SEQ), lambda b: (0, b))],
        out_specs=pl.BlockSpec((SWA_QW, SEQ), lambda b: (0, b)),
        scratch_shapes=[pltpu.VMEM((SWA_Q_HEADS, SWA_WIN, QBT), F32)],
        compiler_params=_params(("arbitrary",)),
        name="swa",
    )(slopes, sink, q_t, keys, v_t)


def _dil(qkv, qkv4, qkv16, slopes):
    t = qkv.shape[0]
    pairs = DIL_W // LANES
    seq_block = lambda first: pl.BlockSpec((SEQ, LANES), lambda b, p: (b, first + p))

    def phased(stride, first):
        return pl.BlockSpec((1, stride, SEQ // stride, LANES), lambda b, p: (b, 0, 0, first + p))

    operands, specs = [], []
    for arr, spec in ((qkv, seq_block), (qkv4, lambda f: phased(DIL_STRIDES[1], f)),
                      (qkv16, lambda f: phased(DIL_STRIDES[2], f))):
        for part in range(3):
            operands.append(arr)
            specs.append(spec(part * pairs))
    return pl.pallas_call(
        _dil_kernel,
        out_shape=jax.ShapeDtypeStruct((t, DIL_W), BF16),
        grid=(t // SEQ, pairs),
        in_specs=[_smem()] + specs,
        out_specs=seq_block(0),
        scratch_shapes=[pltpu.VMEM((pairs, 2, 3, 2 * DIL_QB, DIL_WIN), F32),
                        pltpu.VMEM((pairs, 2 * QB, QB), F32),
                        pltpu.VMEM((3, SEQ, LANES), F32),
                        pltpu.VMEM((3, SEQ, LANES), F32),
                        pltpu.VMEM((SEQ, LANES), F32)],
        compiler_params=_params(("arbitrary", "arbitrary")),
        name="dil",
    )(slopes, *operands)


def _out_ffn(ya_t, yb_t, yc, x, layer, gain_ab, gain_c, w_out, gain_ffn, w_gu, w_down):
    t = x.shape[0]
    feat = lambda n: pl.BlockSpec((n, ROW_TILE), lambda i: (0, i))
    tok = lambda n: pl.BlockSpec((ROW_TILE, n), lambda i: (i, 0))
    return pl.pallas_call(
        _out_ffn_kernel,
        out_shape=jax.ShapeDtypeStruct((t, D_MODEL), F32),
        grid=(t // ROW_TILE,),
        in_specs=[feat(NA_W), feat(SWA_QW), tok(DIL_W), tok(D_MODEL),
                  _resident((NA_W + SWA_QW, 1), layer), _resident((1, DIL_W), layer),
                  _resident((MIX_W, D_MODEL), layer), _resident((1, D_MODEL), layer),
                  _resident((D_MODEL, 2 * FFN_HIDDEN), layer), _resident((FFN_HIDDEN, D_MODEL), layer)],
        out_specs=tok(D_MODEL),
        scratch_shapes=[pltpu.VMEM((ROW_TILE, D_MODEL), BF16)],
        compiler_params=_params(("parallel",)),
        name="out_ffn",
    )(ya_t, yb_t, yc, x, gain_ab, gain_c, w_out, gain_ffn, w_gu, w_down)


def _split_w_in(w):
    cuts = np.cumsum([NA_W, NA_W, NA_W, SWA_QW, SWA_KVW, SWA_KVW, DIL_W, DIL_W])
    qa, ka, va, qb, kb, vb, qc, kc, vc = jnp.split(w, [int(c) for c in cuts], axis=2)
    w_t = jnp.concatenate([qa, qb, va, vb], axis=2).astype(BF16).transpose(0, 2, 1)
    w_r = jnp.concatenate([ka, kb, qc, kc, vc], axis=2).astype(BF16)
    return w_t, w_r


def kernel(x, attn_norm, w_in, qk_gain, rpb, sink, out_gain, w_out, ffn_norm, w_gu, w_down):
    batch, seq, d = x.shape
    assert (seq, d) == (SEQ, D_MODEL)
    depth = w_in.shape[0]
    n_slopes = SWA_Q_HEADS + DIL_HEADS
    slopes = 2.0 ** (-8.0 * (jnp.arange(n_slopes, dtype=F32) + 1.0) / n_slopes)
    xf = x.reshape(batch * seq, d)

    g = qk_gain.astype(F32)
    tile = lambda v, n: jnp.tile(v, (1, n))
    gain_q = jnp.concatenate([tile(g[:, 0, 0], NA_HEADS), tile(g[:, 1, 0], SWA_Q_HEADS)], axis=1)[:, :, None]
    gain_r = jnp.concatenate([tile(g[:, 0, 1], NA_HEADS), tile(g[:, 1, 1], SWA_KV_HEADS),
                              tile(g[:, 2, 0], DIL_HEADS), tile(g[:, 2, 1], DIL_HEADS)], axis=1)[:, None, :]
    w_t, w_r = _split_w_in(w_in)
    table = _na_pair_table(rpb.reshape((depth * NA_HEADS,) + rpb.shape[2:]))
    table = table.reshape((depth, NA_HEADS) + table.shape[1:])
    og = out_gain.astype(F32)
    gain_ab, gain_c = og[:, :NA_W + SWA_QW, None], og[:, None, NA_W + SWA_QW:]
    w_out_b, w_gu_b, w_down_b = w_out.astype(BF16), w_gu.astype(BF16), w_down.astype(BF16)
    attn_gain, ffn_gain = attn_norm.astype(F32)[:, None, :], ffn_norm.astype(F32)[:, None, :]
    sink = sink.astype(F32)

    for l in range(depth):
        qa_t, qb_t, va_t, vb_t, keys, dil1, dil4, dil16 = _proj(xf, l, attn_gain, w_t, w_r, gain_q, gain_r)
        ya_t = _na(qa_t, keys, va_t, l, table)
        yb_t = _swa(qb_t, keys, vb_t, l, slopes, sink)
        yc = _dil(dil1, dil4, dil16, slopes)
        xf = _out_ffn(ya_t, yb_t, yc, xf, l, gain_ab, gain_c, w_out_b, ffn_gain, w_gu_b, w_down_b)
    return xf.reshape(batch, seq, d)
```

```python
import functools
import math

import numpy as np
import jax
import jax.numpy as jnp
from jax import lax
from jax.experimental import pallas as pl
from jax.experimental.pallas import tpu as pltpu

D_MODEL = 1024
SEQ = 2048
HEAD_DIM = 64
GRID_W = 64
NA_HEADS = 4
NA_ROWS = 8
NA_COLS = 16
SWA_Q_HEADS = 6
SWA_KV_HEADS = 2
SWA_GROUP = SWA_Q_HEADS // SWA_KV_HEADS
SWA_HALF_WINDOW = 128
DIL_HEADS = 6
DIL_STRIDES = (1, 4, 16)
DIL_HALF = 64
NA_W = NA_HEADS * HEAD_DIM
SWA_QW = SWA_Q_HEADS * HEAD_DIM
SWA_KVW = SWA_KV_HEADS * HEAD_DIM
DIL_W = DIL_HEADS * HEAD_DIM
IN_W = 3 * NA_W + SWA_QW + 2 * SWA_KVW + 3 * DIL_W
MIX_W = NA_W + SWA_QW + DIL_W
FFN_HIDDEN = 2816
NORM_EPS = 1e-6
NEG_INF = -1e30
QK_SCALE = HEAD_DIM ** -0.5
LOG2E = math.log2(math.e)
Q_SCALE = QK_SCALE * LOG2E

LANES = 128
VMEM_LIMIT = 56 * 1024 * 1024

QB = 128
QBT = 256
PREP_ROWS = 256
NQB = SEQ // QB
ROW_TILE = 1024
PROJ_ROW_TILE = 1024
MXU_DIM = 256
FFN_H_CHUNKS = (3 * MXU_DIM,) * 3 + (2 * MXU_DIM,)
assert sum(FFN_H_CHUNKS) == FFN_HIDDEN
NA_LAG = 2
DIL_LAG = 1

PROJ_T_Q = NA_W + SWA_QW
PROJ_T_V = NA_W + SWA_KVW
PROJ_R_K = NA_W + SWA_KVW
PROJ_R = PROJ_R_K + 3 * DIL_W
PROJ_R_CHUNK = 768

NA_WIN = 768
NA_VARIANTS = 3
NA_LAST_START = SEQ - NA_WIN
SWA_WIN = QBT + 2 * SWA_HALF_WINDOW

F32 = jnp.float32
BF16 = jnp.bfloat16


def _dot(a, b):
    return jnp.dot(a, b, preferred_element_type=F32)


def _dot_nt(a, b):
    return lax.dot_general(a, b, (((1,), (1,)), ((), ())), preferred_element_type=F32)


def _dot_tn(a, b):
    return lax.dot_general(a, b, (((0,), (0,)), ((), ())), preferred_element_type=F32)


def _low_half(shape):
    return lax.broadcasted_iota(jnp.int32, shape, len(shape) - 1) < HEAD_DIM


def _pair_rms(x, gain):
    lo = _low_half(x.shape)
    sq = x * x
    s_lo = jnp.sum(jnp.where(lo, sq, 0.0), axis=-1, keepdims=True)
    s_hi = jnp.sum(jnp.where(lo, 0.0, sq), axis=-1, keepdims=True)
    ms = jnp.where(lo, s_lo, s_hi) * (1.0 / HEAD_DIM)
    return x * lax.rsqrt(ms + NORM_EPS) * gain


def _softmax_block(s):
    m = jnp.max(s, axis=-1, keepdims=True)
    return m, jnp.exp2(s - m).astype(BF16)


def _proj_kernel(x_ref, g_ref, wt_ref, wr_ref, gq_ref, gr_ref,
                 qa_ref, qb_ref, va_ref, vb_ref, k_ref, dil_ref, dil4_ref, dil16_ref, regroup_scr):
    x = x_ref[...]
    ms = jnp.mean(x * x, axis=-1, keepdims=True)
    h = (x * lax.rsqrt(ms + NORM_EPS) * g_ref[...]).astype(BF16)
    tm = h.shape[0]

    def finish_q(qt):
        q3 = qt.reshape(PROJ_T_Q // HEAD_DIM, HEAD_DIM, tm)
        q3 = q3 * lax.rsqrt(jnp.mean(q3 * q3, axis=1, keepdims=True) + NORM_EPS)
        qt = q3.reshape(PROJ_T_Q, tm) * gq_ref[...] * Q_SCALE
        qa_ref[...] = qt[0:NA_W].astype(BF16)
        qb_ref[...] = qt[NA_W:PROJ_T_Q].astype(BF16)

    def finish_v(vt):
        va_ref[...] = vt[0:NA_W].astype(BF16)
        vb_ref[...] = vt[NA_W:PROJ_T_V].astype(BF16)

    n_norm = (PROJ_R_K + 2 * DIL_W) // LANES
    n_key = PROJ_R_K // LANES
    n_dil_q = DIL_W // LANES

    def finish_r(c, r):
        for t in range(PROJ_R_CHUNK // LANES):
            blk = c * (PROJ_R_CHUNK // LANES) + t
            tile = r[:, t * LANES:(t + 1) * LANES]
            if blk < n_norm:
                tile = _pair_rms(tile, gr_ref[:, blk * LANES:(blk + 1) * LANES])
            if n_key <= blk < n_key + n_dil_q:
                tile = tile * Q_SCALE
            if blk < n_key:
                k_ref[:, blk * LANES:(blk + 1) * LANES] = tile.astype(BF16)
                continue
            d = blk - n_key
            cols = slice(d * LANES, (d + 1) * LANES)
            dil_ref[:, cols] = tile.astype(BF16)
            fine, coarse = DIL_STRIDES[1], DIL_STRIDES[2] // DIL_STRIDES[1]
            per_fine = tm // fine
            regroup_scr[d, 0] = tile
            for ph in range(fine):
                part = regroup_scr[d, 0, pl.ds(ph, per_fine, stride=fine), :]
                dil4_ref[0, ph, :, cols] = part.astype(BF16)
                regroup_scr[d, 1, ph * per_fine:(ph + 1) * per_fine, :] = part
            for ph in range(fine):
                for sub_ph in range(coarse):
                    part = regroup_scr[d, 1, pl.ds(ph * per_fine + sub_ph, per_fine // coarse, stride=coarse), :]
                    dil16_ref[0, ph + fine * sub_ph, :, cols] = part.astype(BF16)

    finish_q(_dot_nt(wt_ref[0:PROJ_T_Q, :], h))
    finish_v(_dot_nt(wt_ref[PROJ_T_Q:PROJ_T_Q + PROJ_T_V, :], h))
    for c in range(PROJ_R // PROJ_R_CHUNK):
        finish_r(c, _dot(h, wr_ref[:, c * PROJ_R_CHUNK:(c + 1) * PROJ_R_CHUNK]))


def _group_rms_t(yt, gain_col):
    yf = yt.astype(F32)
    ms = jnp.mean(yf * yf, axis=0, keepdims=True)
    return (yf * lax.rsqrt(ms + NORM_EPS) * gain_col).astype(BF16)


def _out_ffn_kernel(ya_ref, yb_ref, yc_ref, x_ref, gab_ref, gc_ref, wo_ref, gf_ref, wgu_ref, wd_ref,
                    o_ref, h_scr):
    ya = _group_rms_t(ya_ref[...], gab_ref[0:NA_W, :])
    yb = _group_rms_t(yb_ref[...], gab_ref[NA_W:NA_W + SWA_QW, :])
    yc = yc_ref[...].astype(F32)
    ms = jnp.mean(yc * yc, axis=-1, keepdims=True)
    yc = (yc * lax.rsqrt(ms + NORM_EPS) * gc_ref[...]).astype(BF16)
    acc = _dot_tn(ya, wo_ref[0:NA_W, :])
    acc = acc + _dot_tn(yb, wo_ref[NA_W:NA_W + SWA_QW, :])
    acc = acc + _dot(yc, wo_ref[NA_W + SWA_QW:MIX_W, :])
    x = x_ref[...] + acc

    ms = jnp.mean(x * x, axis=-1, keepdims=True)
    h_scr[...] = (x * lax.rsqrt(ms + NORM_EPS) * gf_ref[...]).astype(BF16)
    o_ref[...] = x
    lo = 0
    for width in FFN_H_CHUNKS:
        gate = _dot(h_scr[...], wgu_ref[:, lo:lo + width])
        up = _dot(h_scr[...], wgu_ref[:, FFN_HIDDEN + lo:FFN_HIDDEN + lo + width])
        act = (gate / (1.0 + jnp.exp(-gate)) * up).astype(BF16)
        o_ref[...] += _dot(act, wd_ref[lo:lo + width, :])
        lo += width


def _pipeline3(chains, stage_a, stage_b, stage_c, lag=1):
    n = len(chains)
    a, b = {}, {}
    for i in range(-2 * lag, n):
        if 0 <= i + 2 * lag < n:
            a[i + 2 * lag] = stage_a(chains[i + 2 * lag])
        if 0 <= i + lag < n:
            b[i + lag] = stage_b(chains[i + lag], a.pop(i + lag))
        if i >= 0:
            stage_c(chains[i], b.pop(i))


def _attend_t(chains, lag=1):
    def scores(c):
        qt = c["q"]()
        zeros = jnp.zeros_like(qt)
        qpad = jnp.concatenate([qt, zeros] if c["half"] == 0 else [zeros, qt], axis=0)
        kw = c["k"]()
        mid = kw.shape[0] // 2
        s = jnp.concatenate([_dot(kw[:mid], qpad), _dot(kw[mid:], qpad)], axis=0)
        return s + c["bias"]()

    def softmax(c, s):
        m = jnp.max(s, axis=0, keepdims=True)
        if c["sink"] is not None:
            m = jnp.maximum(m, c["sink"])
        return jnp.exp2(s - m).astype(BF16), m

    def values(c, pm):
        p, m = pm
        vt = c["v"]()
        own = vt[c["half"] * HEAD_DIM:(c["half"] + 1) * HEAD_DIM, :]
        ones = jnp.ones_like(own)
        o = _dot(jnp.concatenate([own, ones], axis=0), p)
        den = o[HEAD_DIM:2 * HEAD_DIM, :]
        if c["sink"] is not None:
            den = den + jnp.exp2(c["sink"] - m)
        c["store"](o[0:HEAD_DIM, :] / den)

    _pipeline3(chains, scores, softmax, values, lag=lag)


NA_KEY_ROWS = NA_WIN // GRID_W
NA_Q_ROWS = QBT // GRID_W
NA_ROW_REL = 2 * NA_ROWS - 1


def _na_geometry():
    n_blocks = SEQ // QBT
    out = []
    for variant, block in enumerate((0, 1, n_blocks - 1)):
        start = min(max(block * QBT - QBT, 0), NA_LAST_START)
        assert (block * QBT - start) // QBT == variant
        out.append((start // GRID_W, block * NA_Q_ROWS))
    return out


def _na_kernel(q_ref, k_ref, v_ref, pair_ref, o_ref, tbl_scr):
    @pl.when(pl.program_id(0) == 0)
    def _():
        grid_rows = SEQ // GRID_W
        lo = _low_half((GRID_W, LANES))
        masked = jnp.full((GRID_W, LANES), NEG_INF, F32)
        for variant, (key_row0, query_row0) in enumerate(_na_geometry()):
            for kr in range(NA_KEY_ROWS):
                for pair in range(NA_Q_ROWS // 2):
                    krow = key_row0 + kr
                    rel, ok = [], []
                    for qrow in (query_row0 + 2 * pair, query_row0 + 2 * pair + 1):
                        first = min(max(qrow - NA_ROWS // 2, 0), grid_rows - NA_ROWS)
                        ok.append(first <= krow < first + NA_ROWS)
                        rel.append(krow - qrow + NA_ROWS - 1)
                    for h in range(NA_HEADS):
                        if not (ok[0] or ok[1]):
                            tile = masked
                        else:
                            tile = pair_ref[h, rel[0]]
                            if not ok[1]:
                                tile = jnp.where(lo, tile, masked)
                            elif not ok[0]:
                                tile = jnp.where(lo, masked, tile)
                        tbl_scr[h, variant, kr * GRID_W:(kr + 1) * GRID_W, pair * LANES:(pair + 1) * LANES] = tile

    def store(rows, cols):
        def put(o):
            o_ref[rows, cols] = o.astype(BF16)
        return put

    chains = []
    for j in range(SEQ // QBT):
        q0 = j * QBT
        start = min(max(q0 - QBT, 0), NA_LAST_START)
        variant = (q0 - start) // QBT
        cols = slice(q0, q0 + QBT)
        keys = slice(start, start + NA_WIN)
        for h in range(NA_HEADS):
            rows = slice(h * HEAD_DIM, (h + 1) * HEAD_DIM)
            pair = slice((h // 2) * LANES, (h // 2 + 1) * LANES)
            chains.append(dict(
                half=h % 2, sink=None,
                q=lambda rows=rows, cols=cols: q_ref[rows, cols],
                k=lambda keys=keys, pair=pair: k_ref[keys, pair],
                v=lambda keys=keys, pair=pair: v_ref[pair, keys],
                bias=lambda h=h, variant=variant: tbl_scr[h, variant],
                store=store(rows, cols)))
    _attend_t(chains, lag=NA_LAG)


def _na_pair_table(rpb):
    cols = np.arange(GRID_W)
    col_start = np.clip(cols - NA_COLS // 2, 0, GRID_W - NA_COLS)
    col_ok = (cols[:, None] >= col_start[None, :]) & (cols[:, None] < col_start[None, :] + NA_COLS)
    heads, n_row_rel, n_col_rel = rpb.shape
    assert n_row_rel == NA_ROW_REL
    period = 2 * GRID_W - 1
    lead = GRID_W - NA_COLS
    seq = jnp.pad(rpb.astype(F32)[:, :, ::-1], ((0, 0), (0, 0), (lead, period - lead - n_col_rel)))
    skew = jnp.tile(seq, (1, 1, GRID_W + 1))[..., :GRID_W * 2 * GRID_W]
    toe = skew.reshape(heads, n_row_rel, GRID_W, 2 * GRID_W)[:, :, ::-1, :GRID_W]
    toe = jnp.where(col_ok[None, None], toe * LOG2E, NEG_INF)
    toe = jnp.pad(toe, ((0, 0), (1, 1), (0, 0), (0, 0)), constant_values=NEG_INF)
    return jnp.concatenate([toe[:, 1:], toe[:, :-1]], axis=-1)


def _swa_kernel(layer, slopes_ref, sink_ref, q_ref, k_ref, v_ref, o_ref, bias_scr):
    @pl.when(pl.program_id(0) == 0)
    def _():
        kk = lax.broadcasted_iota(jnp.int32, (SWA_WIN, QBT), 0)
        qq = lax.broadcasted_iota(jnp.int32, (SWA_WIN, QBT), 1)
        dist = jnp.abs(kk - SWA_HALF_WINDOW - qq)
        in_window = dist <= SWA_HALF_WINDOW
        dist_f = dist.astype(F32)
        for h in range(SWA_Q_HEADS):
            bias_scr[h] = jnp.where(in_window, -slopes_ref[h] * dist_f * LOG2E, NEG_INF)

    def store(rows, cols):
        def put(o):
            o_ref[rows, cols] = o.astype(BF16)
        return put

    chains = []
    for j in range(SEQ // QBT):
        q0 = j * QBT
        lo = max(q0 - SWA_HALF_WINDOW, 0)
        hi = min(q0 + QBT + SWA_HALF_WINDOW, SEQ)
        first = lo - (q0 - SWA_HALF_WINDOW)
        cols = slice(q0, q0 + QBT)
        keys = slice(lo, hi)
        win = slice(first, first + hi - lo)
        for h in range(SWA_Q_HEADS):
            rows = slice(h * HEAD_DIM, (h + 1) * HEAD_DIM)
            chains.append(dict(
                half=h // SWA_GROUP, sink=sink_ref[layer, h] * LOG2E,
                q=lambda rows=rows, cols=cols: q_ref[rows, cols],
                k=lambda keys=keys: k_ref[keys, :],
                v=lambda keys=keys: v_ref[:, keys],
                bias=lambda h=h, win=win: bias_scr[h, win, :],
                store=store(rows, cols)))
    _attend_t(chains)


DIL_QB = 128
DIL_WIN = DIL_QB + 2 * DIL_HALF


def _dil_kernel(slopes_ref, q1_ref, k1_ref, v1_ref, q4_ref, k4_ref, v4_ref, q16_ref, k16_ref, v16_ref,
                o_ref, bias_scr, bias16_scr, o_scr, lse_scr, out_scr):
    pair = pl.program_id(1)

    @pl.when(pl.program_id(0) == 0)
    def _():
        slope = [slopes_ref[SWA_Q_HEADS + 2 * pair + h] for h in range(2)]
        qq = lax.broadcasted_iota(jnp.int32, (DIL_QB, DIL_WIN), 0)
        kk = lax.broadcasted_iota(jnp.int32, (DIL_QB, DIL_WIN), 1)
        for shift in range(3):
            dist = jnp.abs(kk - shift * DIL_HALF - qq)
            ok = dist <= DIL_HALF
            dist_f = dist.astype(F32)
            for h in range(2):
                for b, stride in enumerate(DIL_STRIDES[:2]):
                    bias_scr[pair, b, shift, h * DIL_QB:(h + 1) * DIL_QB, :] = jnp.where(
                        ok, -slope[h] * (dist_f * float(stride)) * LOG2E, NEG_INF)
        qq = lax.broadcasted_iota(jnp.int32, (QB, QB), 0)
        kk = lax.broadcasted_iota(jnp.int32, (QB, QB), 1)
        dist = jnp.abs(kk - qq)
        for h in range(2):
            bias16_scr[pair, h * QB:(h + 1) * QB, :] = jnp.where(
                dist <= DIL_HALF, -slope[h] * (dist.astype(F32) * float(DIL_STRIDES[2])) * LOG2E, NEG_INF)

    chains = []
    for b, stride in enumerate(DIL_STRIDES):
        sub = SEQ // stride
        qb = QB if sub == QB else DIL_QB
        for start in range(0, SEQ, qb):
            phase, local_q = divmod(start, sub)
            if sub == QB:
                local_k, n_keys, shift = 0, QB, None
            else:
                local_k = min(max(local_q - DIL_HALF, 0), sub - DIL_WIN)
                n_keys, shift = DIL_WIN, (local_q - local_k) // DIL_HALF
            qrows, krows = slice(local_q, local_q + qb), slice(local_k, local_k + n_keys)
            fine = DIL_STRIDES[1]
            if stride == 1:
                refs = (q1_ref, k1_ref, v1_ref)
                load = lambda ref, rows: ref[rows, :]
                dst = qrows
            elif stride == fine:
                refs = (q4_ref, k4_ref, v4_ref)
                load = lambda ref, rows, phase=phase: ref[0, phase, rows, :]
                dst = slice(start, start + qb)
            else:
                refs = (q16_ref, k16_ref, v16_ref)
                load = lambda ref, rows, phase=phase: ref[0, phase, rows, :]
                coarse = stride // fine
                dst = pl.ds((phase % fine) * (SEQ // fine) + local_q * coarse + phase // fine, qb, stride=coarse)
            chains.append(dict(b=b, shift=shift, dst=dst, qb=qb, order=start + qb,
                               q=lambda refs=refs, load=load, rows=qrows: load(refs[0], rows),
                               k=lambda refs=refs, load=load, rows=krows: load(refs[1], rows),
                               v=lambda refs=refs, load=load, rows=krows: load(refs[2], rows)))
    chains.sort(key=lambda c: c["order"])

    def scores(c):
        q, kw = c["q"](), c["k"]()
        lo = _low_half(q.shape)
        zeros = jnp.zeros_like(q)
        q_both = jnp.concatenate([jnp.where(lo, q, zeros), jnp.where(lo, zeros, q)], axis=0)
        bias = bias16_scr[pair] if c["shift"] is None else bias_scr[pair, c["b"], c["shift"]]
        return _dot_nt(q_both, kw) + bias

    def softmax(c, s):
        return _softmax_block(s)

    def values(c, mp):
        m, p = mp
        vw = c["v"]()
        res = _dot(p, jnp.concatenate([vw, jnp.ones_like(vw)], axis=1))
        qb = c["qb"]
        lo = _low_half((qb, LANES))
        den = jnp.where(lo, res[0:qb, LANES:2 * LANES], res[qb:2 * qb, LANES:2 * LANES])
        o_scr[c["b"], c["dst"], :] = jnp.where(lo, res[0:qb, 0:LANES], res[qb:2 * qb, 0:LANES]) / den
        lse_scr[c["b"], c["dst"], :] = jnp.where(lo, m[0:qb], m[qb:2 * qb]) + jnp.log2(den)

    _pipeline3(chains, scores, softmax, values, lag=DIL_LAG)

    fine = DIL_STRIDES[1]
    per_phase = SEQ // fine

    def merge(i, carry):
        start = pl.multiple_of(i * PREP_ROWS, PREP_ROWS)
        rows = pl.ds(start, PREP_ROWS)
        natural = pl.ds((start % per_phase) * fine + start // per_phase, PREP_ROWS, stride=fine)
        lse = [lse_scr[0, natural, :], lse_scr[1, rows, :], lse_scr[2, rows, :]]
        outs = [o_scr[0, natural, :], o_scr[1, rows, :], o_scr[2, rows, :]]
        top = jnp.maximum(jnp.maximum(lse[0], lse[1]), lse[2])
        num = jnp.zeros((PREP_ROWS, LANES), F32)
        den = jnp.zeros((PREP_ROWS, LANES), F32)
        for b in range(3):
            w = jnp.exp2(lse[b] - top)
            num = num + w * outs[b]
            den = den + w
        out_scr[natural, :] = num / den
        return carry

    lax.fori_loop(0, SEQ // PREP_ROWS, merge, 0)

    def emit(i, carry):
        rows = pl.ds(pl.multiple_of(i * PREP_ROWS, PREP_ROWS), PREP_ROWS)
        o_ref[rows, :] = out_scr[rows, :].astype(BF16)
        return carry

    lax.fori_loop(0, SEQ // PREP_ROWS, emit, 0)


def _params(semantics):
    return pltpu.CompilerParams(dimension_semantics=semantics, vmem_limit_bytes=VMEM_LIMIT)


def _resident(shape, layer):
    return pl.BlockSpec((None,) + tuple(shape), lambda *_: (layer,) + (0,) * len(shape),
                        pipeline_mode=pl.Buffered(1))


def _smem():
    return pl.BlockSpec(memory_space=pltpu.SMEM)


def _proj(x, layer, gain, w_t, w_r, gain_q, gain_r):
    t = x.shape[0]
    rows = PROJ_ROW_TILE
    feat = lambda n: pl.BlockSpec((n, rows), lambda i: (0, i))
    tok = lambda n: pl.BlockSpec((rows, n), lambda i: (i, 0))
    tiles_per_seq = SEQ // rows

    def phased(stride):
        return pl.BlockSpec((1, stride, rows // stride, 3 * DIL_W),
                            lambda i: (i // tiles_per_seq, 0, i % tiles_per_seq, 0))

    def phased_shape(stride):
        return jax.ShapeDtypeStruct((t // SEQ, stride, SEQ // stride, 3 * DIL_W), BF16)

    return pl.pallas_call(
        _proj_kernel,
        out_shape=[jax.ShapeDtypeStruct((NA_W, t), BF16), jax.ShapeDtypeStruct((SWA_QW, t), BF16),
                   jax.ShapeDtypeStruct((NA_W, t), BF16), jax.ShapeDtypeStruct((SWA_KVW, t), BF16),
                   jax.ShapeDtypeStruct((t, PROJ_R_K), BF16), jax.ShapeDtypeStruct((t, 3 * DIL_W), BF16),
                   phased_shape(DIL_STRIDES[1]), phased_shape(DIL_STRIDES[2])],
        grid=(t // rows,),
        in_specs=[tok(D_MODEL), _resident((1, D_MODEL), layer),
                  _resident((PROJ_T_Q + PROJ_T_V, D_MODEL), layer), _resident((D_MODEL, PROJ_R), layer),
                  _resident((PROJ_T_Q, 1), layer), _resident((1, PROJ_R_K + 2 * DIL_W), layer)],
        out_specs=[feat(NA_W), feat(SWA_QW), feat(NA_W), feat(SWA_KVW), tok(PROJ_R_K), tok(3 * DIL_W),
                   phased(DIL_STRIDES[1]), phased(DIL_STRIDES[2])],
        scratch_shapes=[pltpu.VMEM((3 * DIL_W // LANES, 2, rows, LANES), F32)],
        compiler_params=_params(("parallel",)),
        name="proj",
    )(x, gain, w_t, w_r, gain_q, gain_r)


def _na(q_t, keys, v_t, layer, table):
    t = q_t.shape[1]
    feat = pl.BlockSpec((NA_W, SEQ), lambda b: (0, b))
    return pl.pallas_call(
        _na_kernel,
        out_shape=jax.ShapeDtypeStruct((NA_W, t), BF16),
        grid=(t // SEQ,),
        in_specs=[feat, pl.BlockSpec((SEQ, NA_W), lambda b: (b, 0)), feat,
                  _resident((NA_HEADS, NA_ROW_REL + 1, GRID_W, 2 * GRID_W), layer)],
        out_specs=feat,
        scratch_shapes=[pltpu.VMEM((NA_HEADS, NA_VARIANTS, NA_WIN, QBT), F32)],
        compiler_params=_params(("arbitrary",)),
        name="na",
    )(q_t, keys, v_t, table)


def _swa(q_t, keys, v_t, layer, slopes, sink):
    t = q_t.shape[1]
    return pl.pallas_call(
        functools.partial(_swa_kernel, layer),
        out_shape=jax.ShapeDtypeStruct((SWA_QW, t), BF16),
        grid=(t // SEQ,),
        in_specs=[_smem(), _smem(),
                  pl.BlockSpec((SWA_QW, SEQ), lambda b: (0, b)),
                  pl.BlockSpec((SEQ, SWA_KVW), lambda b: (b, NA_W // SWA_KVW)),
                  pl.BlockSpec((SWA_KVW, SEQ), lambda b: (0, b))],
        out_specs=pl.BlockSpec((SWA_QW, SEQ), lambda b: (0, b)),
        scratch_shapes=[pltpu.VMEM((SWA_Q_HEADS, SWA_WIN, QBT), F32)],
        compiler_params=_params(("arbitrary",)),
        name="swa",
    )(slopes, sink, q_t, keys, v_t)


def _dil(qkv, qkv4, qkv16, slopes):
    t = qkv.shape[0]
    pairs = DIL_W // LANES
    seq_block = lambda first: pl.BlockSpec((SEQ, LANES), lambda b, p: (b, first + p))

    def phased(stride, first):
        return pl.BlockSpec((1, stride, SEQ // stride, LANES), lambda b, p: (b, 0, 0, first + p))

    operands, specs = [], []
    for arr, spec in ((qkv, seq_block), (qkv4, lambda f: phased(DIL_STRIDES[1], f)),
                      (qkv16, lambda f: phased(DIL_STRIDES[2], f))):
        for part in range(3):
            operands.append(arr)
            specs.append(spec(part * pairs))
    return pl.pallas_call(
        _dil_kernel,
        out_shape=jax.ShapeDtypeStruct((t, DIL_W), BF16),
        grid=(t // SEQ, pairs),
        in_specs=[_smem()] + specs,
        out_specs=seq_block(0),
        scratch_shapes=[pltpu.VMEM((pairs, 2, 3, 2 * DIL_QB, DIL_WIN), F32),
                        pltpu.VMEM((pairs, 2 * QB, QB), F32),
                        pltpu.VMEM((3, SEQ, LANES), F32),
                        pltpu.VMEM((3, SEQ, LANES), F32),
                        pltpu.VMEM((SEQ, LANES), F32)],
        compiler_params=_params(("arbitrary", "arbitrary")),
        name="dil",
    )(slopes, *operands)


def _out_ffn(ya_t, yb_t, yc, x, layer, gain_ab, gain_c, w_out, gain_ffn, w_gu, w_down):
    t = x.shape[0]
    feat = lambda n: pl.BlockSpec((n, ROW_TILE), lambda i: (0, i))
    tok = lambda n: pl.BlockSpec((ROW_TILE, n), lambda i: (i, 0))
    return pl.pallas_call(
        _out_ffn_kernel,
        out_shape=jax.ShapeDtypeStruct((t, D_MODEL), F32),
        grid=(t // ROW_TILE,),
        in_specs=[feat(NA_W), feat(SWA_QW), tok(DIL_W), tok(D_MODEL),
                  _resident((NA_W + SWA_QW, 1), layer), _resident((1, DIL_W), layer),
                  _resident((MIX_W, D_MODEL), layer), _resident((1, D_MODEL), layer),
                  _resident((D_MODEL, 2 * FFN_HIDDEN), layer), _resident((FFN_HIDDEN, D_MODEL), layer)],
        out_specs=tok(D_MODEL),
        scratch_shapes=[pltpu.VMEM((ROW_TILE, D_MODEL), BF16)],
        compiler_params=_params(("parallel",)),
        name="out_ffn",
    )(ya_t, yb_t, yc, x, gain_ab, gain_c, w_out, gain_ffn, w_gu, w_down)


def _split_w_in(w):
    cuts = np.cumsum([NA_W, NA_W, NA_W, SWA_QW, SWA_KVW, SWA_KVW, DIL_W, DIL_W])
    qa, ka, va, qb, kb, vb, qc, kc, vc = jnp.split(w, [int(c) for c in cuts], axis=2)
    w_t = jnp.concatenate([qa, qb, va, vb], axis=2).astype(BF16).transpose(0, 2, 1)
    w_r = jnp.concatenate([ka, kb, qc, kc, vc], axis=2).astype(BF16)
    return w_t, w_r


def kernel(x, attn_norm, w_in, qk_gain, rpb, sink, out_gain, w_out, ffn_norm, w_gu, w_down):
    batch, seq, d = x.shape
    assert (seq, d) == (SEQ, D_MODEL)
    depth = w_in.shape[0]
    n_slopes = SWA_Q_HEADS + DIL_HEADS
    slopes = 2.0 ** (-8.0 * (jnp.arange(n_slopes, dtype=F32) + 1.0) / n_slopes)
    xf = x.reshape(batch * seq, d)

    g = qk_gain.astype(F32)
    tile = lambda v, n: jnp.tile(v, (1, n))
    gain_q = jnp.concatenate([tile(g[:, 0, 0], NA_HEADS), tile(g[:, 1, 0], SWA_Q_HEADS)], axis=1)[:, :, None]
    gain_r = jnp.concatenate([tile(g[:, 0, 1], NA_HEADS), tile(g[:, 1, 1], SWA_KV_HEADS),
                              tile(g[:, 2, 0], DIL_HEADS), tile(g[:, 2, 1], DIL_HEADS)], axis=1)[:, None, :]
    w_t, w_r = _split_w_in(w_in)
    table = _na_pair_table(rpb.reshape((depth * NA_HEADS,) + rpb.shape[2:]))
    table = table.reshape((depth, NA_HEADS) + table.shape[1:])
    og = out_gain.astype(F32)
    gain_ab, gain_c = og[:, :NA_W + SWA_QW, None], og[:, None, NA_W + SWA_QW:]
    w_out_b, w_gu_b, w_down_b = w_out.astype(BF16), w_gu.astype(BF16), w_down.astype(BF16)
    attn_gain, ffn_gain = attn_norm.astype(F32)[:, None, :], ffn_norm.astype(F32)[:, None, :]
    sink = sink.astype(F32)

    for l in range(depth):
        qa_t, qb_t, va_t, vb_t, keys, dil1, dil4, dil16 = _proj(xf, l, attn_gain, w_t, w_r, gain_q, gain_r)
        ya_t = _na(qa_t, keys, va_t, l, table)
        yb_t = _swa(qb_t, keys, vb_t, l, slopes, sink)
        yc = _dil(dil1, dil4, dil16, slopes)
        xf = _out_ffn(ya_t, yb_t, yc, xf, l, gain_ab, gain_c, w_out_b, ffn_gain, w_gu_b, w_down_b)
    return xf.reshape(batch, seq, d)
```

```python
import functools
import math

import numpy as np
import jax
import jax.numpy as jnp
from jax import lax
from jax.experimental import pallas as pl
from jax.experimental.pallas import tpu as pltpu

D_MODEL = 1024
SEQ = 2048
HEAD_DIM = 64
GRID_W = 64
NA_HEADS = 4
NA_ROWS = 8
NA_COLS = 16
SWA_Q_HEADS = 6
SWA_KV_HEADS = 2
SWA_GROUP = SWA_Q_HEADS // SWA_KV_HEADS
SWA_HALF_WINDOW = 128
DIL_HEADS = 6
DIL_STRIDES = (1, 4, 16)
DIL_HALF = 64
NA_W = NA_HEADS * HEAD_DIM
SWA_QW = SWA_Q_HEADS * HEAD_DIM
SWA_KVW = SWA_KV_HEADS * HEAD_DIM
DIL_W = DIL_HEADS * HEAD_DIM
IN_W = 3 * NA_W + SWA_QW + 2 * SWA_KVW + 3 * DIL_W
MIX_W = NA_W + SWA_QW + DIL_W
FFN_HIDDEN = 2816
NORM_EPS = 1e-6
NEG_INF = -1e30
QK_SCALE = HEAD_DIM ** -0.5
LOG2E = math.log2(math.e)
Q_SCALE = QK_SCALE * LOG2E

LANES = 128
VMEM_LIMIT = 56 * 1024 * 1024

QB = 128
QBT = 256
PREP_ROWS = 256
NQB = SEQ // QB
ROW_TILE = 1024
PROJ_ROW_TILE = 1024
MXU_DIM = 256
FFN_H_CHUNKS = (3 * MXU_DIM,) * 3 + (2 * MXU_DIM,)
assert sum(FFN_H_CHUNKS) == FFN_HIDDEN
NA_LAG = 2
DIL_LAG = 1

PROJ_T_Q = NA_W + SWA_QW
PROJ_T_V = NA_W + SWA_KVW
PROJ_R_K = NA_W + SWA_KVW
PROJ_R = PROJ_R_K + 3 * DIL_W
PROJ_R_CHUNK = 768

NA_WIN = 768
NA_VARIANTS = 3
NA_LAST_START = SEQ - NA_WIN
SWA_WIN = QBT + 2 * SWA_HALF_WINDOW

F32 = jnp.float32
BF16 = jnp.bfloat16


def _dot(a, b):
    return jnp.dot(a, b, preferred_element_type=F32)


def _dot_nt(a, b):
    return lax.dot_general(a, b, (((1,), (1,)), ((), ())), preferred_element_type=F32)


def _dot_tn(a, b):
    return lax.dot_general(a, b, (((0,), (0,)), ((), ())), preferred_element_type=F32)


def _low_half(shape):
    return lax.broadcasted_iota(jnp.int32, shape, len(shape) - 1) < HEAD_DIM


def _pair_rms(x, gain):
    lo = _low_half(x.shape)
    sq = x * x
    s_lo = jnp.sum(jnp.where(lo, sq, 0.0), axis=-1, keepdims=True)
    s_hi = jnp.sum(jnp.where(lo, 0.0, sq), axis=-1, keepdims=True)
    ms = jnp.where(lo, s_lo, s_hi) * (1.0 / HEAD_DIM)
    return x * lax.rsqrt(ms + NORM_EPS) * gain


def _softmax_block(s):
    m = jnp.max(s, axis=-1, keepdims=True)
    return m, jnp.exp2(s - m).astype(BF16)


def _proj_kernel(x_ref, g_ref, wt_ref, wr_ref, gq_ref, gr_ref,
                 qa_ref, qb_ref, va_ref, vb_ref, k_ref, dil_ref):
    x = x_ref[...]
    ms = jnp.mean(x * x, axis=-1, keepdims=True)
    h = (x * lax.rsqrt(ms + NORM_EPS) * g_ref[...]).astype(BF16)
    tm = h.shape[0]

    def finish_q(qt):
        q3 = qt.reshape(PROJ_T_Q // HEAD_DIM, HEAD_DIM, tm)
        q3 = q3 * lax.rsqrt(jnp.mean(q3 * q3, axis=1, keepdims=True) + NORM_EPS)
        qt = q3.reshape(PROJ_T_Q, tm) * gq_ref[...] * Q_SCALE
        qa_ref[...] = qt[0:NA_W].astype(BF16)
        qb_ref[...] = qt[NA_W:PROJ_T_Q].astype(BF16)

    def finish_v(vt):
        va_ref[...] = vt[0:NA_W].astype(BF16)
        vb_ref[...] = vt[NA_W:PROJ_T_V].astype(BF16)

    n_key = PROJ_R_K // LANES

    def finish_r(c, r):
        for t in range(PROJ_R_CHUNK // LANES):
            blk = c * (PROJ_R_CHUNK // LANES) + t
            tile = r[:, t * LANES:(t + 1) * LANES]
            kind = "k" if blk < n_key else "qkv"[(blk - n_key) % 3]
            if kind != "v":
                tile = _pair_rms(tile, gr_ref[:, blk * LANES:(blk + 1) * LANES])
            if kind == "q":
                tile = tile * Q_SCALE
            if blk < n_key:
                k_ref[:, blk * LANES:(blk + 1) * LANES] = tile.astype(BF16)
            else:
                dil_ref[:, (blk - n_key) * LANES:(blk - n_key + 1) * LANES] = tile.astype(BF16)

    finish_q(_dot_nt(wt_ref[0:PROJ_T_Q, :], h))
    finish_v(_dot_nt(wt_ref[PROJ_T_Q:PROJ_T_Q + PROJ_T_V, :], h))
    for c in range(PROJ_R // PROJ_R_CHUNK):
        finish_r(c, _dot(h, wr_ref[:, c * PROJ_R_CHUNK:(c + 1) * PROJ_R_CHUNK]))


def _group_rms_t(yt, gain_col):
    yf = yt.astype(F32)
    ms = jnp.mean(yf * yf, axis=0, keepdims=True)
    return (yf * lax.rsqrt(ms + NORM_EPS) * gain_col).astype(BF16)


def _out_ffn_kernel(ya_ref, yb_ref, yc_ref, x_ref, gab_ref, gc_ref, wo_ref, gf_ref, wgu_ref, wd_ref,
                    o_ref, h_scr):
    ya = _group_rms_t(ya_ref[...], gab_ref[0:NA_W, :])
    yb = _group_rms_t(yb_ref[...], gab_ref[NA_W:NA_W + SWA_QW, :])
    yc = yc_ref[...].astype(F32)
    ms = jnp.mean(yc * yc, axis=-1, keepdims=True)
    yc = (yc * lax.rsqrt(ms + NORM_EPS) * gc_ref[...]).astype(BF16)
    acc = _dot_tn(ya, wo_ref[0:NA_W, :])
    acc = acc + _dot_tn(yb, wo_ref[NA_W:NA_W + SWA_QW, :])
    acc = acc + _dot(yc, wo_ref[NA_W + SWA_QW:MIX_W, :])
    x = x_ref[...] + acc

    ms = jnp.mean(x * x, axis=-1, keepdims=True)
    h_scr[...] = (x * lax.rsqrt(ms + NORM_EPS) * gf_ref[...]).astype(BF16)
    o_ref[...] = x
    lo = 0
    for width in FFN_H_CHUNKS:
        gate = _dot(h_scr[...], wgu_ref[:, lo:lo + width])
        up = _dot(h_scr[...], wgu_ref[:, FFN_HIDDEN + lo:FFN_HIDDEN + lo + width])
        act = (gate / (1.0 + jnp.exp(-gate)) * up).astype(BF16)
        o_ref[...] += _dot(act, wd_ref[lo:lo + width, :])
        lo += width


def _pipeline3(chains, stage_a, stage_b, stage_c, lag=1):
    n = len(chains)
    a, b = {}, {}
    for i in range(-2 * lag, n):
        if 0 <= i + 2 * lag < n:
            a[i + 2 * lag] = stage_a(chains[i + 2 * lag])
        if 0 <= i + lag < n:
            b[i + lag] = stage_b(chains[i + lag], a.pop(i + lag))
        if i >= 0:
            stage_c(chains[i], b.pop(i))


def _attend_t(chains, lag=1):
    def scores(c):
        qt = c["q"]()
        zeros = jnp.zeros_like(qt)
        qpad = jnp.concatenate([qt, zeros] if c["half"] == 0 else [zeros, qt], axis=0)
        kw = c["k"]()
        mid = kw.shape[0] // 2
        s = jnp.concatenate([_dot(kw[:mid], qpad), _dot(kw[mid:], qpad)], axis=0)
        return s + c["bias"]()

    def softmax(c, s):
        m = jnp.max(s, axis=0, keepdims=True)
        if c["sink"] is not None:
            m = jnp.maximum(m, c["sink"])
        return jnp.exp2(s - m).astype(BF16), m

    def values(c, pm):
        p, m = pm
        vt = c["v"]()
        own = vt[c["half"] * HEAD_DIM:(c["half"] + 1) * HEAD_DIM, :]
        ones = jnp.ones_like(own)
        v_aug = jnp.concatenate([own, ones], axis=0)
        if c["split_values"]:
            cut = max(MXU_DIM, (p.shape[0] // 2) // MXU_DIM * MXU_DIM)
            o = _dot(v_aug[:, :cut], p[:cut]) + _dot(v_aug[:, cut:], p[cut:])
        else:
            o = _dot(v_aug, p)
        den = o[HEAD_DIM:2 * HEAD_DIM, :]
        if c["sink"] is not None:
            den = den + jnp.exp2(c["sink"] - m)
        c["store"](o[0:HEAD_DIM, :] / den)

    _pipeline3(chains, scores, softmax, values, lag=lag)


NA_KEY_ROWS = NA_WIN // GRID_W
NA_Q_ROWS = QBT // GRID_W
NA_ROW_REL = 2 * NA_ROWS - 1


def _na_geometry():
    n_blocks = SEQ // QBT
    out = []
    for variant, block in enumerate((0, 1, n_blocks - 1)):
        start = min(max(block * QBT - QBT, 0), NA_LAST_START)
        assert (block * QBT - start) // QBT == variant
        out.append((start // GRID_W, block * NA_Q_ROWS))
    return out


def _na_kernel(q_ref, k_ref, v_ref, pair_ref, o_ref, tbl_scr):
    @pl.when(pl.program_id(0) == 0)
    def _():
        grid_rows = SEQ // GRID_W
        lo = _low_half((GRID_W, LANES))
        masked = jnp.full((GRID_W, LANES), NEG_INF, F32)
        for variant, (key_row0, query_row0) in enumerate(_na_geometry()):
            for kr in range(NA_KEY_ROWS):
                for pair in range(NA_Q_ROWS // 2):
                    krow = key_row0 + kr
                    rel, ok = [], []
                    for qrow in (query_row0 + 2 * pair, query_row0 + 2 * pair + 1):
                        first = min(max(qrow - NA_ROWS // 2, 0), grid_rows - NA_ROWS)
                        ok.append(first <= krow < first + NA_ROWS)
                        rel.append(krow - qrow + NA_ROWS - 1)
                    for h in range(NA_HEADS):
                        if not (ok[0] or ok[1]):
                            tile = masked
                        else:
                            tile = pair_ref[h, rel[0]]
                            if not ok[1]:
                                tile = jnp.where(lo, tile, masked)
                            elif not ok[0]:
                                tile = jnp.where(lo, masked, tile)
                        tbl_scr[h, variant, kr * GRID_W:(kr + 1) * GRID_W, pair * LANES:(pair + 1) * LANES] = tile

    def store(rows, cols):
        def put(o):
            o_ref[rows, cols] = o.astype(BF16)
        return put

    chains = []
    for j in range(SEQ // QBT):
        q0 = j * QBT
        start = min(max(q0 - QBT, 0), NA_LAST_START)
        variant = (q0 - start) // QBT
        cols = slice(q0, q0 + QBT)
        keys = slice(start, start + NA_WIN)
        for h in range(NA_HEADS):
            rows = slice(h * HEAD_DIM, (h + 1) * HEAD_DIM)
            pair = slice((h // 2) * LANES, (h // 2 + 1) * LANES)
            chains.append(dict(
                half=h % 2, sink=None, split_values=False,
                q=lambda rows=rows, cols=cols: q_ref[rows, cols],
                k=lambda keys=keys, pair=pair: k_ref[keys, pair],
                v=lambda keys=keys, pair=pair: v_ref[pair, keys],
                bias=lambda h=h, variant=variant: tbl_scr[h, variant],
                store=store(rows, cols)))
    _attend_t(chains, lag=NA_LAG)


def _na_pair_table(rpb):
    cols = np.arange(GRID_W)
    col_start = np.clip(cols - NA_COLS // 2, 0, GRID_W - NA_COLS)
    col_ok = (cols[:, None] >= col_start[None, :]) & (cols[:, None] < col_start[None, :] + NA_COLS)
    heads, n_row_rel, n_col_rel = rpb.shape
    assert n_row_rel == NA_ROW_REL
    period = 2 * GRID_W - 1
    lead = GRID_W - NA_COLS
    seq = jnp.pad(rpb.astype(F32)[:, :, ::-1], ((0, 0), (0, 0), (lead, period - lead - n_col_rel)))
    skew = jnp.tile(seq, (1, 1, GRID_W + 1))[..., :GRID_W * 2 * GRID_W]
    toe = skew.reshape(heads, n_row_rel, GRID_W, 2 * GRID_W)[:, :, ::-1, :GRID_W]
    toe = jnp.where(col_ok[None, None], toe * LOG2E, NEG_INF)
    toe = jnp.pad(toe, ((0, 0), (1, 1), (0, 0), (0, 0)), constant_values=NEG_INF)
    return jnp.concatenate([toe[:, 1:], toe[:, :-1]], axis=-1)


def _swa_kernel(layer, slopes_ref, sink_ref, q_ref, k_ref, v_ref, o_ref, bias_scr):
    @pl.when(pl.program_id(0) == 0)
    def _():
        kk = lax.broadcasted_iota(jnp.int32, (SWA_WIN, QBT), 0)
        qq = lax.broadcasted_iota(jnp.int32, (SWA_WIN, QBT), 1)
        dist = jnp.abs(kk - SWA_HALF_WINDOW - qq)
        in_window = dist <= SWA_HALF_WINDOW
        dist_f = dist.astype(F32)
        for h in range(SWA_Q_HEADS):
            bias_scr[h] = jnp.where(in_window, -slopes_ref[h] * dist_f * LOG2E, NEG_INF)

    def store(rows, cols):
        def put(o):
            o_ref[rows, cols] = o.astype(BF16)
        return put

    chains = []
    for j in range(SEQ // QBT):
        q0 = j * QBT
        lo = max(q0 - SWA_HALF_WINDOW, 0)
        hi = min(q0 + QBT + SWA_HALF_WINDOW, SEQ)
        first = lo - (q0 - SWA_HALF_WINDOW)
        cols = slice(q0, q0 + QBT)
        keys = slice(lo, hi)
        win = slice(first, first + hi - lo)
        for h in range(SWA_Q_HEADS):
            rows = slice(h * HEAD_DIM, (h + 1) * HEAD_DIM)
            chains.append(dict(
                half=h // SWA_GROUP, sink=sink_ref[layer, h] * LOG2E, split_values=True,
                q=lambda rows=rows, cols=cols: q_ref[rows, cols],
                k=lambda keys=keys: k_ref[keys, :],
                v=lambda keys=keys: v_ref[:, keys],
                bias=lambda h=h, win=win: bias_scr[h, win, :],
                store=store(rows, cols)))
    _attend_t(chains)


DIL_QB = 128
DIL_WIN = DIL_QB + 2 * DIL_HALF


def _dil_kernel(slopes_ref, nat_ref, *refs):
    n4, n16 = DIL_STRIDES[1], DIL_STRIDES[2]
    phase4_refs, phase16_refs = refs[:n4], refs[n4:n4 + n16]
    o_ref, bias_scr, bias16_scr, o_scr, lse_scr, out_scr = refs[n4 + n16:]
    pair = pl.program_id(1)

    @pl.when(pl.program_id(0) == 0)
    def _():
        slope = [slopes_ref[SWA_Q_HEADS + 2 * pair + h] for h in range(2)]
        qq = lax.broadcasted_iota(jnp.int32, (DIL_QB, DIL_WIN), 0)
        kk = lax.broadcasted_iota(jnp.int32, (DIL_QB, DIL_WIN), 1)
        for shift in range(3):
            dist = jnp.abs(kk - shift * DIL_HALF - qq)
            ok = dist <= DIL_HALF
            dist_f = dist.astype(F32)
            for h in range(2):
                for b, stride in enumerate(DIL_STRIDES[:2]):
                    bias_scr[pair, b, shift, h * DIL_QB:(h + 1) * DIL_QB, :] = jnp.where(
                        ok, -slope[h] * (dist_f * float(stride)) * LOG2E, NEG_INF)
        qq = lax.broadcasted_iota(jnp.int32, (QB, QB), 0)
        kk = lax.broadcasted_iota(jnp.int32, (QB, QB), 1)
        dist = jnp.abs(kk - qq)
        for h in range(2):
            bias16_scr[pair, h * QB:(h + 1) * QB, :] = jnp.where(
                dist <= DIL_HALF, -slope[h] * (dist.astype(F32) * float(DIL_STRIDES[2])) * LOG2E, NEG_INF)

    chains = []
    for b, stride in enumerate(DIL_STRIDES):
        sub = SEQ // stride
        qb = QB if sub == QB else DIL_QB
        for start in range(0, SEQ, qb):
            phase, local_q = divmod(start, sub)
            if sub == QB:
                local_k, n_keys, shift = 0, QB, None
            else:
                local_k = min(max(local_q - DIL_HALF, 0), sub - DIL_WIN)
                n_keys, shift = DIL_WIN, (local_q - local_k) // DIL_HALF
            qrows, krows = slice(local_q, local_q + qb), slice(local_k, local_k + n_keys)
            fine = DIL_STRIDES[1]
            if stride == 1:
                ref = nat_ref
                dst = qrows
            elif stride == fine:
                ref = phase4_refs[phase]
                dst = slice(start, start + qb)
            else:
                ref = phase16_refs[phase]
                coarse = stride // fine
                dst = pl.ds((phase % fine) * (SEQ // fine) + local_q * coarse + phase // fine, qb, stride=coarse)
            load = lambda part, rows, ref=ref: ref[rows, part * LANES:(part + 1) * LANES]
            chains.append(dict(b=b, shift=shift, dst=dst, qb=qb, order=start + qb,
                               q=functools.partial(load, 0, qrows),
                               k=functools.partial(load, 1, krows),
                               v=functools.partial(load, 2, krows)))
    chains.sort(key=lambda c: c["order"])

    def scores(c):
        q, kw = c["q"](), c["k"]()
        lo = _low_half(q.shape)
        zeros = jnp.zeros_like(q)
        q_both = jnp.concatenate([jnp.where(lo, q, zeros), jnp.where(lo, zeros, q)], axis=0)
        bias = bias16_scr[pair] if c["shift"] is None else bias_scr[pair, c["b"], c["shift"]]
        return _dot_nt(q_both, kw) + bias

    def softmax(c, s):
        return _softmax_block(s)

    def values(c, mp):
        m, p = mp
        vw = c["v"]()
        res = _dot(p, jnp.concatenate([vw, jnp.ones_like(vw)], axis=1))
        qb = c["qb"]
        lo = _low_half((qb, LANES))
        den = jnp.where(lo, res[0:qb, LANES:2 * LANES], res[qb:2 * qb, LANES:2 * LANES])
        o_scr[c["b"], c["dst"], :] = jnp.where(lo, res[0:qb, 0:LANES], res[qb:2 * qb, 0:LANES]) / den
        lse_scr[c["b"], c["dst"], :] = jnp.where(lo, m[0:qb], m[qb:2 * qb]) + jnp.log2(den)

    _pipeline3(chains, scores, softmax, values, lag=DIL_LAG)

    fine = DIL_STRIDES[1]
    per_phase = SEQ // fine

    def merge(i, carry):
        start = pl.multiple_of(i * PREP_ROWS, PREP_ROWS)
        rows = pl.ds(start, PREP_ROWS)
        natural = pl.ds((start % per_phase) * fine + start // per_phase, PREP_ROWS, stride=fine)
        lse = [lse_scr[0, natural, :], lse_scr[1, rows, :], lse_scr[2, rows, :]]
        outs = [o_scr[0, natural, :], o_scr[1, rows, :], o_scr[2, rows, :]]
        top = jnp.maximum(jnp.maximum(lse[0], lse[1]), lse[2])
        num = jnp.zeros((PREP_ROWS, LANES), F32)
        den = jnp.zeros((PREP_ROWS, LANES), F32)
        for b in range(3):
            w = jnp.exp2(lse[b] - top)
            num = num + w * outs[b]
            den = den + w
        out_scr[natural, :] = num / den
        return carry

    lax.fori_loop(0, SEQ // PREP_ROWS, merge, 0)

    def emit(i, carry):
        rows = pl.ds(pl.multiple_of(i * PREP_ROWS, PREP_ROWS), PREP_ROWS)
        o_ref[rows, :] = out_scr[rows, :].astype(BF16)
        return carry

    lax.fori_loop(0, SEQ // PREP_ROWS, emit, 0)


def _params(semantics):
    return pltpu.CompilerParams(dimension_semantics=semantics, vmem_limit_bytes=VMEM_LIMIT)


def _resident(shape, layer):
    return pl.BlockSpec((None,) + tuple(shape), lambda *_: (layer,) + (0,) * len(shape),
                        pipeline_mode=pl.Buffered(1))


def _smem():
    return pl.BlockSpec(memory_space=pltpu.SMEM)


def _proj(x, layer, gain, w_t, w_r, gain_q, gain_r):
    t = x.shape[0]
    rows = PROJ_ROW_TILE
    feat = lambda n: pl.BlockSpec((n, rows), lambda i: (0, i))
    tok = lambda n: pl.BlockSpec((rows, n), lambda i: (i, 0))
    return pl.pallas_call(
        _proj_kernel,
        out_shape=[jax.ShapeDtypeStruct((NA_W, t), BF16), jax.ShapeDtypeStruct((SWA_QW, t), BF16),
                   jax.ShapeDtypeStruct((NA_W, t), BF16), jax.ShapeDtypeStruct((SWA_KVW, t), BF16),
                   jax.ShapeDtypeStruct((t, PROJ_R_K), BF16), jax.ShapeDtypeStruct((t, 3 * DIL_W), BF16)],
        grid=(t // rows,),
        in_specs=[tok(D_MODEL), _resident((1, D_MODEL), layer),
                  _resident((PROJ_T_Q + PROJ_T_V, D_MODEL), layer), _resident((D_MODEL, PROJ_R), layer),
                  _resident((PROJ_T_Q, 1), layer), _resident((1, PROJ_R), layer)],
        out_specs=[feat(NA_W), feat(SWA_QW), feat(NA_W), feat(SWA_KVW), tok(PROJ_R_K), tok(3 * DIL_W)],
        compiler_params=_params(("parallel",)),
        name="proj",
    )(x, gain, w_t, w_r, gain_q, gain_r)


def _na(q_t, keys, v_t, layer, table):
    t = q_t.shape[1]
    feat = pl.BlockSpec((NA_W, SEQ), lambda b: (0, b))
    return pl.pallas_call(
        _na_kernel,
        out_shape=jax.ShapeDtypeStruct((NA_W, t), BF16),
        grid=(t // SEQ,),
        in_specs=[feat, pl.BlockSpec((SEQ, NA_W), lambda b: (b, 0)), feat,
                  _resident((NA_HEADS, NA_ROW_REL + 1, GRID_W, 2 * GRID_W), layer)],
        out_specs=feat,
        scratch_shapes=[pltpu.VMEM((NA_HEADS, NA_VARIANTS, NA_WIN, QBT), F32)],
        compiler_params=_params(("arbitrary",)),
        name="na",
    )(q_t, keys, v_t, table)


def _swa(q_t, keys, v_t, layer, slopes, sink):
    t = q_t.shape[1]
    return pl.pallas_call(
        functools.partial(_swa_kernel, layer),
        out_shape=jax.ShapeDtypeStruct((SWA_QW, t), BF16),
        grid=(t // SEQ,),
        in_specs=[_smem(), _smem(),
                  pl.BlockSpec((SWA_QW, SEQ), lambda b: (0, b)),
                  pl.BlockSpec((SEQ, SWA_KVW), lambda b: (b, NA_W // SWA_KVW)),
                  pl.BlockSpec((SWA_KVW, SEQ), lambda b: (0, b))],
        out_specs=pl.BlockSpec((SWA_QW, SEQ), lambda b: (0, b)),
        scratch_shapes=[pltpu.VMEM((SWA_Q_HEADS, SWA_WIN, QBT), F32)],
        compiler_params=_params(("arbitrary",)),
        name="swa",
    )(slopes, sink, q_t, keys, v_t)


def _dil(qkv, slopes):
    t = qkv.shape[0]
    pairs = DIL_W // LANES
    pair_w = 3 * LANES
    operands = [qkv]
    specs = [pl.BlockSpec((SEQ, pair_w), lambda b, p: (b, p))]
    for stride in DIL_STRIDES[1:]:
        view = qkv.reshape(t // SEQ, SEQ // stride, stride * 3 * DIL_W)
        for phase in range(stride):
            operands.append(view)
            specs.append(pl.BlockSpec((None, SEQ // stride, pair_w),
                                      lambda b, p, phase=phase: (b, 0, phase * pairs + p)))
    return pl.pallas_call(
        _dil_kernel,
        out_shape=jax.ShapeDtypeStruct((t, DIL_W), BF16),
        grid=(t // SEQ, pairs),
        in_specs=[_smem()] + specs,
        out_specs=pl.BlockSpec((SEQ, LANES), lambda b, p: (b, p)),
        scratch_shapes=[pltpu.VMEM((pairs, 2, 3, 2 * DIL_QB, DIL_WIN), F32),
                        pltpu.VMEM((pairs, 2 * QB, QB), F32),
                        pltpu.VMEM((3, SEQ, LANES), F32),
                        pltpu.VMEM((3, SEQ, LANES), F32),
                        pltpu.VMEM((SEQ, LANES), F32)],
        compiler_params=_params(("arbitrary", "arbitrary")),
        name="dil",
    )(slopes, *operands)


def _out_ffn(ya_t, yb_t, yc, x, layer, gain_ab, gain_c, w_out, gain_ffn, w_gu, w_down):
    t = x.shape[0]
    feat = lambda n: pl.BlockSpec((n, ROW_TILE), lambda i: (0, i))
    tok = lambda n: pl.BlockSpec((ROW_TILE, n), lambda i: (i, 0))
    return pl.pallas_call(
        _out_ffn_kernel,
        out_shape=jax.ShapeDtypeStruct((t, D_MODEL), F32),
        grid=(t // ROW_TILE,),
        in_specs=[feat(NA_W), feat(SWA_QW), tok(DIL_W), tok(D_MODEL),
                  _resident((NA_W + SWA_QW, 1), layer), _resident((1, DIL_W), layer),
                  _resident((MIX_W, D_MODEL), layer), _resident((1, D_MODEL), layer),
                  _resident((D_MODEL, 2 * FFN_HIDDEN), layer), _resident((FFN_HIDDEN, D_MODEL), layer)],
        out_specs=tok(D_MODEL),
        scratch_shapes=[pltpu.VMEM((ROW_TILE, D_MODEL), BF16)],
        compiler_params=_params(("parallel",)),
        name="out_ffn",
    )(ya_t, yb_t, yc, x, gain_ab, gain_c, w_out, gain_ffn, w_gu, w_down)


def _split_w_in(w):
    cuts = np.cumsum([NA_W, NA_W, NA_W, SWA_QW, SWA_KVW, SWA_KVW, DIL_W, DIL_W])
    qa, ka, va, qb, kb, vb, qc, kc, vc = jnp.split(w, [int(c) for c in cuts], axis=2)
    w_t = jnp.concatenate([qa, qb, va, vb], axis=2).astype(BF16).transpose(0, 2, 1)
    pair_major = [part[:, :, p * LANES:(p + 1) * LANES] for p in range(DIL_W // LANES) for part in (qc, kc, vc)]
    w_r = jnp.concatenate([ka, kb] + pair_major, axis=2).astype(BF16)
    return w_t, w_r


def kernel(x, attn_norm, w_in, qk_gain, rpb, sink, out_gain, w_out, ffn_norm, w_gu, w_down):
    batch, seq, d = x.shape
    assert (seq, d) == (SEQ, D_MODEL)
    depth = w_in.shape[0]
    n_slopes = SWA_Q_HEADS + DIL_HEADS
    slopes = 2.0 ** (-8.0 * (jnp.arange(n_slopes, dtype=F32) + 1.0) / n_slopes)
    xf = x.reshape(batch * seq, d)

    g = qk_gain.astype(F32)
    tile = lambda v, n: jnp.tile(v, (1, n))
    gain_q = jnp.concatenate([tile(g[:, 0, 0], NA_HEADS), tile(g[:, 1, 0], SWA_Q_HEADS)], axis=1)[:, :, None]
    dil_pair = jnp.concatenate([tile(g[:, 2, 0], 2), tile(g[:, 2, 1], 2), jnp.ones((depth, LANES), F32)], axis=1)
    gain_r = jnp.concatenate([tile(g[:, 0, 1], NA_HEADS), tile(g[:, 1, 1], SWA_KV_HEADS),
                              tile(dil_pair, DIL_W // LANES)], axis=1)[:, None, :]
    w_t, w_r = _split_w_in(w_in)
    table = _na_pair_table(rpb.reshape((depth * NA_HEADS,) + rpb.shape[2:]))
    table = table.reshape((depth, NA_HEADS) + table.shape[1:])
    og = out_gain.astype(F32)
    gain_ab, gain_c = og[:, :NA_W + SWA_QW, None], og[:, None, NA_W + SWA_QW:]
    w_out_b, w_gu_b, w_down_b = w_out.astype(BF16), w_gu.astype(BF16), w_down.astype(BF16)
    attn_gain, ffn_gain = attn_norm.astype(F32)[:, None, :], ffn_norm.astype(F32)[:, None, :]
    sink = sink.astype(F32)

    for l in range(depth):
        qa_t, qb_t, va_t, vb_t, keys, dil_qkv = _proj(xf, l, attn_gain, w_t, w_r, gain_q, gain_r)
        ya_t = _na(qa_t, keys, va_t, l, table)
        yb_t = _swa(qb_t, keys, vb_t, l, slopes, sink)
        yc = _dil(dil_qkv, slopes)
        xf = _out_ffn(ya_t, yb_t, yc, xf, l, gain_ab, gain_c, w_out_b, ffn_gain, w_gu_b, w_down_b)
    return xf.reshape(batch, seq, d)
```

```python
import functools
import math

import numpy as np
import jax
import jax.numpy as jnp
from jax import lax
from jax.experimental import pallas as pl
from jax.experimental.pallas import tpu as pltpu

D_MODEL = 1024
SEQ = 2048
HEAD_DIM = 64
GRID_W = 64
NA_HEADS = 4
NA_ROWS = 8
NA_COLS = 16
SWA_Q_HEADS = 6
SWA_KV_HEADS = 2
SWA_GROUP = SWA_Q_HEADS // SWA_KV_HEADS
SWA_HALF_WINDOW = 128
DIL_HEADS = 6
DIL_STRIDES = (1, 4, 16)
DIL_HALF = 64
NA_W = NA_HEADS * HEAD_DIM
SWA_QW = SWA_Q_HEADS * HEAD_DIM
SWA_KVW = SWA_KV_HEADS * HEAD_DIM
DIL_W = DIL_HEADS * HEAD_DIM
IN_W = 3 * NA_W + SWA_QW + 2 * SWA_KVW + 3 * DIL_W
MIX_W = NA_W + SWA_QW + DIL_W
FFN_HIDDEN = 2816
NORM_EPS = 1e-6
NEG_INF = -1e30
QK_SCALE = HEAD_DIM ** -0.5
LOG2E = math.log2(math.e)
Q_SCALE = QK_SCALE * LOG2E

LANES = 128
VMEM_LIMIT = 56 * 1024 * 1024

QB = 128
QBT = 256
PREP_ROWS = 256
NQB = SEQ // QB
ROW_TILE = 1024
PROJ_ROW_TILE = 1024
MXU_DIM = 256
FFN_H_CHUNKS = (3 * MXU_DIM,) * 3 + (2 * MXU_DIM,)
assert sum(FFN_H_CHUNKS) == FFN_HIDDEN
W_CHUNK_ROWS = 128
NA_LAG = 2
DIL_LAG = 1

PROJ_T_Q = NA_W + SWA_QW
PROJ_T_V = NA_W + SWA_KVW
PROJ_R_K = NA_W + SWA_KVW
PROJ_R = PROJ_R_K + 3 * DIL_W
PROJ_R_CHUNK = 768

NA_WIN = 768
NA_VARIANTS = 3
NA_LAST_START = SEQ - NA_WIN
SWA_WIN = QBT + 2 * SWA_HALF_WINDOW

F32 = jnp.float32
BF16 = jnp.bfloat16


def _dot(a, b):
    return jnp.dot(a, b, preferred_element_type=F32)


def _dot_nt(a, b):
    return lax.dot_general(a, b, (((1,), (1,)), ((), ())), preferred_element_type=F32)


def _dot_tn(a, b):
    return lax.dot_general(a, b, (((0,), (0,)), ((), ())), preferred_element_type=F32)


def _low_half(shape):
    return lax.broadcasted_iota(jnp.int32, shape, len(shape) - 1) < HEAD_DIM


def _pair_rms(x, gain):
    lo = _low_half(x.shape)
    sq = x * x
    s_lo = jnp.sum(jnp.where(lo, sq, 0.0), axis=-1, keepdims=True)
    s_hi = jnp.sum(jnp.where(lo, 0.0, sq), axis=-1, keepdims=True)
    ms = jnp.where(lo, s_lo, s_hi) * (1.0 / HEAD_DIM)
    return x * lax.rsqrt(ms + NORM_EPS) * gain


def _softmax_block(s):
    m = jnp.max(s, axis=-1, keepdims=True)
    return m, jnp.exp2(s - m).astype(BF16)


def _proj_kernel(x_ref, g_ref, wt_ref, wr_ref, gq_ref, gr_ref,
                 qa_ref, qb_ref, va_ref, vb_ref, k_ref, dil_ref, dil4_ref, dil16_ref, regroup_scr):
    x = x_ref[...]
    ms = jnp.mean(x * x, axis=-1, keepdims=True)
    h = (x * lax.rsqrt(ms + NORM_EPS) * g_ref[...]).astype(BF16)
    tm = h.shape[0]

    def finish_q(qt):
        q3 = qt.reshape(PROJ_T_Q // HEAD_DIM, HEAD_DIM, tm)
        q3 = q3 * lax.rsqrt(jnp.mean(q3 * q3, axis=1, keepdims=True) + NORM_EPS)
        qt = q3.reshape(PROJ_T_Q, tm) * gq_ref[...] * Q_SCALE
        qa_ref[...] = qt[0:NA_W].astype(BF16)
        qb_ref[...] = qt[NA_W:PROJ_T_Q].astype(BF16)

    def finish_v(vt):
        va_ref[...] = vt[0:NA_W].astype(BF16)
        vb_ref[...] = vt[NA_W:PROJ_T_V].astype(BF16)

    n_norm = (PROJ_R_K + 2 * DIL_W) // LANES
    n_key = PROJ_R_K // LANES
    n_dil_q = DIL_W // LANES

    def finish_r(c, r):
        for t in range(PROJ_R_CHUNK // LANES):
            blk = c * (PROJ_R_CHUNK // LANES) + t
            tile = r[:, t * LANES:(t + 1) * LANES]
            if blk < n_norm:
                tile = _pair_rms(tile, gr_ref[:, blk * LANES:(blk + 1) * LANES])
            if n_key <= blk < n_key + n_dil_q:
                tile = tile * Q_SCALE
            if blk < n_key:
                k_ref[:, blk * LANES:(blk + 1) * LANES] = tile.astype(BF16)
                continue
            d = blk - n_key
            cols = slice(d * LANES, (d + 1) * LANES)
            dil_ref[:, cols] = tile.astype(BF16)
            fine, coarse = DIL_STRIDES[1], DIL_STRIDES[2] // DIL_STRIDES[1]
            per_fine = tm // fine
            regroup_scr[d, 0] = tile
            for ph in range(fine):
                part = regroup_scr[d, 0, pl.ds(ph, per_fine, stride=fine), :]
                dil4_ref[0, ph, :, cols] = part.astype(BF16)
                regroup_scr[d, 1, ph * per_fine:(ph + 1) * per_fine, :] = part
            for ph in range(fine):
                for sub_ph in range(coarse):
                    part = regroup_scr[d, 1, pl.ds(ph * per_fine + sub_ph, per_fine // coarse, stride=coarse), :]
                    dil16_ref[0, ph + fine * sub_ph, :, cols] = part.astype(BF16)

    finish_q(_dot_nt(wt_ref[0:PROJ_T_Q, :], h))
    finish_v(_dot_nt(wt_ref[PROJ_T_Q:PROJ_T_Q + PROJ_T_V, :], h))
    for c in range(PROJ_R // PROJ_R_CHUNK):
        finish_r(c, _dot(h, wr_ref[:, c * PROJ_R_CHUNK:(c + 1) * PROJ_R_CHUNK]))


def _group_rms_t(yt, gain_col):
    yf = yt.astype(F32)
    ms = jnp.mean(yf * yf, axis=0, keepdims=True)
    return (yf * lax.rsqrt(ms + NORM_EPS) * gain_col).astype(BF16)


def _load_bf16(layer, src_hbm, dst_scr, stage_scr, sems):
    rows, cols = dst_scr.shape
    chunk_rows = W_CHUNK_ROWS
    n_chunks = rows // chunk_rows
    assert n_chunks * chunk_rows == rows

    def copy(c):
        slot = c % 2
        return pltpu.make_async_copy(src_hbm.at[layer, pl.ds(c * chunk_rows, chunk_rows), :],
                                     stage_scr.at[slot, pl.ds(0, chunk_rows), pl.ds(0, cols)], sems.at[slot])

    copy(0).start()
    for c in range(n_chunks):
        if c + 1 < n_chunks:
            copy(c + 1).start()
        copy(c).wait()
        dst_scr[c * chunk_rows:(c + 1) * chunk_rows, :] = stage_scr[c % 2, 0:chunk_rows, 0:cols].astype(BF16)


def _out_ffn_kernel(layer, ya_ref, yb_ref, yc_ref, x_ref, gab_ref, gc_ref, gf_ref, wo_hbm, wgu_hbm, wd_hbm,
                    o_ref, h_scr, wo_scr, wgu_scr, wd_scr, stage_scr, sems):
    @pl.when(pl.program_id(0) == 0)
    def _():
        _load_bf16(layer, wo_hbm, wo_scr, stage_scr, sems)
        _load_bf16(layer, wgu_hbm, wgu_scr, stage_scr, sems)
        _load_bf16(layer, wd_hbm, wd_scr, stage_scr, sems)

    ya = _group_rms_t(ya_ref[...], gab_ref[0:NA_W, :])
    yb = _group_rms_t(yb_ref[...], gab_ref[NA_W:NA_W + SWA_QW, :])
    yc = yc_ref[...].astype(F32)
    ms = jnp.mean(yc * yc, axis=-1, keepdims=True)
    yc = (yc * lax.rsqrt(ms + NORM_EPS) * gc_ref[...]).astype(BF16)
    acc = _dot_tn(ya, wo_scr[0:NA_W, :])
    acc = acc + _dot_tn(yb, wo_scr[NA_W:NA_W + SWA_QW, :])
    acc = acc + _dot(yc, wo_scr[NA_W + SWA_QW:MIX_W, :])
    x = x_ref[...] + acc

    ms = jnp.mean(x * x, axis=-1, keepdims=True)
    h_scr[...] = (x * lax.rsqrt(ms + NORM_EPS) * gf_ref[...]).astype(BF16)
    o_ref[...] = x
    lo = 0
    for width in FFN_H_CHUNKS:
        gate = _dot(h_scr[...], wgu_scr[:, lo:lo + width])
        up = _dot(h_scr[...], wgu_scr[:, FFN_HIDDEN + lo:FFN_HIDDEN + lo + width])
        act = (gate / (1.0 + jnp.exp(-gate)) * up).astype(BF16)
        o_ref[...] += _dot(act, wd_scr[lo:lo + width, :])
        lo += width


def _pipeline3(chains, stage_a, stage_b, stage_c, lag=1):
    n = len(chains)
    a, b = {}, {}
    for i in range(-2 * lag, n):
        if 0 <= i + 2 * lag < n:
            a[i + 2 * lag] = stage_a(chains[i + 2 * lag])
        if 0 <= i + lag < n:
            b[i + lag] = stage_b(chains[i + lag], a.pop(i + lag))
        if i >= 0:
            stage_c(chains[i], b.pop(i))


def _attend_t(chains, lag=1):
    def scores(c):
        qt = c["q"]()
        zeros = jnp.zeros_like(qt)
        qpad = jnp.concatenate([qt, zeros] if c["half"] == 0 else [zeros, qt], axis=0)
        kw = c["k"]()
        mid = kw.shape[0] // 2
        s = jnp.concatenate([_dot(kw[:mid], qpad), _dot(kw[mid:], qpad)], axis=0)
        return s + c["bias"]()

    def softmax(c, s):
        m = jnp.max(s, axis=0, keepdims=True)
        if c["sink"] is not None:
            m = jnp.maximum(m, c["sink"])
        return jnp.exp2(s - m).astype(BF16), m

    def values(c, pm):
        p, m = pm
        vt = c["v"]()
        own = vt[c["half"] * HEAD_DIM:(c["half"] + 1) * HEAD_DIM, :]
        ones = jnp.ones_like(own)
        v_aug = jnp.concatenate([own, ones], axis=0)
        if c["split_values"]:
            cut = max(MXU_DIM, (p.shape[0] // 2) // MXU_DIM * MXU_DIM)
            o = _dot(v_aug[:, :cut], p[:cut]) + _dot(v_aug[:, cut:], p[cut:])
        else:
            o = _dot(v_aug, p)
        den = o[HEAD_DIM:2 * HEAD_DIM, :]
        if c["sink"] is not None:
            den = den + jnp.exp2(c["sink"] - m)
        c["store"](o[0:HEAD_DIM, :] / den)

    _pipeline3(chains, scores, softmax, values, lag=lag)


NA_KEY_ROWS = NA_WIN // GRID_W
NA_Q_ROWS = QBT // GRID_W
NA_ROW_REL = 2 * NA_ROWS - 1


def _na_geometry():
    n_blocks = SEQ // QBT
    out = []
    for variant, block in enumerate((0, 1, n_blocks - 1)):
        start = min(max(block * QBT - QBT, 0), NA_LAST_START)
        assert (block * QBT - start) // QBT == variant
        out.append((start // GRID_W, block * NA_Q_ROWS))
    return out


def _na_kernel(q_ref, k_ref, v_ref, pair_ref, o_ref, tbl_scr):
    @pl.when(pl.program_id(0) == 0)
    def _():
        grid_rows = SEQ // GRID_W
        lo = _low_half((GRID_W, LANES))
        masked = jnp.full((GRID_W, LANES), NEG_INF, F32)
        for variant, (key_row0, query_row0) in enumerate(_na_geometry()):
            for kr in range(NA_KEY_ROWS):
                for pair in range(NA_Q_ROWS // 2):
                    krow = key_row0 + kr
                    rel, ok = [], []
                    for qrow in (query_row0 + 2 * pair, query_row0 + 2 * pair + 1):
                        first = min(max(qrow - NA_ROWS // 2, 0), grid_rows - NA_ROWS)
                        ok.append(first <= krow < first + NA_ROWS)
                        rel.append(krow - qrow + NA_ROWS - 1)
                    for h in range(NA_HEADS):
                        if not (ok[0] or ok[1]):
                            tile = masked
                        else:
                            tile = pair_ref[h, rel[0]]
                            if not ok[1]:
                                tile = jnp.where(lo, tile, masked)
                            elif not ok[0]:
                                tile = jnp.where(lo, masked, tile)
                        tbl_scr[h, variant, kr * GRID_W:(kr + 1) * GRID_W, pair * LANES:(pair + 1) * LANES] = tile

    def store(rows, cols):
        def put(o):
            o_ref[rows, cols] = o.astype(BF16)
        return put

    chains = []
    for j in range(SEQ // QBT):
        q0 = j * QBT
        start = min(max(q0 - QBT, 0), NA_LAST_START)
        variant = (q0 - start) // QBT
        cols = slice(q0, q0 + QBT)
        keys = slice(start, start + NA_WIN)
        for h in range(NA_HEADS):
            rows = slice(h * HEAD_DIM, (h + 1) * HEAD_DIM)
            pair = slice((h // 2) * LANES, (h // 2 + 1) * LANES)
            chains.append(dict(
                half=h % 2, sink=None, split_values=False,
                q=lambda rows=rows, cols=cols: q_ref[rows, cols],
                k=lambda keys=keys, pair=pair: k_ref[keys, pair],
                v=lambda keys=keys, pair=pair: v_ref[pair, keys],
                bias=lambda h=h, variant=variant: tbl_scr[h, variant],
                store=store(rows, cols)))
    _attend_t(chains, lag=NA_LAG)


def _na_pair_table(rpb):
    cols = np.arange(GRID_W)
    col_start = np.clip(cols - NA_COLS // 2, 0, GRID_W - NA_COLS)
    col_ok = (cols[:, None] >= col_start[None, :]) & (cols[:, None] < col_start[None, :] + NA_COLS)
    heads, n_row_rel, n_col_rel = rpb.shape
    assert n_row_rel == NA_ROW_REL
    period = 2 * GRID_W - 1
    lead = GRID_W - NA_COLS
    seq = jnp.pad(rpb.astype(F32)[:, :, ::-1], ((0, 0), (0, 0), (lead, period - lead - n_col_rel)))
    skew = jnp.tile(seq, (1, 1, GRID_W + 1))[..., :GRID_W * 2 * GRID_W]
    toe = skew.reshape(heads, n_row_rel, GRID_W, 2 * GRID_W)[:, :, ::-1, :GRID_W]
    toe = jnp.where(col_ok[None, None], toe * LOG2E, NEG_INF)
    toe = jnp.pad(toe, ((0, 0), (1, 1), (0, 0), (0, 0)), constant_values=NEG_INF)
    return jnp.concatenate([toe[:, 1:], toe[:, :-1]], axis=-1)


def _swa_kernel(layer, slopes_ref, sink_ref, q_ref, k_ref, v_ref, o_ref, bias_scr):
    @pl.when(pl.program_id(0) == 0)
    def _():
        kk = lax.broadcasted_iota(jnp.int32, (SWA_WIN, QBT), 0)
        qq = lax.broadcasted_iota(jnp.int32, (SWA_WIN, QBT), 1)
        dist = jnp.abs(kk - SWA_HALF_WINDOW - qq)
        in_window = dist <= SWA_HALF_WINDOW
        dist_f = dist.astype(F32)
        for h in range(SWA_Q_HEADS):
            bias_scr[h] = jnp.where(in_window, -slopes_ref[h] * dist_f * LOG2E, NEG_INF)

    def store(rows, cols):
        def put(o):
            o_ref[rows, cols] = o.astype(BF16)
        return put

    chains = []
    for j in range(SEQ // QBT):
        q0 = j * QBT
        lo = max(q0 - SWA_HALF_WINDOW, 0)
        hi = min(q0 + QBT + SWA_HALF_WINDOW, SEQ)
        first = lo - (q0 - SWA_HALF_WINDOW)
        cols = slice(q0, q0 + QBT)
        keys = slice(lo, hi)
        win = slice(first, first + hi - lo)
        for h in range(SWA_Q_HEADS):
            rows = slice(h * HEAD_DIM, (h + 1) * HEAD_DIM)
            chains.append(dict(
                half=h // SWA_GROUP, sink=sink_ref[layer, h] * LOG2E, split_values=True,
                q=lambda rows=rows, cols=cols: q_ref[rows, cols],
                k=lambda keys=keys: k_ref[keys, :],
                v=lambda keys=keys: v_ref[:, keys],
                bias=lambda h=h, win=win: bias_scr[h, win, :],
                store=store(rows, cols)))
    _attend_t(chains)


DIL_QB = 128
DIL_WIN = DIL_QB + 2 * DIL_HALF


def _dil_kernel(slopes_ref, q1_ref, k1_ref, v1_ref, q4_ref, k4_ref, v4_ref, q16_ref, k16_ref, v16_ref,
                o_ref, bias_scr, bias16_scr, o_scr, lse_scr, out_scr):
    pair = pl.program_id(1)

    @pl.when(pl.program_id(0) == 0)
    def _():
        slope = [slopes_ref[SWA_Q_HEADS + 2 * pair + h] for h in range(2)]
        qq = lax.broadcasted_iota(jnp.int32, (DIL_QB, DIL_WIN), 0)
        kk = lax.broadcasted_iota(jnp.int32, (DIL_QB, DIL_WIN), 1)
        for shift in range(3):
            dist = jnp.abs(kk - shift * DIL_HALF - qq)
            ok = dist <= DIL_HALF
            dist_f = dist.astype(F32)
            for h in range(2):
                for b, stride in enumerate(DIL_STRIDES[:2]):
                    bias_scr[pair, b, shift, h * DIL_QB:(h + 1) * DIL_QB, :] = jnp.where(
                        ok, -slope[h] * (dist_f * float(stride)) * LOG2E, NEG_INF)
        qq = lax.broadcasted_iota(jnp.int32, (QB, QB), 0)
        kk = lax.broadcasted_iota(jnp.int32, (QB, QB), 1)
        dist = jnp.abs(kk - qq)
        for h in range(2):
            bias16_scr[pair, h * QB:(h + 1) * QB, :] = jnp.where(
                dist <= DIL_HALF, -slope[h] * (dist.astype(F32) * float(DIL_STRIDES[2])) * LOG2E, NEG_INF)

    chains = []
    for b, stride in enumerate(DIL_STRIDES):
        sub = SEQ // stride
        qb = QB if sub == QB else DIL_QB
        for start in range(0, SEQ, qb):
            phase, local_q = divmod(start, sub)
            if sub == QB:
                local_k, n_keys, shift = 0, QB, None
            else:
                local_k = min(max(local_q - DIL_HALF, 0), sub - DIL_WIN)
                n_keys, shift = DIL_WIN, (local_q - local_k) // DIL_HALF
            qrows, krows = slice(local_q, local_q + qb), slice(local_k, local_k + n_keys)
            fine = DIL_STRIDES[1]
            if stride == 1:
                refs = (q1_ref, k1_ref, v1_ref)
                load = lambda ref, rows: ref[rows, :]
                dst = qrows
            elif stride == fine:
                refs = (q4_ref, k4_ref, v4_ref)
                load = lambda ref, rows, phase=phase: ref[0, phase, rows, :]
                dst = slice(start, start + qb)
            else:
                refs = (q16_ref, k16_ref, v16_ref)
                load = lambda ref, rows, phase=phase: ref[0, phase, rows, :]
                coarse = stride // fine
                dst = pl.ds((phase % fine) * (SEQ // fine) + local_q * coarse + phase // fine, qb, stride=coarse)
            chains.append(dict(b=b, shift=shift, dst=dst, qb=qb, order=start + qb,
                               q=lambda refs=refs, load=load, rows=qrows: load(refs[0], rows),
                               k=lambda refs=refs, load=load, rows=krows: load(refs[1], rows),
                               v=lambda refs=refs, load=load, rows=krows: load(refs[2], rows)))
    chains.sort(key=lambda c: c["order"])

    def scores(c):
        q, kw = c["q"](), c["k"]()
        lo = _low_half(q.shape)
        zeros = jnp.zeros_like(q)
        q_both = jnp.concatenate([jnp.where(lo, q, zeros), jnp.where(lo, zeros, q)], axis=0)
        bias = bias16_scr[pair] if c["shift"] is None else bias_scr[pair, c["b"], c["shift"]]
        return _dot_nt(q_both, kw) + bias

    def softmax(c, s):
        return _softmax_block(s)

    def values(c, mp):
        m, p = mp
        vw = c["v"]()
        res = _dot(p, jnp.concatenate([vw, jnp.ones_like(vw)], axis=1))
        qb = c["qb"]
        lo = _low_half((qb, LANES))
        den = jnp.where(lo, res[0:qb, LANES:2 * LANES], res[qb:2 * qb, LANES:2 * LANES])
        o_scr[c["b"], c["dst"], :] = jnp.where(lo, res[0:qb, 0:LANES], res[qb:2 * qb, 0:LANES]) / den
        lse_scr[c["b"], c["dst"], :] = jnp.where(lo, m[0:qb], m[qb:2 * qb]) + jnp.log2(den)

    _pipeline3(chains, scores, softmax, values, lag=DIL_LAG)

    fine = DIL_STRIDES[1]
    per_phase = SEQ // fine

    def merge(i, carry):
        start = pl.multiple_of(i * PREP_ROWS, PREP_ROWS)
        rows = pl.ds(start, PREP_ROWS)
        natural = pl.ds((start % per_phase) * fine + start // per_phase, PREP_ROWS, stride=fine)
        lse = [lse_scr[0, natural, :], lse_scr[1, rows, :], lse_scr[2, rows, :]]
        outs = [o_scr[0, natural, :], o_scr[1, rows, :], o_scr[2, rows, :]]
        top = jnp.maximum(jnp.maximum(lse[0], lse[1]), lse[2])
        num = jnp.zeros((PREP_ROWS, LANES), F32)
        den = jnp.zeros((PREP_ROWS, LANES), F32)
        for b in range(3):
            w = jnp.exp2(lse[b] - top)
            num = num + w * outs[b]
            den = den + w
        out_scr[natural, :] = num / den
        return carry

    lax.fori_loop(0, SEQ // PREP_ROWS, merge, 0)

    def emit(i, carry):
        rows = pl.ds(pl.multiple_of(i * PREP_ROWS, PREP_ROWS), PREP_ROWS)
        o_ref[rows, :] = out_scr[rows, :].astype(BF16)
        return carry

    lax.fori_loop(0, SEQ // PREP_ROWS, emit, 0)


def _params(semantics):
    return pltpu.CompilerParams(dimension_semantics=semantics, vmem_limit_bytes=VMEM_LIMIT)


def _resident(shape, layer):
    return pl.BlockSpec((None,) + tuple(shape), lambda *_: (layer,) + (0,) * len(shape),
                        pipeline_mode=pl.Buffered(1))


def _smem():
    return pl.BlockSpec(memory_space=pltpu.SMEM)


def _proj(x, layer, gain, w_t, w_r, gain_q, gain_r):
    t = x.shape[0]
    rows = PROJ_ROW_TILE
    feat = lambda n: pl.BlockSpec((n, rows), lambda i: (0, i))
    tok = lambda n: pl.BlockSpec((rows, n), lambda i: (i, 0))
    tiles_per_seq = SEQ // rows

    def phased(stride):
        return pl.BlockSpec((1, stride, rows // stride, 3 * DIL_W),
                            lambda i: (i // tiles_per_seq, 0, i % tiles_per_seq, 0))

    def phased_shape(stride):
        return jax.ShapeDtypeStruct((t // SEQ, stride, SEQ // stride, 3 * DIL_W), BF16)

    return pl.pallas_call(
        _proj_kernel,
        out_shape=[jax.ShapeDtypeStruct((NA_W, t), BF16), jax.ShapeDtypeStruct((SWA_QW, t), BF16),
                   jax.ShapeDtypeStruct((NA_W, t), BF16), jax.ShapeDtypeStruct((SWA_KVW, t), BF16),
                   jax.ShapeDtypeStruct((t, PROJ_R_K), BF16), jax.ShapeDtypeStruct((t, 3 * DIL_W), BF16),
                   phased_shape(DIL_STRIDES[1]), phased_shape(DIL_STRIDES[2])],
        grid=(t // rows,),
        in_specs=[tok(D_MODEL), _resident((1, D_MODEL), layer),
                  _resident((PROJ_T_Q + PROJ_T_V, D_MODEL), layer), _resident((D_MODEL, PROJ_R), layer),
                  _resident((PROJ_T_Q, 1), layer), _resident((1, PROJ_R_K + 2 * DIL_W), layer)],
        out_specs=[feat(NA_W), feat(SWA_QW), feat(NA_W), feat(SWA_KVW), tok(PROJ_R_K), tok(3 * DIL_W),
                   phased(DIL_STRIDES[1]), phased(DIL_STRIDES[2])],
        scratch_shapes=[pltpu.VMEM((3 * DIL_W // LANES, 2, rows, LANES), F32)],
        compiler_params=_params(("parallel",)),
        name="proj",
    )(x, gain, w_t, w_r, gain_q, gain_r)


def _na(q_t, keys, v_t, layer, table):
    t = q_t.shape[1]
    feat = pl.BlockSpec((NA_W, SEQ), lambda b: (0, b))
    return pl.pallas_call(
        _na_kernel,
        out_shape=jax.ShapeDtypeStruct((NA_W, t), BF16),
        grid=(t // SEQ,),
        in_specs=[feat, pl.BlockSpec((SEQ, NA_W), lambda b: (b, 0)), feat,
                  _resident((NA_HEADS, NA_ROW_REL + 1, GRID_W, 2 * GRID_W), layer)],
        out_specs=feat,
        scratch_shapes=[pltpu.VMEM((NA_HEADS, NA_VARIANTS, NA_WIN, QBT), F32)],
        compiler_params=_params(("arbitrary",)),
        name="na",
    )(q_t, keys, v_t, table)


def _swa(q_t, keys, v_t, layer, slopes, sink):
    t = q_t.shape[1]
    return pl.pallas_call(
        functools.partial(_swa_kernel, layer),
        out_shape=jax.ShapeDtypeStruct((SWA_QW, t), BF16),
        grid=(t // SEQ,),
        in_specs=[_smem(), _smem(),
                  pl.BlockSpec((SWA_QW, SEQ), lambda b: (0, b)),
                  pl.BlockSpec((SEQ, SWA_KVW), lambda b: (b, NA_W // SWA_KVW)),
                  pl.BlockSpec((SWA_KVW, SEQ), lambda b: (0, b))],
        out_specs=pl.BlockSpec((SWA_QW, SEQ), lambda b: (0, b)),
        scratch_shapes=[pltpu.VMEM((SWA_Q_HEADS, SWA_WIN, QBT), F32)],
        compiler_params=_params(("arbitrary",)),
        name="swa",
    )(slopes, sink, q_t, keys, v_t)


def _dil(qkv, qkv4, qkv16, slopes):
    t = qkv.shape[0]
    pairs = DIL_W // LANES
    seq_block = lambda first: pl.BlockSpec((SEQ, LANES), lambda b, p: (b, first + p))

    def phased(stride, first):
        return pl.BlockSpec((1, stride, SEQ // stride, LANES), lambda b, p: (b, 0, 0, first + p))

    operands, specs = [], []
    for arr, spec in ((qkv, seq_block), (qkv4, lambda f: phased(DIL_STRIDES[1], f)),
                      (qkv16, lambda f: phased(DIL_STRIDES[2], f))):
        for part in range(3):
            operands.append(arr)
            specs.append(spec(part * pairs))
    return pl.pallas_call(
        _dil_kernel,
        out_shape=jax.ShapeDtypeStruct((t, DIL_W), BF16),
        grid=(t // SEQ, pairs),
        in_specs=[_smem()] + specs,
        out_specs=seq_block(0),
        scratch_shapes=[pltpu.VMEM((pairs, 2, 3, 2 * DIL_QB, DIL_WIN), F32),
                        pltpu.VMEM((pairs, 2 * QB, QB), F32),
                        pltpu.VMEM((3, SEQ, LANES), F32),
                        pltpu.VMEM((3, SEQ, LANES), F32),
                        pltpu.VMEM((SEQ, LANES), F32)],
        compiler_params=_params(("arbitrary", "arbitrary")),
        name="dil",
    )(slopes, *operands)


def _out_ffn(ya_t, yb_t, yc, x, layer, gain_ab, gain_c, gain_ffn, w_out, w_gu, w_down):
    t = x.shape[0]
    feat = lambda n: pl.BlockSpec((n, ROW_TILE), lambda i: (0, i))
    tok = lambda n: pl.BlockSpec((ROW_TILE, n), lambda i: (i, 0))
    hbm = pl.BlockSpec(memory_space=pl.ANY)
    return pl.pallas_call(
        functools.partial(_out_ffn_kernel, layer),
        out_shape=jax.ShapeDtypeStruct((t, D_MODEL), F32),
        grid=(t // ROW_TILE,),
        in_specs=[feat(NA_W), feat(SWA_QW), tok(DIL_W), tok(D_MODEL),
                  _resident((NA_W + SWA_QW, 1), layer), _resident((1, DIL_W), layer),
                  _resident((1, D_MODEL), layer), hbm, hbm, hbm],
        out_specs=tok(D_MODEL),
        scratch_shapes=[pltpu.VMEM((ROW_TILE, D_MODEL), BF16),
                        pltpu.VMEM((MIX_W, D_MODEL), BF16),
                        pltpu.VMEM((D_MODEL, 2 * FFN_HIDDEN), BF16),
                        pltpu.VMEM((FFN_HIDDEN, D_MODEL), BF16),
                        pltpu.VMEM((2, W_CHUNK_ROWS, 2 * FFN_HIDDEN), F32),
                        pltpu.SemaphoreType.DMA((2,))],
        compiler_params=_params(("arbitrary",)),
        name="out_ffn",
    )(ya_t, yb_t, yc, x, gain_ab, gain_c, gain_ffn, w_out, w_gu, w_down)


def _split_w_in(w):
    cuts = np.cumsum([NA_W, NA_W, NA_W, SWA_QW, SWA_KVW, SWA_KVW, DIL_W, DIL_W])
    qa, ka, va, qb, kb, vb, qc, kc, vc = jnp.split(w, [int(c) for c in cuts], axis=2)
    w_t = jnp.concatenate([qa, qb, va, vb], axis=2).astype(BF16).transpose(0, 2, 1)
    w_r = jnp.concatenate([ka, kb, qc, kc, vc], axis=2).astype(BF16)
    return w_t, w_r


def kernel(x, attn_norm, w_in, qk_gain, rpb, sink, out_gain, w_out, ffn_norm, w_gu, w_down):
    batch, seq, d = x.shape
    assert (seq, d) == (SEQ, D_MODEL)
    depth = w_in.shape[0]
    n_slopes = SWA_Q_HEADS + DIL_HEADS
    slopes = 2.0 ** (-8.0 * (jnp.arange(n_slopes, dtype=F32) + 1.0) / n_slopes)
    xf = x.reshape(batch * seq, d)

    g = qk_gain.astype(F32)
    tile = lambda v, n: jnp.tile(v, (1, n))
    gain_q = jnp.concatenate([tile(g[:, 0, 0], NA_HEADS), tile(g[:, 1, 0], SWA_Q_HEADS)], axis=1)[:, :, None]
    gain_r = jnp.concatenate([tile(g[:, 0, 1], NA_HEADS), tile(g[:, 1, 1], SWA_KV_HEADS),
                              tile(g[:, 2, 0], DIL_HEADS), tile(g[:, 2, 1], DIL_HEADS)], axis=1)[:, None, :]
    w_t, w_r = _split_w_in(w_in)
    table = _na_pair_table(rpb.reshape((depth * NA_HEADS,) + rpb.shape[2:]))
    table = table.reshape((depth, NA_HEADS) + table.shape[1:])
    og = out_gain.astype(F32)
    gain_ab, gain_c = og[:, :NA_W + SWA_QW, None], og[:, None, NA_W + SWA_QW:]
    attn_gain, ffn_gain = attn_norm.astype(F32)[:, None, :], ffn_norm.astype(F32)[:, None, :]
    sink = sink.astype(F32)

    for l in range(depth):
        qa_t, qb_t, va_t, vb_t, keys, dil1, dil4, dil16 = _proj(xf, l, attn_gain, w_t, w_r, gain_q, gain_r)
        ya_t = _na(qa_t, keys, va_t, l, table)
        yb_t = _swa(qb_t, keys, vb_t, l, slopes, sink)
        yc = _dil(dil1, dil4, dil16, slopes)
        xf = _out_ffn(ya_t, yb_t, yc, xf, l, gain_ab, gain_c, ffn_gain, w_out.astype(F32), w_gu.astype(F32),
                      w_down.astype(F32))
    return xf.reshape(batch, seq, d)
```

```python
import functools
import math

import numpy as np
import jax
import jax.numpy as jnp
from jax import lax
from jax.experimental import pallas as pl
from jax.experimental.pallas import tpu as pltpu

D_MODEL = 1024
SEQ = 2048
HEAD_DIM = 64
GRID_W = 64
NA_HEADS = 4
NA_ROWS = 8
NA_COLS = 16
SWA_Q_HEADS = 6
SWA_KV_HEADS = 2
SWA_GROUP = SWA_Q_HEADS // SWA_KV_HEADS
SWA_HALF_WINDOW = 128
DIL_HEADS = 6
DIL_STRIDES = (1, 4, 16)
DIL_HALF = 64
NA_W = NA_HEADS * HEAD_DIM
SWA_QW = SWA_Q_HEADS * HEAD_DIM
SWA_KVW = SWA_KV_HEADS * HEAD_DIM
DIL_W = DIL_HEADS * HEAD_DIM
IN_W = 3 * NA_W + SWA_QW + 2 * SWA_KVW + 3 * DIL_W
MIX_W = NA_W + SWA_QW + DIL_W
FFN_HIDDEN = 2816
NORM_EPS = 1e-6
NEG_INF = -1e30
QK_SCALE = HEAD_DIM ** -0.5
LOG2E = math.log2(math.e)
Q_SCALE = QK_SCALE * LOG2E

LANES = 128
VMEM_LIMIT = 56 * 1024 * 1024

QB = 128
QBT = 256
PREP_ROWS = 256
NQB = SEQ // QB
ROW_TILE = 1024
PROJ_ROW_TILE = 1024
MXU_DIM = 256
FFN_H_CHUNKS = (3 * MXU_DIM,) * 3 + (2 * MXU_DIM,)
assert sum(FFN_H_CHUNKS) == FFN_HIDDEN
NA_LAG = 2
DIL_LAG = 1

PROJ_T_Q = NA_W + SWA_QW
PROJ_T_V = NA_W + SWA_KVW
PROJ_R_K = NA_W + SWA_KVW
PROJ_R = PROJ_R_K + 3 * DIL_W
PROJ_R_CHUNK = 768

NA_WIN = 768
NA_VARIANTS = 3
NA_LAST_START = SEQ - NA_WIN
SWA_WIN = QBT + 2 * SWA_HALF_WINDOW

F32 = jnp.float32
BF16 = jnp.bfloat16


def _dot(a, b):
    return jnp.dot(a, b, preferred_element_type=F32)


def _dot_nt(a, b):
    return lax.dot_general(a, b, (((1,), (1,)), ((), ())), preferred_element_type=F32)


def _dot_tn(a, b):
    return lax.dot_general(a, b, (((0,), (0,)), ((), ())), preferred_element_type=F32)


def _low_half(shape):
    return lax.broadcasted_iota(jnp.int32, shape, len(shape) - 1) < HEAD_DIM


def _pair_rms(x, gain):
    lo = _low_half(x.shape)
    sq = x * x
    s_lo = jnp.sum(jnp.where(lo, sq, 0.0), axis=-1, keepdims=True)
    s_hi = jnp.sum(jnp.where(lo, 0.0, sq), axis=-1, keepdims=True)
    ms = jnp.where(lo, s_lo, s_hi) * (1.0 / HEAD_DIM)
    return x * lax.rsqrt(ms + NORM_EPS) * gain


def _softmax_block(s):
    m = jnp.max(s, axis=-1, keepdims=True)
    return m, jnp.exp2(s - m).astype(BF16)


def _proj_kernel(x_ref, g_ref, wt_ref, wr_ref, gq_ref, gr_ref,
                 qva_ref, qvb_ref, k_ref, dil_ref, dil4_ref, dil16_ref, regroup_scr):
    x = x_ref[...]
    ms = jnp.mean(x * x, axis=-1, keepdims=True)
    h = (x * lax.rsqrt(ms + NORM_EPS) * g_ref[...]).astype(BF16)
    tm = h.shape[0]

    def finish_q(qt):
        q3 = qt.reshape(PROJ_T_Q // HEAD_DIM, HEAD_DIM, tm)
        q3 = q3 * lax.rsqrt(jnp.mean(q3 * q3, axis=1, keepdims=True) + NORM_EPS)
        qt = q3.reshape(PROJ_T_Q, tm) * gq_ref[...] * Q_SCALE
        qva_ref[0:NA_W, :] = qt[0:NA_W].astype(BF16)
        qvb_ref[0:SWA_QW, :] = qt[NA_W:PROJ_T_Q].astype(BF16)

    def finish_v(vt):
        qva_ref[NA_W:2 * NA_W, :] = vt[0:NA_W].astype(BF16)
        qvb_ref[SWA_QW:SWA_QW + SWA_KVW, :] = vt[NA_W:PROJ_T_V].astype(BF16)

    n_norm = (PROJ_R_K + 2 * DIL_W) // LANES
    n_key = PROJ_R_K // LANES
    n_dil_q = DIL_W // LANES

    def finish_r(c, r):
        for t in range(PROJ_R_CHUNK // LANES):
            blk = c * (PROJ_R_CHUNK // LANES) + t
            tile = r[:, t * LANES:(t + 1) * LANES]
            if blk < n_norm:
                tile = _pair_rms(tile, gr_ref[:, blk * LANES:(blk + 1) * LANES])
            if n_key <= blk < n_key + n_dil_q:
                tile = tile * Q_SCALE
            if blk < n_key:
                k_ref[:, blk * LANES:(blk + 1) * LANES] = tile.astype(BF16)
                continue
            d = blk - n_key
            cols = slice(d * LANES, (d + 1) * LANES)
            dil_ref[:, cols] = tile.astype(BF16)
            fine, coarse = DIL_STRIDES[1], DIL_STRIDES[2] // DIL_STRIDES[1]
            per_fine = tm // fine
            regroup_scr[d, 0] = tile
            for ph in range(fine):
                part = regroup_scr[d, 0, pl.ds(ph, per_fine, stride=fine), :]
                dil4_ref[0, ph, :, cols] = part.astype(BF16)
                regroup_scr[d, 1, ph * per_fine:(ph + 1) * per_fine, :] = part
            for ph in range(fine):
                for sub_ph in range(coarse):
                    part = regroup_scr[d, 1, pl.ds(ph * per_fine + sub_ph, per_fine // coarse, stride=coarse), :]
                    dil16_ref[0, ph + fine * sub_ph, :, cols] = part.astype(BF16)

    finish_q(_dot_nt(wt_ref[0:PROJ_T_Q, :], h))
    finish_v(_dot_nt(wt_ref[PROJ_T_Q:PROJ_T_Q + PROJ_T_V, :], h))
    for c in range(PROJ_R // PROJ_R_CHUNK):
        finish_r(c, _dot(h, wr_ref[:, c * PROJ_R_CHUNK:(c + 1) * PROJ_R_CHUNK]))


def _group_rms_t(yt, gain_col):
    yf = yt.astype(F32)
    ms = jnp.mean(yf * yf, axis=0, keepdims=True)
    return (yf * lax.rsqrt(ms + NORM_EPS) * gain_col).astype(BF16)


def _out_ffn_kernel(ya_ref, yb_ref, yc_ref, x_ref, gab_ref, gc_ref, wo_ref, gf_ref, wgu_ref, wd_ref,
                    o_ref, h_scr):
    ya = _group_rms_t(ya_ref[...], gab_ref[0:NA_W, :])
    yb = _group_rms_t(yb_ref[...], gab_ref[NA_W:NA_W + SWA_QW, :])
    yc = yc_ref[...].astype(F32)
    ms = jnp.mean(yc * yc, axis=-1, keepdims=True)
    yc = (yc * lax.rsqrt(ms + NORM_EPS) * gc_ref[...]).astype(BF16)
    acc = _dot_tn(ya, wo_ref[0:NA_W, :])
    acc = acc + _dot_tn(yb, wo_ref[NA_W:NA_W + SWA_QW, :])
    acc = acc + _dot(yc, wo_ref[NA_W + SWA_QW:MIX_W, :])
    x = x_ref[...] + acc

    ms = jnp.mean(x * x, axis=-1, keepdims=True)
    h_scr[...] = (x * lax.rsqrt(ms + NORM_EPS) * gf_ref[...]).astype(BF16)
    o_ref[...] = x
    lo = 0
    for width in FFN_H_CHUNKS:
        gate = _dot(h_scr[...], wgu_ref[:, lo:lo + width])
        up = _dot(h_scr[...], wgu_ref[:, FFN_HIDDEN + lo:FFN_HIDDEN + lo + width])
        act = (gate / (1.0 + jnp.exp(-gate)) * up).astype(BF16)
        o_ref[...] += _dot(act, wd_ref[lo:lo + width, :])
        lo += width


def _pipeline3(chains, stage_a, stage_b, stage_c, lag=1):
    n = len(chains)
    a, b = {}, {}
    for i in range(-2 * lag, n):
        if 0 <= i + 2 * lag < n:
            a[i + 2 * lag] = stage_a(chains[i + 2 * lag])
        if 0 <= i + lag < n:
            b[i + lag] = stage_b(chains[i + lag], a.pop(i + lag))
        if i >= 0:
            stage_c(chains[i], b.pop(i))


def _attend_t(chains, lag=1):
    def scores(c):
        qt = c["q"]()
        zeros = jnp.zeros_like(qt)
        qpad = jnp.concatenate([qt, zeros] if c["half"] == 0 else [zeros, qt], axis=0)
        kw = c["k"]()
        mid = kw.shape[0] // 2
        s = jnp.concatenate([_dot(kw[:mid], qpad), _dot(kw[mid:], qpad)], axis=0)
        return s + c["bias"]()

    def softmax(c, s):
        m = jnp.max(s, axis=0, keepdims=True)
        if c["sink"] is not None:
            m = jnp.maximum(m, c["sink"])
        return jnp.exp2(s - m).astype(BF16), m

    def values(c, pm):
        p, m = pm
        vt = c["v"]()
        own = vt[c["half"] * HEAD_DIM:(c["half"] + 1) * HEAD_DIM, :]
        ones = jnp.ones_like(own)
        o = _dot(jnp.concatenate([own, ones], axis=0), p)
        den = o[HEAD_DIM:2 * HEAD_DIM, :]
        if c["sink"] is not None:
            den = den + jnp.exp2(c["sink"] - m)
        c["store"](o[0:HEAD_DIM, :] / den)

    _pipeline3(chains, scores, softmax, values, lag=lag)


NA_KEY_ROWS = NA_WIN // GRID_W
NA_Q_ROWS = QBT // GRID_W
NA_ROW_REL = 2 * NA_ROWS - 1


def _na_geometry():
    n_blocks = SEQ // QBT
    out = []
    for variant, block in enumerate((0, 1, n_blocks - 1)):
        start = min(max(block * QBT - QBT, 0), NA_LAST_START)
        assert (block * QBT - start) // QBT == variant
        out.append((start // GRID_W, block * NA_Q_ROWS))
    return out


def _na_kernel(q_ref, k_ref, v_ref, pair_ref, o_ref, tbl_scr):
    @pl.when(pl.program_id(0) == 0)
    def _():
        grid_rows = SEQ // GRID_W
        lo = _low_half((GRID_W, LANES))
        masked = jnp.full((GRID_W, LANES), NEG_INF, F32)
        for variant, (key_row0, query_row0) in enumerate(_na_geometry()):
            for kr in range(NA_KEY_ROWS):
                for pair in range(NA_Q_ROWS // 2):
                    krow = key_row0 + kr
                    rel, ok = [], []
                    for qrow in (query_row0 + 2 * pair, query_row0 + 2 * pair + 1):
                        first = min(max(qrow - NA_ROWS // 2, 0), grid_rows - NA_ROWS)
                        ok.append(first <= krow < first + NA_ROWS)
                        rel.append(krow - qrow + NA_ROWS - 1)
                    for h in range(NA_HEADS):
                        if not (ok[0] or ok[1]):
                            tile = masked
                        else:
                            tile = pair_ref[h, rel[0]]
                            if not ok[1]:
                                tile = jnp.where(lo, tile, masked)
                            elif not ok[0]:
                                tile = jnp.where(lo, masked, tile)
                        tbl_scr[h, variant, kr * GRID_W:(kr + 1) * GRID_W, pair * LANES:(pair + 1) * LANES] = tile

    def store(rows, cols):
        def put(o):
            o_ref[rows, cols] = o.astype(BF16)
        return put

    chains = []
    for j in range(SEQ // QBT):
        q0 = j * QBT
        start = min(max(q0 - QBT, 0), NA_LAST_START)
        variant = (q0 - start) // QBT
        cols = slice(q0, q0 + QBT)
        keys = slice(start, start + NA_WIN)
        for h in range(NA_HEADS):
            rows = slice(h * HEAD_DIM, (h + 1) * HEAD_DIM)
            pair = slice((h // 2) * LANES, (h // 2 + 1) * LANES)
            chains.append(dict(
                half=h % 2, sink=None,
                q=lambda rows=rows, cols=cols: q_ref[rows, cols],
                k=lambda keys=keys, pair=pair: k_ref[keys, pair],
                v=lambda keys=keys, pair=pair: v_ref[pair, keys],
                bias=lambda h=h, variant=variant: tbl_scr[h, variant],
                store=store(rows, cols)))
    _attend_t(chains, lag=NA_LAG)


def _na_pair_table(rpb):
    cols = np.arange(GRID_W)
    col_start = np.clip(cols - NA_COLS // 2, 0, GRID_W - NA_COLS)
    col_ok = (cols[:, None] >= col_start[None, :]) & (cols[:, None] < col_start[None, :] + NA_COLS)
    heads, n_row_rel, n_col_rel = rpb.shape
    assert n_row_rel == NA_ROW_REL
    period = 2 * GRID_W - 1
    lead = GRID_W - NA_COLS
    seq = jnp.pad(rpb.astype(F32)[:, :, ::-1], ((0, 0), (0, 0), (lead, period - lead - n_col_rel)))
    skew = jnp.tile(seq, (1, 1, GRID_W + 1))[..., :GRID_W * 2 * GRID_W]
    toe = skew.reshape(heads, n_row_rel, GRID_W, 2 * GRID_W)[:, :, ::-1, :GRID_W]
    toe = jnp.where(col_ok[None, None], toe * LOG2E, NEG_INF)
    toe = jnp.pad(toe, ((0, 0), (1, 1), (0, 0), (0, 0)), constant_values=NEG_INF)
    return jnp.concatenate([toe[:, 1:], toe[:, :-1]], axis=-1)


def _swa_kernel(layer, slopes_ref, sink_ref, q_ref, k_ref, v_ref, o_ref, bias_scr):
    @pl.when(pl.program_id(0) == 0)
    def _():
        kk = lax.broadcasted_iota(jnp.int32, (SWA_WIN, QBT), 0)
        qq = lax.broadcasted_iota(jnp.int32, (SWA_WIN, QBT), 1)
        dist = jnp.abs(kk - SWA_HALF_WINDOW - qq)
        in_window = dist <= SWA_HALF_WINDOW
        dist_f = dist.astype(F32)
        for h in range(SWA_Q_HEADS):
            bias_scr[h] = jnp.where(in_window, -slopes_ref[h] * dist_f * LOG2E, NEG_INF)

    def store(rows, cols):
        def put(o):
            o_ref[rows, cols] = o.astype(BF16)
        return put

    chains = []
    for j in range(SEQ // QBT):
        q0 = j * QBT
        lo = max(q0 - SWA_HALF_WINDOW, 0)
        hi = min(q0 + QBT + SWA_HALF_WINDOW, SEQ)
        first = lo - (q0 - SWA_HALF_WINDOW)
        cols = slice(q0, q0 + QBT)
        keys = slice(lo, hi)
        win = slice(first, first + hi - lo)
        for h in range(SWA_Q_HEADS):
            rows = slice(h * HEAD_DIM, (h + 1) * HEAD_DIM)
            chains.append(dict(
                half=h // SWA_GROUP, sink=sink_ref[layer, h] * LOG2E,
                q=lambda rows=rows, cols=cols: q_ref[rows, cols],
                k=lambda keys=keys: k_ref[keys, :],
                v=lambda keys=keys: v_ref[:, keys],
                bias=lambda h=h, win=win: bias_scr[h, win, :],
                store=store(rows, cols)))
    _attend_t(chains)


DIL_QB = 128
DIL_WIN = DIL_QB + 2 * DIL_HALF


def _dil_kernel(slopes_ref, q1_ref, k1_ref, v1_ref, q4_ref, k4_ref, v4_ref, q16_ref, k16_ref, v16_ref,
                o_ref, bias_scr, bias16_scr, o_scr, lse_scr, out_scr):
    pair = pl.program_id(1)

    @pl.when(pl.program_id(0) == 0)
    def _():
        slope = [slopes_ref[SWA_Q_HEADS + 2 * pair + h] for h in range(2)]
        qq = lax.broadcasted_iota(jnp.int32, (DIL_QB, DIL_WIN), 0)
        kk = lax.broadcasted_iota(jnp.int32, (DIL_QB, DIL_WIN), 1)
        for shift in range(3):
            dist = jnp.abs(kk - shift * DIL_HALF - qq)
            ok = dist <= DIL_HALF
            dist_f = dist.astype(F32)
            for h in range(2):
                for b, stride in enumerate(DIL_STRIDES[:2]):
                    bias_scr[pair, b, shift, h * DIL_QB:(h + 1) * DIL_QB, :] = jnp.where(
                        ok, -slope[h] * (dist_f * float(stride)) * LOG2E, NEG_INF)
        qq = lax.broadcasted_iota(jnp.int32, (QB, QB), 0)
        kk = lax.broadcasted_iota(jnp.int32, (QB, QB), 1)
        dist = jnp.abs(kk - qq)
        for h in range(2):
            bias16_scr[pair, h * QB:(h + 1) * QB, :] = jnp.where(
                dist <= DIL_HALF, -slope[h] * (dist.astype(F32) * float(DIL_STRIDES[2])) * LOG2E, NEG_INF)

    chains = []
    for b, stride in enumerate(DIL_STRIDES):
        sub = SEQ // stride
        qb = QB if sub == QB else DIL_QB
        for start in range(0, SEQ, qb):
            phase, local_q = divmod(start, sub)
            if sub == QB:
                local_k, n_keys, shift = 0, QB, None
            else:
                local_k = min(max(local_q - DIL_HALF, 0), sub - DIL_WIN)
                n_keys, shift = DIL_WIN, (local_q - local_k) // DIL_HALF
            qrows, krows = slice(local_q, local_q + qb), slice(local_k, local_k + n_keys)
            fine = DIL_STRIDES[1]
            if stride == 1:
                refs = (q1_ref, k1_ref, v1_ref)
                load = lambda ref, rows: ref[rows, :]
                dst = qrows
            elif stride == fine:
                refs = (q4_ref, k4_ref, v4_ref)
                load = lambda ref, rows, phase=phase: ref[0, phase, rows, :]
                dst = slice(start, start + qb)
            else:
                refs = (q16_ref, k16_ref, v16_ref)
                load = lambda ref, rows, phase=phase: ref[0, phase, rows, :]
                coarse = stride // fine
                dst = pl.ds((phase % fine) * (SEQ // fine) + local_q * coarse + phase // fine, qb, stride=coarse)
            chains.append(dict(b=b, shift=shift, dst=dst, qb=qb, order=start + qb,
                               q=lambda refs=refs, load=load, rows=qrows: load(refs[0], rows),
                               k=lambda refs=refs, load=load, rows=krows: load(refs[1], rows),
                               v=lambda refs=refs, load=load, rows=krows: load(refs[2], rows)))
    chains.sort(key=lambda c: c["order"])

    def scores(c):
        q, kw = c["q"](), c["k"]()
        lo = _low_half(q.shape)
        zeros = jnp.zeros_like(q)
        q_both = jnp.concatenate([jnp.where(lo, q, zeros), jnp.where(lo, zeros, q)], axis=0)
        bias = bias16_scr[pair] if c["shift"] is None else bias_scr[pair, c["b"], c["shift"]]
        return _dot_nt(q_both, kw) + bias

    def softmax(c, s):
        return _softmax_block(s)

    def values(c, mp):
        m, p = mp
        vw = c["v"]()
        res = _dot(p, jnp.concatenate([vw, jnp.ones_like(vw)], axis=1))
        qb = c["qb"]
        lo = _low_half((qb, LANES))
        den = jnp.where(lo, res[0:qb, LANES:2 * LANES], res[qb:2 * qb, LANES:2 * LANES])
        o_scr[c["b"], c["dst"], :] = jnp.where(lo, res[0:qb, 0:LANES], res[qb:2 * qb, 0:LANES]) / den
        lse_scr[c["b"], c["dst"], :] = jnp.where(lo, m[0:qb], m[qb:2 * qb]) + jnp.log2(den)

    _pipeline3(chains, scores, softmax, values, lag=DIL_LAG)

    fine = DIL_STRIDES[1]
    per_phase = SEQ // fine

    def merge(i, carry):
        start = pl.multiple_of(i * PREP_ROWS, PREP_ROWS)
        rows = pl.ds(start, PREP_ROWS)
        natural = pl.ds((start % per_phase) * fine + start // per_phase, PREP_ROWS, stride=fine)
        lse = [lse_scr[0, natural, :], lse_scr[1, rows, :], lse_scr[2, rows, :]]
        outs = [o_scr[0, natural, :], o_scr[1, rows, :], o_scr[2, rows, :]]
        top = jnp.maximum(jnp.maximum(lse[0], lse[1]), lse[2])
        num = jnp.zeros((PREP_ROWS, LANES), F32)
        den = jnp.zeros((PREP_ROWS, LANES), F32)
        for b in range(3):
            w = jnp.exp2(lse[b] - top)
            num = num + w * outs[b]
            den = den + w
        out_scr[natural, :] = num / den
        return carry

    lax.fori_loop(0, SEQ // PREP_ROWS, merge, 0)

    def emit(i, carry):
        rows = pl.ds(pl.multiple_of(i * PREP_ROWS, PREP_ROWS), PREP_ROWS)
        o_ref[rows, :] = out_scr[rows, :].astype(BF16)
        return carry

    lax.fori_loop(0, SEQ // PREP_ROWS, emit, 0)


def _params(semantics):
    return pltpu.CompilerParams(dimension_semantics=semantics, vmem_limit_bytes=VMEM_LIMIT)


def _resident(shape, layer):
    return pl.BlockSpec((None,) + tuple(shape), lambda *_: (layer,) + (0,) * len(shape),
                        pipeline_mode=pl.Buffered(1))


def _smem():
    return pl.BlockSpec(memory_space=pltpu.SMEM)


def _proj(x, layer, gain, w_t, w_r, gain_q, gain_r):
    t = x.shape[0]
    rows = PROJ_ROW_TILE
    feat = lambda n: pl.BlockSpec((n, rows), lambda i: (0, i))
    tok = lambda n: pl.BlockSpec((rows, n), lambda i: (i, 0))
    tiles_per_seq = SEQ // rows

    def phased(stride):
        return pl.BlockSpec((1, stride, rows // stride, 3 * DIL_W),
                            lambda i: (i // tiles_per_seq, 0, i % tiles_per_seq, 0))

    def phased_shape(stride):
        return jax.ShapeDtypeStruct((t // SEQ, stride, SEQ // stride, 3 * DIL_W), BF16)

    return pl.pallas_call(
        _proj_kernel,
        out_shape=[jax.ShapeDtypeStruct((2 * NA_W, t), BF16), jax.ShapeDtypeStruct((SWA_QW + SWA_KVW, t), BF16),
                   jax.ShapeDtypeStruct((t, PROJ_R_K), BF16), jax.ShapeDtypeStruct((t, 3 * DIL_W), BF16),
                   phased_shape(DIL_STRIDES[1]), phased_shape(DIL_STRIDES[2])],
        grid=(t // rows,),
        in_specs=[tok(D_MODEL), _resident((1, D_MODEL), layer),
                  _resident((PROJ_T_Q + PROJ_T_V, D_MODEL), layer), _resident((D_MODEL, PROJ_R), layer),
                  _resident((PROJ_T_Q, 1), layer), _resident((1, PROJ_R_K + 2 * DIL_W), layer)],
        out_specs=[feat(2 * NA_W), feat(SWA_QW + SWA_KVW), tok(PROJ_R_K), tok(3 * DIL_W),
                   phased(DIL_STRIDES[1]), phased(DIL_STRIDES[2])],
        scratch_shapes=[pltpu.VMEM((3 * DIL_W // LANES, 2, rows, LANES), F32)],
        compiler_params=_params(("parallel",)),
        name="proj",
    )(x, gain, w_t, w_r, gain_q, gain_r)


def _na(qv_t, keys, layer, table):
    t = qv_t.shape[1]
    feat = pl.BlockSpec((NA_W, SEQ), lambda b: (0, b))
    return pl.pallas_call(
        _na_kernel,
        out_shape=jax.ShapeDtypeStruct((NA_W, t), BF16),
        grid=(t // SEQ,),
        in_specs=[feat, pl.BlockSpec((SEQ, NA_W), lambda b: (b, 0)),
                  pl.BlockSpec((NA_W, SEQ), lambda b: (1, b)),
                  _resident((NA_HEADS, NA_ROW_REL + 1, GRID_W, 2 * GRID_W), layer)],
        out_specs=feat,
        scratch_shapes=[pltpu.VMEM((NA_HEADS, NA_VARIANTS, NA_WIN, QBT), F32)],
        compiler_params=_params(("arbitrary",)),
        name="na",
    )(qv_t, keys, qv_t, table)


def _swa(qv_t, keys, layer, slopes, sink):
    t = qv_t.shape[1]
    return pl.pallas_call(
        functools.partial(_swa_kernel, layer),
        out_shape=jax.ShapeDtypeStruct((SWA_QW, t), BF16),
        grid=(t // SEQ,),
        in_specs=[_smem(), _smem(),
                  pl.BlockSpec((SWA_QW, SEQ), lambda b: (0, b)),
                  pl.BlockSpec((SEQ, SWA_KVW), lambda b: (b, NA_W // SWA_KVW)),
                  pl.BlockSpec((SWA_KVW, SEQ), lambda b: (SWA_QW // SWA_KVW, b))],
        out_specs=pl.BlockSpec((SWA_QW, SEQ), lambda b: (0, b)),
        scratch_shapes=[pltpu.VMEM((SWA_Q_HEADS, SWA_WIN, QBT), F32)],
        compiler_params=_params(("arbitrary",)),
        name="swa",
    )(slopes, sink, qv_t, keys, qv_t)


def _dil(qkv, qkv4, qkv16, slopes):
    t = qkv.shape[0]
    pairs = DIL_W // LANES
    seq_block = lambda first: pl.BlockSpec((SEQ, LANES), lambda b, p: (b, first + p))

    def phased(stride, first):
        return pl.BlockSpec((1, stride, SEQ // stride, LANES), lambda b, p: (b, 0, 0, first + p))

    operands, specs = [], []
    for arr, spec in ((qkv, seq_block), (qkv4, lambda f: phased(DIL_STRIDES[1], f)),
                      (qkv16, lambda f: phased(DIL_STRIDES[2], f))):
        for part in range(3):
            operands.append(arr)
            specs.append(spec(part * pairs))
    return pl.pallas_call(
        _dil_kernel,
        out_shape=jax.ShapeDtypeStruct((t, DIL_W), BF16),
        grid=(t // SEQ, pairs),
        in_specs=[_smem()] + specs,
        out_specs=seq_block(0),
        scratch_shapes=[pltpu.VMEM((pairs, 2, 3, 2 * DIL_QB, DIL_WIN), F32),
                        pltpu.VMEM((pairs, 2 * QB, QB), F32),
                        pltpu.VMEM((3, SEQ, LANES), F32),
                        pltpu.VMEM((3, SEQ, LANES), F32),
                        pltpu.VMEM((SEQ, LANES), F32)],
        compiler_params=_params(("arbitrary", "arbitrary")),
        name="dil",
    )(slopes, *operands)


def _out_ffn(ya_t, yb_t, yc, x, layer, gain_ab, gain_c, w_out, gain_ffn, w_gu, w_down):
    t = x.shape[0]
    feat = lambda n: pl.BlockSpec((n, ROW_TILE), lambda i: (0, i))
    tok = lambda n: pl.BlockSpec((ROW_TILE, n), lambda i: (i, 0))
    return pl.pallas_call(
        _out_ffn_kernel,
        out_shape=jax.ShapeDtypeStruct((t, D_MODEL), F32),
        grid=(t // ROW_TILE,),
        in_specs=[feat(NA_W), feat(SWA_QW), tok(DIL_W), tok(D_MODEL),
                  _resident((NA_W + SWA_QW, 1), layer), _resident((1, DIL_W), layer),
                  _resident((MIX_W, D_MODEL), layer), _resident((1, D_MODEL), layer),
                  _resident((D_MODEL, 2 * FFN_HIDDEN), layer), _resident((FFN_HIDDEN, D_MODEL), layer)],
        out_specs=tok(D_MODEL),
        scratch_shapes=[pltpu.VMEM((ROW_TILE, D_MODEL), BF16)],
        compiler_params=_params(("parallel",)),
        name="out_ffn",
    )(ya_t, yb_t, yc, x, gain_ab, gain_c, w_out, gain_ffn, w_gu, w_down)


def _split_w_in(w):
    cuts = np.cumsum([NA_W, NA_W, NA_W, SWA_QW, SWA_KVW, SWA_KVW, DIL_W, DIL_W])
    qa, ka, va, qb, kb, vb, qc, kc, vc = jnp.split(w, [int(c) for c in cuts], axis=2)
    w_t = jnp.concatenate([qa, qb, va, vb], axis=2).astype(BF16).transpose(0, 2, 1)
    w_r = jnp.concatenate([ka, kb, qc, kc, vc], axis=2).astype(BF16)
    return w_t, w_r


def kernel(x, attn_norm, w_in, qk_gain, rpb, sink, out_gain, w_out, ffn_norm, w_gu, w_down):
    batch, seq, d = x.shape
    assert (seq, d) == (SEQ, D_MODEL)
    depth = w_in.shape[0]
    n_slopes = SWA_Q_HEADS + DIL_HEADS
    slopes = 2.0 ** (-8.0 * (jnp.arange(n_slopes, dtype=F32) + 1.0) / n_slopes)
    xf = x.reshape(batch * seq, d)

    g = qk_gain.astype(F32)
    tile = lambda v, n: jnp.tile(v, (1, n))
    gain_q = jnp.concatenate([tile(g[:, 0, 0], NA_HEADS), tile(g[:, 1, 0], SWA_Q_HEADS)], axis=1)[:, :, None]
    gain_r = jnp.concatenate([tile(g[:, 0, 1], NA_HEADS), tile(g[:, 1, 1], SWA_KV_HEADS),
                              tile(g[:, 2, 0], DIL_HEADS), tile(g[:, 2, 1], DIL_HEADS)], axis=1)[:, None, :]
    w_t, w_r = _split_w_in(w_in)
    table = _na_pair_table(rpb.reshape((depth * NA_HEADS,) + rpb.shape[2:]))
    table = table.reshape((depth, NA_HEADS) + table.shape[1:])
    og = out_gain.astype(F32)
    gain_ab, gain_c = og[:, :NA_W + SWA_QW, None], og[:, None, NA_W + SWA_QW:]
    w_out_b, w_gu_b, w_down_b = w_out.astype(BF16), w_gu.astype(BF16), w_down.astype(BF16)
    attn_gain, ffn_gain = attn_norm.astype(F32)[:, None, :], ffn_norm.astype(F32)[:, None, :]
    sink = sink.astype(F32)

    for l in range(depth):
        qva_t, qvb_t, keys, dil1, dil4, dil16 = _proj(xf, l, attn_gain, w_t, w_r, gain_q, gain_r)
        ya_t = _na(qva_t, keys, l, table)
        yb_t = _swa(qvb_t, keys, l, slopes, sink)
        yc = _dil(dil1, dil4, dil16, slopes)
        xf = _out_ffn(ya_t, yb_t, yc, xf, l, gain_ab, gain_c, w_out_b, ffn_gain, w_gu_b, w_down_b)
    return xf.reshape(batch, seq, d)
```

```python
import functools
import math

import numpy as np
import jax
import jax.numpy as jnp
from jax import lax
from jax.experimental import pallas as pl
from jax.experimental.pallas import tpu as pltpu

D_MODEL = 1024
SEQ = 2048
HEAD_DIM = 64
GRID_W = 64
NA_HEADS = 4
NA_ROWS = 8
NA_COLS = 16
SWA_Q_HEADS = 6
SWA_KV_HEADS = 2
SWA_GROUP = SWA_Q_HEADS // SWA_KV_HEADS
SWA_HALF_WINDOW = 128
DIL_HEADS = 6
DIL_STRIDES = (1, 4, 16)
DIL_HALF = 64
NA_W = NA_HEADS * HEAD_DIM
SWA_QW = SWA_Q_HEADS * HEAD_DIM
SWA_KVW = SWA_KV_HEADS * HEAD_DIM
DIL_W = DIL_HEADS * HEAD_DIM
IN_W = 3 * NA_W + SWA_QW + 2 * SWA_KVW + 3 * DIL_W
MIX_W = NA_W + SWA_QW + DIL_W
FFN_HIDDEN = 2816
NORM_EPS = 1e-6
NEG_INF = -1e30
QK_SCALE = HEAD_DIM ** -0.5
LOG2E = math.log2(math.e)
Q_SCALE = QK_SCALE * LOG2E

LANES = 128
VMEM_LIMIT = 56 * 1024 * 1024

QB = 128
QBT = 256
PREP_ROWS = 256
ROW_TILE = 1024
PROJ_ROW_TILE = 1024
MXU_DIM = 256
FFN_H_CHUNKS = (3 * MXU_DIM,) * 3 + (2 * MXU_DIM,)
assert sum(FFN_H_CHUNKS) == FFN_HIDDEN
NA_LAG = 2
DIL_LAG = 1

PROJ_T_Q = NA_W + SWA_QW
PROJ_T_V = NA_W + SWA_KVW
PROJ_R_K = NA_W + SWA_KVW
PROJ_R = PROJ_R_K + 3 * DIL_W
PROJ_R_CHUNK = 768

NA_WIN = 768
NA_VARIANTS = 3
NA_LAST_START = SEQ - NA_WIN
SWA_WIN = QBT + 2 * SWA_HALF_WINDOW

F32 = jnp.float32
BF16 = jnp.bfloat16


def _dot(a, b):
    return jnp.dot(a, b, preferred_element_type=F32)


def _dot_nt(a, b):
    return lax.dot_general(a, b, (((1,), (1,)), ((), ())), preferred_element_type=F32)


def _dot_tn(a, b):
    return lax.dot_general(a, b, (((0,), (0,)), ((), ())), preferred_element_type=F32)


def _low_half(shape):
    return lax.broadcasted_iota(jnp.int32, shape, len(shape) - 1) < HEAD_DIM


def _pair_rms(x, gain):
    lo = _low_half(x.shape)
    sq = x * x
    s_lo = jnp.sum(jnp.where(lo, sq, 0.0), axis=-1, keepdims=True)
    s_hi = jnp.sum(jnp.where(lo, 0.0, sq), axis=-1, keepdims=True)
    ms = jnp.where(lo, s_lo, s_hi) * (1.0 / HEAD_DIM)
    return x * lax.rsqrt(ms + NORM_EPS) * gain


def _softmax_block(s):
    m = jnp.max(s, axis=-1, keepdims=True)
    return m, jnp.exp2(s - m).astype(BF16)


def _proj_kernel(x_ref, g_ref, wt_ref, wr_ref, gq_ref, gr_ref,
                 qva_ref, qvb_ref, k_ref, dil_ref, dil4_ref, dil16_ref, regroup_scr):
    x = x_ref[...]
    ms = jnp.mean(x * x, axis=-1, keepdims=True)
    h = (x * lax.rsqrt(ms + NORM_EPS) * g_ref[...]).astype(BF16)
    tm = h.shape[0]

    def finish_q(qt):
        q3 = qt.reshape(PROJ_T_Q // HEAD_DIM, HEAD_DIM, tm)
        q3 = q3 * lax.rsqrt(jnp.mean(q3 * q3, axis=1, keepdims=True) + NORM_EPS)
        qt = q3.reshape(PROJ_T_Q, tm) * gq_ref[...] * Q_SCALE
        qva_ref[0:NA_W, :] = qt[0:NA_W].astype(BF16)
        qvb_ref[0:SWA_QW, :] = qt[NA_W:PROJ_T_Q].astype(BF16)

    def finish_v(vt):
        qva_ref[NA_W:2 * NA_W, :] = vt[0:NA_W].astype(BF16)
        qvb_ref[SWA_QW:SWA_QW + SWA_KVW, :] = vt[NA_W:PROJ_T_V].astype(BF16)

    n_norm = (PROJ_R_K + 2 * DIL_W) // LANES
    n_key = PROJ_R_K // LANES
    n_dil_q = DIL_W // LANES

    def finish_r(c, r):
        for t in range(PROJ_R_CHUNK // LANES):
            blk = c * (PROJ_R_CHUNK // LANES) + t
            tile = r[:, t * LANES:(t + 1) * LANES]
            if blk < n_norm:
                tile = _pair_rms(tile, gr_ref[:, blk * LANES:(blk + 1) * LANES])
            if n_key <= blk < n_key + n_dil_q:
                tile = tile * Q_SCALE
            if blk < n_key:
                k_ref[:, blk * LANES:(blk + 1) * LANES] = tile.astype(BF16)
                continue
            d = blk - n_key
            cols = slice(d * LANES, (d + 1) * LANES)
            dil_ref[:, cols] = tile.astype(BF16)
            fine, coarse = DIL_STRIDES[1], DIL_STRIDES[2] // DIL_STRIDES[1]
            per_fine = tm // fine
            regroup_scr[d, 0] = tile
            for ph in range(fine):
                part = regroup_scr[d, 0, pl.ds(ph, per_fine, stride=fine), :]
                dil4_ref[0, ph, :, cols] = part.astype(BF16)
                regroup_scr[d, 1, ph * per_fine:(ph + 1) * per_fine, :] = part
            for ph in range(fine):
                for sub_ph in range(coarse):
                    part = regroup_scr[d, 1, pl.ds(ph * per_fine + sub_ph, per_fine // coarse, stride=coarse), :]
                    dil16_ref[0, ph + fine * sub_ph, :, cols] = part.astype(BF16)

    finish_q(_dot_nt(wt_ref[0:PROJ_T_Q, :], h))
    finish_v(_dot_nt(wt_ref[PROJ_T_Q:PROJ_T_Q + PROJ_T_V, :], h))
    for c in range(PROJ_R // PROJ_R_CHUNK):
        finish_r(c, _dot(h, wr_ref[:, c * PROJ_R_CHUNK:(c + 1) * PROJ_R_CHUNK]))


def _group_rms_t(yt, gain_col):
    yf = yt.astype(F32)
    ms = jnp.mean(yf * yf, axis=0, keepdims=True)
    return (yf * lax.rsqrt(ms + NORM_EPS) * gain_col).astype(BF16)


def _out_ffn_kernel(ya_ref, yb_ref, yc_ref, x_ref, gab_ref, gc_ref, wo_ref, gf_ref, wgu_ref, wd_ref,
                    o_ref, h_scr):
    ya = _group_rms_t(ya_ref[...], gab_ref[0:NA_W, :])
    yb = _group_rms_t(yb_ref[...], gab_ref[NA_W:NA_W + SWA_QW, :])
    yc = yc_ref[...].astype(F32)
    ms = jnp.mean(yc * yc, axis=-1, keepdims=True)
    yc = (yc * lax.rsqrt(ms + NORM_EPS) * gc_ref[...]).astype(BF16)
    acc = _dot_tn(ya, wo_ref[0:NA_W, :])
    acc = acc + _dot_tn(yb, wo_ref[NA_W:NA_W + SWA_QW, :])
    acc = acc + _dot(yc, wo_ref[NA_W + SWA_QW:MIX_W, :])
    x = x_ref[...] + acc

    ms = jnp.mean(x * x, axis=-1, keepdims=True)
    h_scr[...] = (x * lax.rsqrt(ms + NORM_EPS) * gf_ref[...]).astype(BF16)
    o_ref[...] = x
    lo = 0
    for width in FFN_H_CHUNKS:
        gate = _dot(h_scr[...], wgu_ref[:, lo:lo + width])
        up = _dot(h_scr[...], wgu_ref[:, FFN_HIDDEN + lo:FFN_HIDDEN + lo + width])
        act = (gate / (1.0 + jnp.exp(-gate)) * up).astype(BF16)
        o_ref[...] += _dot(act, wd_ref[lo:lo + width, :])
        lo += width


def _pipeline3(chains, stage_a, stage_b, stage_c, lag=1):
    n = len(chains)
    a, b = {}, {}
    for i in range(-2 * lag, n):
        if 0 <= i + 2 * lag < n:
            a[i + 2 * lag] = stage_a(chains[i + 2 * lag])
        if 0 <= i + lag < n:
            b[i + lag] = stage_b(chains[i + lag], a.pop(i + lag))
        if i >= 0:
            stage_c(chains[i], b.pop(i))


def _attend_t(chains, lag=1):
    def scores(c):
        qt = c["q"]()
        zeros = jnp.zeros_like(qt)
        qpad = jnp.concatenate([qt, zeros] if c["half"] == 0 else [zeros, qt], axis=0)
        kw = c["k"]()
        mid = kw.shape[0] // 2
        s = jnp.concatenate([_dot(kw[:mid], qpad), _dot(kw[mid:], qpad)], axis=0)
        return s + c["bias"]()

    def softmax(c, s):
        m = jnp.max(s, axis=0, keepdims=True)
        if c["sink"] is not None:
            m = jnp.maximum(m, c["sink"])
        return jnp.exp2(s - m).astype(BF16), m

    def values(c, pm):
        p, m = pm
        vt = c["v"]()
        own = vt[c["half"] * HEAD_DIM:(c["half"] + 1) * HEAD_DIM, :]
        ones = jnp.ones_like(own)
        v_aug = jnp.concatenate([own, ones], axis=0)
        if c["split_values"]:
            cut = max(MXU_DIM, (p.shape[0] // 2) // MXU_DIM * MXU_DIM)
            o = _dot(v_aug[:, :cut], p[:cut]) + _dot(v_aug[:, cut:], p[cut:])
        else:
            o = _dot(v_aug, p)
        den = o[HEAD_DIM:2 * HEAD_DIM, :]
        if c["sink"] is not None:
            den = den + jnp.exp2(c["sink"] - m)
        c["store"](o[0:HEAD_DIM, :] / den)

    _pipeline3(chains, scores, softmax, values, lag=lag)


NA_KEY_ROWS = NA_WIN // GRID_W
NA_Q_ROWS = QBT // GRID_W
NA_ROW_REL = 2 * NA_ROWS - 1


def _na_geometry():
    n_blocks = SEQ // QBT
    out = []
    for variant, block in enumerate((0, 1, n_blocks - 1)):
        start = min(max(block * QBT - QBT, 0), NA_LAST_START)
        assert (block * QBT - start) // QBT == variant
        out.append((start // GRID_W, block * NA_Q_ROWS))
    return out


def _na_kernel(q_ref, k_ref, v_ref, pair_ref, o_ref, tbl_scr):
    @pl.when(pl.program_id(0) == 0)
    def _():
        grid_rows = SEQ // GRID_W
        lo = _low_half((GRID_W, LANES))
        masked = jnp.full((GRID_W, LANES), NEG_INF, F32)
        for variant, (key_row0, query_row0) in enumerate(_na_geometry()):
            for kr in range(NA_KEY_ROWS):
                for pair in range(NA_Q_ROWS // 2):
                    krow = key_row0 + kr
                    rel, ok = [], []
                    for qrow in (query_row0 + 2 * pair, query_row0 + 2 * pair + 1):
                        first = min(max(qrow - NA_ROWS // 2, 0), grid_rows - NA_ROWS)
                        ok.append(first <= krow < first + NA_ROWS)
                        rel.append(krow - qrow + NA_ROWS - 1)
                    for h in range(NA_HEADS):
                        if not (ok[0] or ok[1]):
                            tile = masked
                        else:
                            tile = pair_ref[h, rel[0]]
                            if not ok[1]:
                                tile = jnp.where(lo, tile, masked)
                            elif not ok[0]:
                                tile = jnp.where(lo, masked, tile)
                        tbl_scr[h, variant, kr * GRID_W:(kr + 1) * GRID_W, pair * LANES:(pair + 1) * LANES] = tile

    def store(rows, cols):
        def put(o):
            o_ref[rows, cols] = o.astype(BF16)
        return put

    chains = []
    for j in range(SEQ // QBT):
        q0 = j * QBT
        start = min(max(q0 - QBT, 0), NA_LAST_START)
        variant = (q0 - start) // QBT
        cols = slice(q0, q0 + QBT)
        keys = slice(start, start + NA_WIN)
        for h in range(NA_HEADS):
            rows = slice(h * HEAD_DIM, (h + 1) * HEAD_DIM)
            pair = slice((h // 2) * LANES, (h // 2 + 1) * LANES)
            chains.append(dict(
                half=h % 2, sink=None, split_values=False,
                q=lambda rows=rows, cols=cols: q_ref[rows, cols],
                k=lambda keys=keys, pair=pair: k_ref[keys, pair],
                v=lambda keys=keys, pair=pair: v_ref[pair, keys],
                bias=lambda h=h, variant=variant: tbl_scr[h, variant],
                store=store(rows, cols)))
    _attend_t(chains, lag=NA_LAG)


def _na_pair_table(rpb):
    cols = np.arange(GRID_W)
    col_start = np.clip(cols - NA_COLS // 2, 0, GRID_W - NA_COLS)
    col_ok = (cols[:, None] >= col_start[None, :]) & (cols[:, None] < col_start[None, :] + NA_COLS)
    heads, n_row_rel, n_col_rel = rpb.shape
    assert n_row_rel == NA_ROW_REL
    period = 2 * GRID_W - 1
    lead = GRID_W - NA_COLS
    seq = jnp.pad(rpb.astype(F32)[:, :, ::-1], ((0, 0), (0, 0), (lead, period - lead - n_col_rel)))
    skew = jnp.tile(seq, (1, 1, GRID_W + 1))[..., :GRID_W * 2 * GRID_W]
    toe = skew.reshape(heads, n_row_rel, GRID_W, 2 * GRID_W)[:, :, ::-1, :GRID_W]
    toe = jnp.where(col_ok[None, None], toe * LOG2E, NEG_INF)
    toe = jnp.pad(toe, ((0, 0), (1, 1), (0, 0), (0, 0)), constant_values=NEG_INF)
    return jnp.concatenate([toe[:, 1:], toe[:, :-1]], axis=-1)


def _swa_kernel(layer, slopes_ref, sink_ref, q_ref, k_ref, v_ref, o_ref, bias_scr):
    @pl.when(pl.program_id(0) == 0)
    def _():
        kk = lax.broadcasted_iota(jnp.int32, (SWA_WIN, QBT), 0)
        qq = lax.broadcasted_iota(jnp.int32, (SWA_WIN, QBT), 1)
        dist = jnp.abs(kk - SWA_HALF_WINDOW - qq)
        in_window = dist <= SWA_HALF_WINDOW
        dist_f = dist.astype(F32)
        for h in range(SWA_Q_HEADS):
            bias_scr[h] = jnp.where(in_window, -slopes_ref[h] * dist_f * LOG2E, NEG_INF)

    def store(rows, cols):
        def put(o):
            o_ref[rows, cols] = o.astype(BF16)
        return put

    chains = []
    for j in range(SEQ // QBT):
        q0 = j * QBT
        lo = max(q0 - SWA_HALF_WINDOW, 0)
        hi = min(q0 + QBT + SWA_HALF_WINDOW, SEQ)
        first = lo - (q0 - SWA_HALF_WINDOW)
        cols = slice(q0, q0 + QBT)
        keys = slice(lo, hi)
        win = slice(first, first + hi - lo)
        for h in range(SWA_Q_HEADS):
            rows = slice(h * HEAD_DIM, (h + 1) * HEAD_DIM)
            chains.append(dict(
                half=h // SWA_GROUP, sink=sink_ref[layer, h] * LOG2E, split_values=True,
                q=lambda rows=rows, cols=cols: q_ref[rows, cols],
                k=lambda keys=keys: k_ref[keys, :],
                v=lambda keys=keys: v_ref[:, keys],
                bias=lambda h=h, win=win: bias_scr[h, win, :],
                store=store(rows, cols)))
    _attend_t(chains)


DIL_QB = 128
DIL_WIN = DIL_QB + 2 * DIL_HALF


def _dil_kernel(slopes_ref, q1_ref, k1_ref, v1_ref, q4_ref, k4_ref, v4_ref, q16_ref, k16_ref, v16_ref,
                o_ref, bias_scr, bias16_scr, o_scr, lse_scr, out_scr):
    pair = pl.program_id(1)

    @pl.when(pl.program_id(0) == 0)
    def _():
        slope = [slopes_ref[SWA_Q_HEADS + 2 * pair + h] for h in range(2)]
        qq = lax.broadcasted_iota(jnp.int32, (DIL_QB, DIL_WIN), 0)
        kk = lax.broadcasted_iota(jnp.int32, (DIL_QB, DIL_WIN), 1)
        for shift in range(3):
            dist = jnp.abs(kk - shift * DIL_HALF - qq)
            ok = dist <= DIL_HALF
            dist_f = dist.astype(F32)
            for h in range(2):
                for b, stride in enumerate(DIL_STRIDES[:2]):
                    bias_scr[pair, b, shift, h * DIL_QB:(h + 1) * DIL_QB, :] = jnp.where(
                        ok, -slope[h] * (dist_f * float(stride)) * LOG2E, NEG_INF)
        qq = lax.broadcasted_iota(jnp.int32, (QB, QB), 0)
        kk = lax.broadcasted_iota(jnp.int32, (QB, QB), 1)
        dist = jnp.abs(kk - qq)
        for h in range(2):
            bias16_scr[pair, h * QB:(h + 1) * QB, :] = jnp.where(
                dist <= DIL_HALF, -slope[h] * (dist.astype(F32) * float(DIL_STRIDES[2])) * LOG2E, NEG_INF)

    chains = []
    for b, stride in enumerate(DIL_STRIDES):
        sub = SEQ // stride
        qb = QB if sub == QB else DIL_QB
        for start in range(0, SEQ, qb):
            phase, local_q = divmod(start, sub)
            if sub == QB:
                local_k, n_keys, shift = 0, QB, None
            else:
                local_k = min(max(local_q - DIL_HALF, 0), sub - DIL_WIN)
                n_keys, shift = DIL_WIN, (local_q - local_k) // DIL_HALF
            qrows, krows = slice(local_q, local_q + qb), slice(local_k, local_k + n_keys)
            fine = DIL_STRIDES[1]
            if stride == 1:
                refs = (q1_ref, k1_ref, v1_ref)
                load = lambda ref, rows: ref[rows, :]
                dst = qrows
            elif stride == fine:
                refs = (q4_ref, k4_ref, v4_ref)
                load = lambda ref, rows, phase=phase: ref[0, phase, rows, :]
                dst = slice(start, start + qb)
            else:
                refs = (q16_ref, k16_ref, v16_ref)
                load = lambda ref, rows, phase=phase: ref[0, phase, rows, :]
                coarse = stride // fine
                dst = pl.ds((phase % fine) * (SEQ // fine) + local_q * coarse + phase // fine, qb, stride=coarse)
            chains.append(dict(b=b, shift=shift, dst=dst, qb=qb, order=start + qb,
                               q=lambda refs=refs, load=load, rows=qrows: load(refs[0], rows),
                               k=lambda refs=refs, load=load, rows=krows: load(refs[1], rows),
                               v=lambda refs=refs, load=load, rows=krows: load(refs[2], rows)))
    chains.sort(key=lambda c: c["order"])

    def scores(c):
        q, kw = c["q"](), c["k"]()
        lo = _low_half(q.shape)
        zeros = jnp.zeros_like(q)
        q_both = jnp.concatenate([jnp.where(lo, q, zeros), jnp.where(lo, zeros, q)], axis=0)
        bias = bias16_scr[pair] if c["shift"] is None else bias_scr[pair, c["b"], c["shift"]]
        return _dot_nt(q_both, kw) + bias

    def softmax(c, s):
        return _softmax_block(s)

    def values(c, mp):
        m, p = mp
        vw = c["v"]()
        res = _dot(p, jnp.concatenate([vw, jnp.ones_like(vw)], axis=1))
        qb = c["qb"]
        lo = _low_half((qb, LANES))
        den = jnp.where(lo, res[0:qb, LANES:2 * LANES], res[qb:2 * qb, LANES:2 * LANES])
        o_scr[c["b"], c["dst"], :] = jnp.where(lo, res[0:qb, 0:LANES], res[qb:2 * qb, 0:LANES]) / den
        lse_scr[c["b"], c["dst"], :] = jnp.where(lo, m[0:qb], m[qb:2 * qb]) + jnp.log2(den)

    _pipeline3(chains, scores, softmax, values, lag=DIL_LAG)

    fine = DIL_STRIDES[1]
    per_phase = SEQ // fine

    def merge(i, carry):
        start = pl.multiple_of(i * PREP_ROWS, PREP_ROWS)
        rows = pl.ds(start, PREP_ROWS)
        natural = pl.ds((start % per_phase) * fine + start // per_phase, PREP_ROWS, stride=fine)
        lse = [lse_scr[0, natural, :], lse_scr[1, rows, :], lse_scr[2, rows, :]]
        outs = [o_scr[0, natural, :], o_scr[1, rows, :], o_scr[2, rows, :]]
        top = jnp.maximum(jnp.maximum(lse[0], lse[1]), lse[2])
        num = jnp.zeros((PREP_ROWS, LANES), F32)
        den = jnp.zeros((PREP_ROWS, LANES), F32)
        for b in range(3):
            w = jnp.exp2(lse[b] - top)
            num = num + w * outs[b]
            den = den + w
        out_scr[natural, :] = num / den
        return carry

    lax.fori_loop(0, SEQ // PREP_ROWS, merge, 0)

    def emit(i, carry):
        rows = pl.ds(pl.multiple_of(i * PREP_ROWS, PREP_ROWS), PREP_ROWS)
        o_ref[rows, :] = out_scr[rows, :].astype(BF16)
        return carry

    lax.fori_loop(0, SEQ // PREP_ROWS, emit, 0)


def _params(semantics):
    return pltpu.CompilerParams(dimension_semantics=semantics, vmem_limit_bytes=VMEM_LIMIT)


def _resident(shape, layer):
    return pl.BlockSpec((None,) + tuple(shape), lambda *_: (layer,) + (0,) * len(shape),
                        pipeline_mode=pl.Buffered(1))


def _smem():
    return pl.BlockSpec(memory_space=pltpu.SMEM)


def _proj(x, layer, gain, w_t, w_r, gain_q, gain_r):
    t = x.shape[0]
    rows = PROJ_ROW_TILE
    feat = lambda n: pl.BlockSpec((n, rows), lambda i: (0, i))
    tok = lambda n: pl.BlockSpec((rows, n), lambda i: (i, 0))
    tiles_per_seq = SEQ // rows

    def phased(stride):
        return pl.BlockSpec((1, stride, rows // stride, 3 * DIL_W),
                            lambda i: (i // tiles_per_seq, 0, i % tiles_per_seq, 0))

    def phased_shape(stride):
        return jax.ShapeDtypeStruct((t // SEQ, stride, SEQ // stride, 3 * DIL_W), BF16)

    return pl.pallas_call(
        _proj_kernel,
        out_shape=[jax.ShapeDtypeStruct((2 * NA_W, t), BF16), jax.ShapeDtypeStruct((SWA_QW + SWA_KVW, t), BF16),
                   jax.ShapeDtypeStruct((t, PROJ_R_K), BF16), jax.ShapeDtypeStruct((t, 3 * DIL_W), BF16),
                   phased_shape(DIL_STRIDES[1]), phased_shape(DIL_STRIDES[2])],
        grid=(t // rows,),
        in_specs=[tok(D_MODEL), _resident((1, D_MODEL), layer),
                  _resident((PROJ_T_Q + PROJ_T_V, D_MODEL), layer), _resident((D_MODEL, PROJ_R), layer),
                  _resident((PROJ_T_Q, 1), layer), _resident((1, PROJ_R_K + 2 * DIL_W), layer)],
        out_specs=[feat(2 * NA_W), feat(SWA_QW + SWA_KVW), tok(PROJ_R_K), tok(3 * DIL_W),
                   phased(DIL_STRIDES[1]), phased(DIL_STRIDES[2])],
        scratch_shapes=[pltpu.VMEM((3 * DIL_W // LANES, 2, rows, LANES), F32)],
        compiler_params=_params(("parallel",)),
        name="proj",
    )(x, gain, w_t, w_r, gain_q, gain_r)


def _na(qv_t, keys, layer, table):
    t = qv_t.shape[1]
    feat = pl.BlockSpec((NA_W, SEQ), lambda b: (0, b))
    return pl.pallas_call(
        _na_kernel,
        out_shape=jax.ShapeDtypeStruct((NA_W, t), BF16),
        grid=(t // SEQ,),
        in_specs=[feat, pl.BlockSpec((SEQ, NA_W), lambda b: (b, 0)),
                  pl.BlockSpec((NA_W, SEQ), lambda b: (1, b)),
                  _resident((NA_HEADS, NA_ROW_REL + 1, GRID_W, 2 * GRID_W), layer)],
        out_specs=feat,
        scratch_shapes=[pltpu.VMEM((NA_HEADS, NA_VARIANTS, NA_WIN, QBT), F32)],
        compiler_params=_params(("arbitrary",)),
        name="na",
    )(qv_t, keys, qv_t, table)


def _swa(qv_t, keys, layer, slopes, sink):
    t = qv_t.shape[1]
    return pl.pallas_call(
        functools.partial(_swa_kernel, layer),
        out_shape=jax.ShapeDtypeStruct((SWA_QW, t), BF16),
        grid=(t // SEQ,),
        in_specs=[_smem(), _smem(),
                  pl.BlockSpec((SWA_QW, SEQ), lambda b: (0, b)),
                  pl.BlockSpec((SEQ, SWA_KVW), lambda b: (b, NA_W // SWA_KVW)),
                  pl.BlockSpec((SWA_KVW, SEQ), lambda b: (SWA_QW // SWA_KVW, b))],
        out_specs=pl.BlockSpec((SWA_QW, SEQ), lambda b: (0, b)),
        scratch_shapes=[pltpu.VMEM((SWA_Q_HEADS, SWA_WIN, QBT), F32)],
        compiler_params=_params(("arbitrary",)),
        name="swa",
    )(slopes, sink, qv_t, keys, qv_t)


def _dil(qkv, qkv4, qkv16, slopes):
    t = qkv.shape[0]
    pairs = DIL_W // LANES
    seq_block = lambda first: pl.BlockSpec((SEQ, LANES), lambda b, p: (b, first + p))

    def phased(stride, first):
        return pl.BlockSpec((1, stride, SEQ // stride, LANES), lambda b, p: (b, 0, 0, first + p))

    operands, specs = [], []
    for arr, spec in ((qkv, seq_block), (qkv4, lambda f: phased(DIL_STRIDES[1], f)),
                      (qkv16, lambda f: phased(DIL_STRIDES[2], f))):
        for part in range(3):
            operands.append(arr)
            specs.append(spec(part * pairs))
    return pl.pallas_call(
        _dil_kernel,
        out_shape=jax.ShapeDtypeStruct((t, DIL_W), BF16),
        grid=(t // SEQ, pairs),
        in_specs=[_smem()] + specs,
        out_specs=seq_block(0),
        scratch_shapes=[pltpu.VMEM((pairs, 2, 3, 2 * DIL_QB, DIL_WIN), F32),
                        pltpu.VMEM((pairs, 2 * QB, QB), F32),
                        pltpu.VMEM((3, SEQ, LANES), F32),
                        pltpu.VMEM((3, SEQ, LANES), F32),
                        pltpu.VMEM((SEQ, LANES), F32)],
        compiler_params=_params(("arbitrary", "arbitrary")),
        name="dil",
    )(slopes, *operands)


def _out_ffn(ya_t, yb_t, yc, x, layer, gain_ab, gain_c, w_out, gain_ffn, w_gu, w_down):
    t = x.shape[0]
    feat = lambda n: pl.BlockSpec((n, ROW_TILE), lambda i: (0, i))
    tok = lambda n: pl.BlockSpec((ROW_TILE, n), lambda i: (i, 0))
    return pl.pallas_call(
        _out_ffn_kernel,
        out_shape=jax.ShapeDtypeStruct((t, D_MODEL), F32),
        grid=(t // ROW_TILE,),
        in_specs=[feat(NA_W), feat(SWA_QW), tok(DIL_W), tok(D_MODEL),
                  _resident((NA_W + SWA_QW, 1), layer), _resident((1, DIL_W), layer),
                  _resident((MIX_W, D_MODEL), layer), _resident((1, D_MODEL), layer),
                  _resident((D_MODEL, 2 * FFN_HIDDEN), layer), _resident((FFN_HIDDEN, D_MODEL), layer)],
        out_specs=tok(D_MODEL),
        scratch_shapes=[pltpu.VMEM((ROW_TILE, D_MODEL), BF16)],
        compiler_params=_params(("parallel",)),
        name="out_ffn",
    )(ya_t, yb_t, yc, x, gain_ab, gain_c, w_out, gain_ffn, w_gu, w_down)


def _split_w_in(w):
    cuts = np.cumsum([NA_W, NA_W, NA_W, SWA_QW, SWA_KVW, SWA_KVW, DIL_W, DIL_W])
    qa, ka, va, qb, kb, vb, qc, kc, vc = jnp.split(w, [int(c) for c in cuts], axis=2)
    w_t = jnp.concatenate([qa, qb, va, vb], axis=2).astype(BF16).transpose(0, 2, 1)
    w_r = jnp.concatenate([ka, kb, qc, kc, vc], axis=2).astype(BF16)
    return w_t, w_r


def kernel(x, attn_norm, w_in, qk_gain, rpb, sink, out_gain, w_out, ffn_norm, w_gu, w_down):
    batch, seq, d = x.shape
    assert (seq, d) == (SEQ, D_MODEL)
    depth = w_in.shape[0]
    n_slopes = SWA_Q_HEADS + DIL_HEADS
    slopes = 2.0 ** (-8.0 * (jnp.arange(n_slopes, dtype=F32) + 1.0) / n_slopes)
    xf = x.reshape(batch * seq, d)

    g = qk_gain.astype(F32)
    tile = lambda v, n: jnp.tile(v, (1, n))
    gain_q = jnp.concatenate([tile(g[:, 0, 0], NA_HEADS), tile(g[:, 1, 0], SWA_Q_HEADS)], axis=1)[:, :, None]
    gain_r = jnp.concatenate([tile(g[:, 0, 1], NA_HEADS), tile(g[:, 1, 1], SWA_KV_HEADS),
                              tile(g[:, 2, 0], DIL_HEADS), tile(g[:, 2, 1], DIL_HEADS)], axis=1)[:, None, :]
    w_t, w_r = _split_w_in(w_in)
    table = _na_pair_table(rpb.reshape((depth * NA_HEADS,) + rpb.shape[2:]))
    table = table.reshape((depth, NA_HEADS) + table.shape[1:])
    og = out_gain.astype(F32)
    gain_ab, gain_c = og[:, :NA_W + SWA_QW, None], og[:, None, NA_W + SWA_QW:]
    w_out_b, w_gu_b, w_down_b = w_out.astype(BF16), w_gu.astype(BF16), w_down.astype(BF16)
    attn_gain, ffn_gain = attn_norm.astype(F32)[:, None, :], ffn_norm.astype(F32)[:, None, :]
    sink = sink.astype(F32)

    for l in range(depth):
        qva_t, qvb_t, keys, dil1, dil4, dil16 = _proj(xf, l, attn_gain, w_t, w_r, gain_q, gain_r)
        ya_t = _na(qva_t, keys, l, table)
        yb_t = _swa(qvb_t, keys, l, slopes, sink)
        yc = _dil(dil1, dil4, dil16, slopes)
        xf = _out_ffn(ya_t, yb_t, yc, xf, l, gain_ab, gain_c, w_out_b, ffn_gain, w_gu_b, w_down_b)
    return xf.reshape(batch, seq, d)
```

```python
import functools
import math

import numpy as np
import jax
import jax.numpy as jnp
from jax import lax
from jax.experimental import pallas as pl
from jax.experimental.pallas import tpu as pltpu

D_MODEL = 1024
SEQ = 2048
HEAD_DIM = 64
GRID_W = 64
NA_HEADS = 4
NA_ROWS = 8
NA_COLS = 16
SWA_Q_HEADS = 6
SWA_KV_HEADS = 2
SWA_GROUP = SWA_Q_HEADS // SWA_KV_HEADS
SWA_HALF_WINDOW = 128
DIL_HEADS = 6
DIL_STRIDES = (1, 4, 16)
DIL_HALF = 64
NA_W = NA_HEADS * HEAD_DIM
SWA_QW = SWA_Q_HEADS * HEAD_DIM
SWA_KVW = SWA_KV_HEADS * HEAD_DIM
DIL_W = DIL_HEADS * HEAD_DIM
IN_W = 3 * NA_W + SWA_QW + 2 * SWA_KVW + 3 * DIL_W
MIX_W = NA_W + SWA_QW + DIL_W
FFN_HIDDEN = 2816
NORM_EPS = 1e-6
NEG_INF = -1e30
QK_SCALE = HEAD_DIM ** -0.5
LOG2E = math.log2(math.e)
Q_SCALE = QK_SCALE * LOG2E

LANES = 128
VMEM_LIMIT = 56 * 1024 * 1024

QB = 128
QBT = 256
PREP_ROWS = 256
ROW_TILE = 1024
PROJ_ROW_TILE = 1024
MXU_DIM = 256
FFN_H_CHUNKS = (3 * MXU_DIM,) * 3 + (2 * MXU_DIM,)
assert sum(FFN_H_CHUNKS) == FFN_HIDDEN
NA_LAG = 2
DIL_LAG = 1

PROJ_T_Q = NA_W + SWA_QW
PROJ_T_V = NA_W + SWA_KVW
PROJ_R_K = NA_W + SWA_KVW
PROJ_R = PROJ_R_K + 3 * DIL_W
PROJ_R_CHUNK = 768

NA_WIN = 768
NA_VARIANTS = 3
NA_LAST_START = SEQ - NA_WIN
SWA_WIN = QBT + 2 * SWA_HALF_WINDOW

F32 = jnp.float32
BF16 = jnp.bfloat16


def _dot(a, b):
    return jnp.dot(a, b, preferred_element_type=F32)


def _dot_nt(a, b):
    return lax.dot_general(a, b, (((1,), (1,)), ((), ())), preferred_element_type=F32)


def _dot_tn(a, b):
    return lax.dot_general(a, b, (((0,), (0,)), ((), ())), preferred_element_type=F32)


def _low_half(shape):
    return lax.broadcasted_iota(jnp.int32, shape, len(shape) - 1) < HEAD_DIM


def _pair_rms(x, gain):
    lo = _low_half(x.shape)
    sq = x * x
    s_lo = jnp.sum(jnp.where(lo, sq, 0.0), axis=-1, keepdims=True)
    s_hi = jnp.sum(jnp.where(lo, 0.0, sq), axis=-1, keepdims=True)
    ms = jnp.where(lo, s_lo, s_hi) * (1.0 / HEAD_DIM)
    return x * lax.rsqrt(ms + NORM_EPS) * gain


def _softmax_block(s):
    m = jnp.max(s, axis=-1, keepdims=True)
    return m, jnp.exp2(s - m).astype(BF16)


def _proj_kernel(x_ref, g_ref, wt_ref, wr_ref, gq_ref, gr_ref,
                 qva_ref, qvb_ref, k_ref, dil_ref, dil4_ref, dil16_ref, regroup_scr):
    x = x_ref[...]
    ms = jnp.mean(x * x, axis=-1, keepdims=True)
    h = (x * lax.rsqrt(ms + NORM_EPS) * g_ref[...]).astype(BF16)
    tm = h.shape[0]

    def finish_q(qt):
        q3 = qt.reshape(PROJ_T_Q // HEAD_DIM, HEAD_DIM, tm)
        q3 = q3 * lax.rsqrt(jnp.mean(q3 * q3, axis=1, keepdims=True) + NORM_EPS)
        qt = q3.reshape(PROJ_T_Q, tm) * gq_ref[...] * Q_SCALE
        qva_ref[0:NA_W, :] = qt[0:NA_W].astype(BF16)
        qvb_ref[0:SWA_QW, :] = qt[NA_W:PROJ_T_Q].astype(BF16)

    def finish_v(vt):
        qva_ref[NA_W:2 * NA_W, :] = vt[0:NA_W].astype(BF16)
        qvb_ref[SWA_QW:SWA_QW + SWA_KVW, :] = vt[NA_W:PROJ_T_V].astype(BF16)

    n_norm = (PROJ_R_K + 2 * DIL_W) // LANES
    n_key = PROJ_R_K // LANES
    n_dil_q = DIL_W // LANES

    def finish_r(c, r):
        fine, coarse = DIL_STRIDES[1], DIL_STRIDES[2] // DIL_STRIDES[1]
        per_fine = tm // fine
        dil_tiles = []
        for t in range(PROJ_R_CHUNK // LANES):
            blk = c * (PROJ_R_CHUNK // LANES) + t
            tile = r[:, t * LANES:(t + 1) * LANES]
            if blk < n_norm:
                tile = _pair_rms(tile, gr_ref[:, blk * LANES:(blk + 1) * LANES])
            if n_key <= blk < n_key + n_dil_q:
                tile = tile * Q_SCALE
            if blk < n_key:
                k_ref[:, blk * LANES:(blk + 1) * LANES] = tile.astype(BF16)
                continue
            d = blk - n_key
            dil_ref[:, d * LANES:(d + 1) * LANES] = tile.astype(BF16)
            regroup_scr[d, 0] = tile
            dil_tiles.append(d)
        for d in dil_tiles:
            for ph in range(fine):
                part = regroup_scr[d, 0, pl.ds(ph, per_fine, stride=fine), :]
                dil4_ref[0, ph, :, d * LANES:(d + 1) * LANES] = part.astype(BF16)
                regroup_scr[d, 1, ph * per_fine:(ph + 1) * per_fine, :] = part
        for d in dil_tiles:
            for ph in range(fine):
                for sub_ph in range(coarse):
                    part = regroup_scr[d, 1, pl.ds(ph * per_fine + sub_ph, per_fine // coarse, stride=coarse), :]
                    dil16_ref[0, ph + fine * sub_ph, :, d * LANES:(d + 1) * LANES] = part.astype(BF16)

    finish_q(_dot_nt(wt_ref[0:PROJ_T_Q, :], h))
    finish_v(_dot_nt(wt_ref[PROJ_T_Q:PROJ_T_Q + PROJ_T_V, :], h))
    for c in range(PROJ_R // PROJ_R_CHUNK):
        finish_r(c, _dot(h, wr_ref[:, c * PROJ_R_CHUNK:(c + 1) * PROJ_R_CHUNK]))


def _group_rms_t(yt, gain_col):
    yf = yt.astype(F32)
    ms = jnp.mean(yf * yf, axis=0, keepdims=True)
    return (yf * lax.rsqrt(ms + NORM_EPS) * gain_col).astype(BF16)


def _out_ffn_kernel(ya_ref, yb_ref, yc_ref, x_ref, gab_ref, gc_ref, wo_ref, gf_ref, wgu_ref, wd_ref,
                    o_ref, h_scr):
    ya = _group_rms_t(ya_ref[...], gab_ref[0:NA_W, :])
    yb = _group_rms_t(yb_ref[...], gab_ref[NA_W:NA_W + SWA_QW, :])
    yc = yc_ref[...].astype(F32)
    ms = jnp.mean(yc * yc, axis=-1, keepdims=True)
    yc = (yc * lax.rsqrt(ms + NORM_EPS) * gc_ref[...]).astype(BF16)
    acc = _dot_tn(ya, wo_ref[0:NA_W, :])
    acc = acc + _dot_tn(yb, wo_ref[NA_W:NA_W + SWA_QW, :])
    acc = acc + _dot(yc, wo_ref[NA_W + SWA_QW:MIX_W, :])
    x = x_ref[...] + acc

    ms = jnp.mean(x * x, axis=-1, keepdims=True)
    h_scr[...] = (x * lax.rsqrt(ms + NORM_EPS) * gf_ref[...]).astype(BF16)
    o_ref[...] = x
    lo = 0
    for width in FFN_H_CHUNKS:
        gate = _dot(h_scr[...], wgu_ref[:, lo:lo + width])
        up = _dot(h_scr[...], wgu_ref[:, FFN_HIDDEN + lo:FFN_HIDDEN + lo + width])
        act = (gate / (1.0 + jnp.exp(-gate)) * up).astype(BF16)
        o_ref[...] += _dot(act, wd_ref[lo:lo + width, :])
        lo += width


def _pipeline3(chains, stage_a, stage_b, stage_c, lag=1):
    n = len(chains)
    a, b = {}, {}
    for i in range(-2 * lag, n):
        if 0 <= i + 2 * lag < n:
            a[i + 2 * lag] = stage_a(chains[i + 2 * lag])
        if 0 <= i + lag < n:
            b[i + lag] = stage_b(chains[i + lag], a.pop(i + lag))
        if i >= 0:
            stage_c(chains[i], b.pop(i))


def _attend_t(chains, lag=1):
    def scores(c):
        qt = c["q"]()
        zeros = jnp.zeros_like(qt)
        qpad = jnp.concatenate([qt, zeros] if c["half"] == 0 else [zeros, qt], axis=0)
        kw = c["k"]()
        mid = kw.shape[0] // 2
        s = jnp.concatenate([_dot(kw[:mid], qpad), _dot(kw[mid:], qpad)], axis=0)
        return s + c["bias"]()

    def softmax(c, s):
        m = jnp.max(s, axis=0, keepdims=True)
        if c["sink"] is not None:
            m = jnp.maximum(m, c["sink"])
        return jnp.exp2(s - m).astype(BF16), m

    def values(c, pm):
        p, m = pm
        vt = c["v"]()
        own = vt[c["half"] * HEAD_DIM:(c["half"] + 1) * HEAD_DIM, :]
        ones = jnp.ones_like(own)
        v_aug = jnp.concatenate([own, ones], axis=0)
        if c["split_values"]:
            cut = max(MXU_DIM, (p.shape[0] // 2) // MXU_DIM * MXU_DIM)
            o = _dot(v_aug[:, :cut], p[:cut]) + _dot(v_aug[:, cut:], p[cut:])
        else:
            o = _dot(v_aug, p)
        den = o[HEAD_DIM:2 * HEAD_DIM, :]
        if c["sink"] is not None:
            den = den + jnp.exp2(c["sink"] - m)
        c["store"](o[0:HEAD_DIM, :] / den)

    _pipeline3(chains, scores, softmax, values, lag=lag)


NA_KEY_ROWS = NA_WIN // GRID_W
NA_Q_ROWS = QBT // GRID_W
NA_ROW_REL = 2 * NA_ROWS - 1


def _na_geometry():
    n_blocks = SEQ // QBT
    out = []
    for variant, block in enumerate((0, 1, n_blocks - 1)):
        start = min(max(block * QBT - QBT, 0), NA_LAST_START)
        assert (block * QBT - start) // QBT == variant
        out.append((start // GRID_W, block * NA_Q_ROWS))
    return out


def _na_kernel(q_ref, k_ref, v_ref, pair_ref, o_ref, tbl_scr):
    @pl.when(pl.program_id(0) == 0)
    def _():
        grid_rows = SEQ // GRID_W
        lo = _low_half((GRID_W, LANES))
        masked = jnp.full((GRID_W, LANES), NEG_INF, F32)
        for variant, (key_row0, query_row0) in enumerate(_na_geometry()):
            for kr in range(NA_KEY_ROWS):
                for pair in range(NA_Q_ROWS // 2):
                    krow = key_row0 + kr
                    rel, ok = [], []
                    for qrow in (query_row0 + 2 * pair, query_row0 + 2 * pair + 1):
                        first = min(max(qrow - NA_ROWS // 2, 0), grid_rows - NA_ROWS)
                        ok.append(first <= krow < first + NA_ROWS)
                        rel.append(krow - qrow + NA_ROWS - 1)
                    for h in range(NA_HEADS):
                        if not (ok[0] or ok[1]):
                            tile = masked
                        else:
                            tile = pair_ref[h, rel[0]]
                            if not ok[1]:
                                tile = jnp.where(lo, tile, masked)
                            elif not ok[0]:
                                tile = jnp.where(lo, masked, tile)
                        tbl_scr[h, variant, kr * GRID_W:(kr + 1) * GRID_W, pair * LANES:(pair + 1) * LANES] = tile

    def store(rows, cols):
        def put(o):
            o_ref[rows, cols] = o.astype(BF16)
        return put

    chains = []
    for j in range(SEQ // QBT):
        q0 = j * QBT
        start = min(max(q0 - QBT, 0), NA_LAST_START)
        variant = (q0 - start) // QBT
        cols = slice(q0, q0 + QBT)
        keys = slice(start, start + NA_WIN)
        for h in range(NA_HEADS):
            rows = slice(h * HEAD_DIM, (h + 1) * HEAD_DIM)
            pair = slice((h // 2) * LANES, (h // 2 + 1) * LANES)
            chains.append(dict(
                half=h % 2, sink=None, split_values=False,
                q=lambda rows=rows, cols=cols: q_ref[rows, cols],
                k=lambda keys=keys, pair=pair: k_ref[keys, pair],
                v=lambda keys=keys, pair=pair: v_ref[pair, keys],
                bias=lambda h=h, variant=variant: tbl_scr[h, variant],
                store=store(rows, cols)))
    _attend_t(chains, lag=NA_LAG)


def _na_pair_table(rpb):
    cols = np.arange(GRID_W)
    col_start = np.clip(cols - NA_COLS // 2, 0, GRID_W - NA_COLS)
    col_ok = (cols[:, None] >= col_start[None, :]) & (cols[:, None] < col_start[None, :] + NA_COLS)
    heads, n_row_rel, n_col_rel = rpb.shape
    assert n_row_rel == NA_ROW_REL
    period = 2 * GRID_W - 1
    lead = GRID_W - NA_COLS
    seq = jnp.pad(rpb.astype(F32)[:, :, ::-1], ((0, 0), (0, 0), (lead, period - lead - n_col_rel)))
    skew = jnp.tile(seq, (1, 1, GRID_W + 1))[..., :GRID_W * 2 * GRID_W]
    toe = skew.reshape(heads, n_row_rel, GRID_W, 2 * GRID_W)[:, :, ::-1, :GRID_W]
    toe = jnp.where(col_ok[None, None], toe * LOG2E, NEG_INF)
    toe = jnp.pad(toe, ((0, 0), (1, 1), (0, 0), (0, 0)), constant_values=NEG_INF)
    return jnp.concatenate([toe[:, 1:], toe[:, :-1]], axis=-1)


def _swa_kernel(layer, slopes_ref, sink_ref, q_ref, k_ref, v_ref, o_ref, bias_scr):
    @pl.when(pl.program_id(0) == 0)
    def _():
        kk = lax.broadcasted_iota(jnp.int32, (SWA_WIN, QBT), 0)
        qq = lax.broadcasted_iota(jnp.int32, (SWA_WIN, QBT), 1)
        dist = jnp.abs(kk - SWA_HALF_WINDOW - qq)
        in_window = dist <= SWA_HALF_WINDOW
        dist_f = dist.astype(F32)
        for h in range(SWA_Q_HEADS):
            bias_scr[h] = jnp.where(in_window, -slopes_ref[h] * dist_f * LOG2E, NEG_INF)

    def store(rows, cols):
        def put(o):
            o_ref[rows, cols] = o.astype(BF16)
        return put

    chains = []
    for j in range(SEQ // QBT):
        q0 = j * QBT
        lo = max(q0 - SWA_HALF_WINDOW, 0)
        hi = min(q0 + QBT + SWA_HALF_WINDOW, SEQ)
        first = lo - (q0 - SWA_HALF_WINDOW)
        cols = slice(q0, q0 + QBT)
        keys = slice(lo, hi)
        win = slice(first, first + hi - lo)
        for h in range(SWA_Q_HEADS):
            rows = slice(h * HEAD_DIM, (h + 1) * HEAD_DIM)
            chains.append(dict(
                half=h // SWA_GROUP, sink=sink_ref[layer, h] * LOG2E, split_values=True,
                q=lambda rows=rows, cols=cols: q_ref[rows, cols],
                k=lambda keys=keys: k_ref[keys, :],
                v=lambda keys=keys: v_ref[:, keys],
                bias=lambda h=h, win=win: bias_scr[h, win, :],
                store=store(rows, cols)))
    _attend_t(chains)


DIL_QB = 128
DIL_WIN = DIL_QB + 2 * DIL_HALF


def _dil_kernel(slopes_ref, q1_ref, k1_ref, v1_ref, q4_ref, k4_ref, v4_ref, q16_ref, k16_ref, v16_ref,
                o_ref, bias_scr, bias16_scr, o_scr, lse_scr, out_scr):
    pair = pl.program_id(1)

    @pl.when(pl.program_id(0) == 0)
    def _():
        slope = [slopes_ref[SWA_Q_HEADS + 2 * pair + h] for h in range(2)]
        qq = lax.broadcasted_iota(jnp.int32, (DIL_QB, DIL_WIN), 0)
        kk = lax.broadcasted_iota(jnp.int32, (DIL_QB, DIL_WIN), 1)
        for shift in range(3):
            dist = jnp.abs(kk - shift * DIL_HALF - qq)
            ok = dist <= DIL_HALF
            dist_f = dist.astype(F32)
            for h in range(2):
                for b, stride in enumerate(DIL_STRIDES[:2]):
                    bias_scr[pair, b, shift, h * DIL_QB:(h + 1) * DIL_QB, :] = jnp.where(
                        ok, -slope[h] * (dist_f * float(stride)) * LOG2E, NEG_INF)
        qq = lax.broadcasted_iota(jnp.int32, (QB, QB), 0)
        kk = lax.broadcasted_iota(jnp.int32, (QB, QB), 1)
        dist = jnp.abs(kk - qq)
        for h in range(2):
            bias16_scr[pair, h * QB:(h + 1) * QB, :] = jnp.where(
                dist <= DIL_HALF, -slope[h] * (dist.astype(F32) * float(DIL_STRIDES[2])) * LOG2E, NEG_INF)

    chains = []
    for b, stride in enumerate(DIL_STRIDES):
        sub = SEQ // stride
        qb = QB if sub == QB else DIL_QB
        for start in range(0, SEQ, qb):
            phase, local_q = divmod(start, sub)
            if sub == QB:
                local_k, n_keys, shift = 0, QB, None
            else:
                local_k = min(max(local_q - DIL_HALF, 0), sub - DIL_WIN)
                n_keys, shift = DIL_WIN, (local_q - local_k) // DIL_HALF
            qrows, krows = slice(local_q, local_q + qb), slice(local_k, local_k + n_keys)
            fine = DIL_STRIDES[1]
            if stride == 1:
                refs = (q1_ref, k1_ref, v1_ref)
                load = lambda ref, rows: ref[rows, :]
                dst = qrows
            elif stride == fine:
                refs = (q4_ref, k4_ref, v4_ref)
                load = lambda ref, rows, phase=phase: ref[0, phase, rows, :]
                dst = slice(start, start + qb)
            else:
                refs = (q16_ref, k16_ref, v16_ref)
                load = lambda ref, rows, phase=phase: ref[0, phase, rows, :]
                coarse = stride // fine
                dst = pl.ds((phase % fine) * (SEQ // fine) + local_q * coarse + phase // fine, qb, stride=coarse)
            chains.append(dict(b=b, shift=shift, dst=dst, qb=qb, order=start + qb,
                               q=lambda refs=refs, load=load, rows=qrows: load(refs[0], rows),
                               k=lambda refs=refs, load=load, rows=krows: load(refs[1], rows),
                               v=lambda refs=refs, load=load, rows=krows: load(refs[2], rows)))
    chains.sort(key=lambda c: c["order"])

    def scores(c):
        q, kw = c["q"](), c["k"]()
        lo = _low_half(q.shape)
        zeros = jnp.zeros_like(q)
        q_both = jnp.concatenate([jnp.where(lo, q, zeros), jnp.where(lo, zeros, q)], axis=0)
        bias = bias16_scr[pair] if c["shift"] is None else bias_scr[pair, c["b"], c["shift"]]
        return _dot_nt(q_both, kw) + bias

    def softmax(c, s):
        return _softmax_block(s)

    def values(c, mp):
        m, p = mp
        vw = c["v"]()
        res = _dot(p, jnp.concatenate([vw, jnp.ones_like(vw)], axis=1))
        qb = c["qb"]
        lo = _low_half((qb, LANES))
        den = jnp.where(lo, res[0:qb, LANES:2 * LANES], res[qb:2 * qb, LANES:2 * LANES])
        o_scr[c["b"], c["dst"], :] = jnp.where(lo, res[0:qb, 0:LANES], res[qb:2 * qb, 0:LANES]) / den
        lse_scr[c["b"], c["dst"], :] = jnp.where(lo, m[0:qb], m[qb:2 * qb]) + jnp.log2(den)

    _pipeline3(chains, scores, softmax, values, lag=DIL_LAG)

    fine = DIL_STRIDES[1]
    per_phase = SEQ // fine

    def merge(i, carry):
        start = pl.multiple_of(i * PREP_ROWS, PREP_ROWS)
        rows = pl.ds(start, PREP_ROWS)
        natural = pl.ds((start % per_phase) * fine + start // per_phase, PREP_ROWS, stride=fine)
        lse = [lse_scr[0, natural, :], lse_scr[1, rows, :], lse_scr[2, rows, :]]
        outs = [o_scr[0, natural, :], o_scr[1, rows, :], o_scr[2, rows, :]]
        top = jnp.maximum(jnp.maximum(lse[0], lse[1]), lse[2])
        num = jnp.zeros((PREP_ROWS, LANES), F32)
        den = jnp.zeros((PREP_ROWS, LANES), F32)
        for b in range(3):
            w = jnp.exp2(lse[b] - top)
            num = num + w * outs[b]
            den = den + w
        out_scr[natural, :] = num / den
        return carry

    lax.fori_loop(0, SEQ // PREP_ROWS, merge, 0)

    def emit(i, carry):
        rows = pl.ds(pl.multiple_of(i * PREP_ROWS, PREP_ROWS), PREP_ROWS)
        o_ref[rows, :] = out_scr[rows, :].astype(BF16)
        return carry

    lax.fori_loop(0, SEQ // PREP_ROWS, emit, 0)


def _params(semantics):
    return pltpu.CompilerParams(dimension_semantics=semantics, vmem_limit_bytes=VMEM_LIMIT)


def _resident(shape, layer):
    return pl.BlockSpec((None,) + tuple(shape), lambda *_: (layer,) + (0,) * len(shape),
                        pipeline_mode=pl.Buffered(1))


def _smem():
    return pl.BlockSpec(memory_space=pltpu.SMEM)


def _proj(x, layer, gain, w_t, w_r, gain_q, gain_r):
    t = x.shape[0]
    rows = PROJ_ROW_TILE
    feat = lambda n: pl.BlockSpec((n, rows), lambda i: (0, i))
    tok = lambda n: pl.BlockSpec((rows, n), lambda i: (i, 0))
    tiles_per_seq = SEQ // rows

    def phased(stride):
        return pl.BlockSpec((1, stride, rows // stride, 3 * DIL_W),
                            lambda i: (i // tiles_per_seq, 0, i % tiles_per_seq, 0))

    def phased_shape(stride):
        return jax.ShapeDtypeStruct((t // SEQ, stride, SEQ // stride, 3 * DIL_W), BF16)

    return pl.pallas_call(
        _proj_kernel,
        out_shape=[jax.ShapeDtypeStruct((2 * NA_W, t), BF16), jax.ShapeDtypeStruct((SWA_QW + SWA_KVW, t), BF16),
                   jax.ShapeDtypeStruct((t, PROJ_R_K), BF16), jax.ShapeDtypeStruct((t, 3 * DIL_W), BF16),
                   phased_shape(DIL_STRIDES[1]), phased_shape(DIL_STRIDES[2])],
        grid=(t // rows,),
        in_specs=[tok(D_MODEL), _resident((1, D_MODEL), layer),
                  _resident((PROJ_T_Q + PROJ_T_V, D_MODEL), layer), _resident((D_MODEL, PROJ_R), layer),
                  _resident((PROJ_T_Q, 1), layer), _resident((1, PROJ_R_K + 2 * DIL_W), layer)],
        out_specs=[feat(2 * NA_W), feat(SWA_QW + SWA_KVW), tok(PROJ_R_K), tok(3 * DIL_W),
                   phased(DIL_STRIDES[1]), phased(DIL_STRIDES[2])],
        scratch_shapes=[pltpu.VMEM((3 * DIL_W // LANES, 2, rows, LANES), F32)],
        compiler_params=_params(("parallel",)),
        name="proj",
    )(x, gain, w_t, w_r, gain_q, gain_r)


def _na(qv_t, keys, layer, table):
    t = qv_t.shape[1]
    feat = pl.BlockSpec((NA_W, SEQ), lambda b: (0, b))
    return pl.pallas_call(
        _na_kernel,
        out_shape=jax.ShapeDtypeStruct((NA_W, t), BF16),
        grid=(t // SEQ,),
        in_specs=[feat, pl.BlockSpec((SEQ, NA_W), lambda b: (b, 0)),
                  pl.BlockSpec((NA_W, SEQ), lambda b: (1, b)),
                  _resident((NA_HEADS, NA_ROW_REL + 1, GRID_W, 2 * GRID_W), layer)],
        out_specs=feat,
        scratch_shapes=[pltpu.VMEM((NA_HEADS, NA_VARIANTS, NA_WIN, QBT), F32)],
        compiler_params=_params(("arbitrary",)),
        name="na",
    )(qv_t, keys, qv_t, table)


def _swa(qv_t, keys, layer, slopes, sink):
    t = qv_t.shape[1]
    return pl.pallas_call(
        functools.partial(_swa_kernel, layer),
        out_shape=jax.ShapeDtypeStruct((SWA_QW, t), BF16),
        grid=(t // SEQ,),
        in_specs=[_smem(), _smem(),
                  pl.BlockSpec((SWA_QW, SEQ), lambda b: (0, b)),
                  pl.BlockSpec((SEQ, SWA_KVW), lambda b: (b, NA_W // SWA_KVW)),
                  pl.BlockSpec((SWA_KVW, SEQ), lambda b: (SWA_QW // SWA_KVW, b))],
        out_specs=pl.BlockSpec((SWA_QW, SEQ), lambda b: (0, b)),
        scratch_shapes=[pltpu.VMEM((SWA_Q_HEADS, SWA_WIN, QBT), F32)],
        compiler_params=_params(("arbitrary",)),
        name="swa",
    )(slopes, sink, qv_t, keys, qv_t)


def _dil(qkv, qkv4, qkv16, slopes):
    t = qkv.shape[0]
    pairs = DIL_W // LANES
    seq_block = lambda first: pl.BlockSpec((SEQ, LANES), lambda b, p: (b, first + p))

    def phased(stride, first):
        return pl.BlockSpec((1, stride, SEQ // stride, LANES), lambda b, p: (b, 0, 0, first + p))

    operands, specs = [], []
    for arr, spec in ((qkv, seq_block), (qkv4, lambda f: phased(DIL_STRIDES[1], f)),
                      (qkv16, lambda f: phased(DIL_STRIDES[2], f))):
        for part in range(3):
            operands.append(arr)
            specs.append(spec(part * pairs))
    return pl.pallas_call(
        _dil_kernel,
        out_shape=jax.ShapeDtypeStruct((t, DIL_W), BF16),
        grid=(t // SEQ, pairs),
        in_specs=[_smem()] + specs,
        out_specs=seq_block(0),
        scratch_shapes=[pltpu.VMEM((pairs, 2, 3, 2 * DIL_QB, DIL_WIN), F32),
                        pltpu.VMEM((pairs, 2 * QB, QB), F32),
                        pltpu.VMEM((3, SEQ, LANES), F32),
                        pltpu.VMEM((3, SEQ, LANES), F32),
                        pltpu.VMEM((SEQ, LANES), F32)],
        compiler_params=_params(("arbitrary", "arbitrary")),
        name="dil",
    )(slopes, *operands)


def _out_ffn(ya_t, yb_t, yc, x, layer, gain_ab, gain_c, w_out, gain_ffn, w_gu, w_down):
    t = x.shape[0]
    feat = lambda n: pl.BlockSpec((n, ROW_TILE), lambda i: (0, i))
    tok = lambda n: pl.BlockSpec((ROW_TILE, n), lambda i: (i, 0))
    return pl.pallas_call(
        _out_ffn_kernel,
        out_shape=jax.ShapeDtypeStruct((t, D_MODEL), F32),
        grid=(t // ROW_TILE,),
        in_specs=[feat(NA_W), feat(SWA_QW), tok(DIL_W), tok(D_MODEL),
                  _resident((NA_W + SWA_QW, 1), layer), _resident((1, DIL_W), layer),
                  _resident((MIX_W, D_MODEL), layer), _resident((1, D_MODEL), layer),
                  _resident((D_MODEL, 2 * FFN_HIDDEN), layer), _resident((FFN_HIDDEN, D_MODEL), layer)],
        out_specs=tok(D_MODEL),
        scratch_shapes=[pltpu.VMEM((ROW_TILE, D_MODEL), BF16)],
        compiler_params=_params(("parallel",)),
        name="out_ffn",
    )(ya_t, yb_t, yc, x, gain_ab, gain_c, w_out, gain_ffn, w_gu, w_down)


def _split_w_in(w):
    cuts = np.cumsum([NA_W, NA_W, NA_W, SWA_QW, SWA_KVW, SWA_KVW, DIL_W, DIL_W])
    qa, ka, va, qb, kb, vb, qc, kc, vc = jnp.split(w, [int(c) for c in cuts], axis=2)
    w_t = jnp.concatenate([qa, qb, va, vb], axis=2).astype(BF16).transpose(0, 2, 1)
    w_r = jnp.concatenate([ka, kb, qc, kc, vc], axis=2).astype(BF16)
    return w_t, w_r


def kernel(x, attn_norm, w_in, qk_gain, rpb, sink, out_gain, w_out, ffn_norm, w_gu, w_down):
    batch, seq, d = x.shape
    assert (seq, d) == (SEQ, D_MODEL)
    depth = w_in.shape[0]
    n_slopes = SWA_Q_HEADS + DIL_HEADS
    slopes = 2.0 ** (-8.0 * (jnp.arange(n_slopes, dtype=F32) + 1.0) / n_slopes)
    xf = x.reshape(batch * seq, d)

    g = qk_gain.astype(F32)
    tile = lambda v, n: jnp.tile(v, (1, n))
    gain_q = jnp.concatenate([tile(g[:, 0, 0], NA_HEADS), tile(g[:, 1, 0], SWA_Q_HEADS)], axis=1)[:, :, None]
    gain_r = jnp.concatenate([tile(g[:, 0, 1], NA_HEADS), tile(g[:, 1, 1], SWA_KV_HEADS),
                              tile(g[:, 2, 0], DIL_HEADS), tile(g[:, 2, 1], DIL_HEADS)], axis=1)[:, None, :]
    w_t, w_r = _split_w_in(w_in)
    table = _na_pair_table(rpb.reshape((depth * NA_HEADS,) + rpb.shape[2:]))
    table = table.reshape((depth, NA_HEADS) + table.shape[1:])
    og = out_gain.astype(F32)
    gain_ab, gain_c = og[:, :NA_W + SWA_QW, None], og[:, None, NA_W + SWA_QW:]
    w_out_b, w_gu_b, w_down_b = w_out.astype(BF16), w_gu.astype(BF16), w_down.astype(BF16)
    attn_gain, ffn_gain = attn_norm.astype(F32)[:, None, :], ffn_norm.astype(F32)[:, None, :]
    sink = sink.astype(F32)

    for l in range(depth):
        qva_t, qvb_t, keys, dil1, dil4, dil16 = _proj(xf, l, attn_gain, w_t, w_r, gain_q, gain_r)
        ya_t = _na(qva_t, keys, l, table)
        yb_t = _swa(qvb_t, keys, l, slopes, sink)
        yc = _dil(dil1, dil4, dil16, slopes)
        xf = _out_ffn(ya_t, yb_t, yc, xf, l, gain_ab, gain_c, w_out_b, ffn_gain, w_gu_b, w_down_b)
    return xf.reshape(batch, seq, d)
```

```python
import functools
import math

import numpy as np
import jax
import jax.numpy as jnp
from jax import lax
from jax.experimental import pallas as pl
from jax.experimental.pallas import tpu as pltpu

D_MODEL = 1024
SEQ = 2048
HEAD_DIM = 64
GRID_W = 64
NA_HEADS = 4
NA_ROWS = 8
NA_COLS = 16
SWA_Q_HEADS = 6
SWA_KV_HEADS = 2
SWA_GROUP = SWA_Q_HEADS // SWA_KV_HEADS
SWA_HALF_WINDOW = 128
DIL_HEADS = 6
DIL_STRIDES = (1, 4, 16)
DIL_HALF = 64
NA_W = NA_HEADS * HEAD_DIM
SWA_QW = SWA_Q_HEADS * HEAD_DIM
SWA_KVW = SWA_KV_HEADS * HEAD_DIM
DIL_W = DIL_HEADS * HEAD_DIM
IN_W = 3 * NA_W + SWA_QW + 2 * SWA_KVW + 3 * DIL_W
MIX_W = NA_W + SWA_QW + DIL_W
FFN_HIDDEN = 2816
NORM_EPS = 1e-6
NEG_INF = -1e30
QK_SCALE = HEAD_DIM ** -0.5
LOG2E = math.log2(math.e)
Q_SCALE = QK_SCALE * LOG2E

LANES = 128
VMEM_LIMIT = 56 * 1024 * 1024

QB = 128
QBT = 256
PREP_ROWS = 256
ROW_TILE = 1024
PROJ_ROW_TILE = 1024
MXU_DIM = 256
FFN_H_CHUNKS = (3 * MXU_DIM,) * 3 + (2 * MXU_DIM,)
assert sum(FFN_H_CHUNKS) == FFN_HIDDEN
NA_LAG = 2
DIL_LAG = 1

PROJ_T_Q = NA_W + SWA_QW
PROJ_T_V = NA_W + SWA_KVW
PROJ_R_K = NA_W + SWA_KVW
PROJ_R = PROJ_R_K + 3 * DIL_W
PROJ_R_CHUNK = 768
REGROUP_ROWS = 16

NA_WIN = 768
NA_VARIANTS = 3
NA_LAST_START = SEQ - NA_WIN
SWA_WIN = QBT + 2 * SWA_HALF_WINDOW

F32 = jnp.float32
BF16 = jnp.bfloat16


def _dot(a, b):
    return jnp.dot(a, b, preferred_element_type=F32)


def _dot_nt(a, b):
    return lax.dot_general(a, b, (((1,), (1,)), ((), ())), preferred_element_type=F32)


def _dot_tn(a, b):
    return lax.dot_general(a, b, (((0,), (0,)), ((), ())), preferred_element_type=F32)


def _low_half(shape):
    return lax.broadcasted_iota(jnp.int32, shape, len(shape) - 1) < HEAD_DIM


def _pair_rms(x, gain):
    lo = _low_half(x.shape)
    sq = x * x
    s_lo = jnp.sum(jnp.where(lo, sq, 0.0), axis=-1, keepdims=True)
    s_hi = jnp.sum(jnp.where(lo, 0.0, sq), axis=-1, keepdims=True)
    ms = jnp.where(lo, s_lo, s_hi) * (1.0 / HEAD_DIM)
    return x * lax.rsqrt(ms + NORM_EPS) * gain


def _softmax_block(s):
    m = jnp.max(s, axis=-1, keepdims=True)
    return m, jnp.exp2(s - m).astype(BF16)


def _proj_kernel(x_ref, g_ref, wt_ref, wr_ref, gq_ref, gr_ref,
                 qva_ref, qvb_ref, k_ref, dil_ref, dil4_ref, dil16_ref, regroup_scr):
    x = x_ref[...]
    ms = jnp.mean(x * x, axis=-1, keepdims=True)
    h = (x * lax.rsqrt(ms + NORM_EPS) * g_ref[...]).astype(BF16)
    tm = h.shape[0]

    def finish_q(qt):
        q3 = qt.reshape(PROJ_T_Q // HEAD_DIM, HEAD_DIM, tm)
        q3 = q3 * lax.rsqrt(jnp.mean(q3 * q3, axis=1, keepdims=True) + NORM_EPS)
        qt = q3.reshape(PROJ_T_Q, tm) * gq_ref[...] * Q_SCALE
        qva_ref[0:NA_W, :] = qt[0:NA_W].astype(BF16)
        qvb_ref[0:SWA_QW, :] = qt[NA_W:PROJ_T_Q].astype(BF16)

    def finish_v(vt):
        qva_ref[NA_W:2 * NA_W, :] = vt[0:NA_W].astype(BF16)
        qvb_ref[SWA_QW:SWA_QW + SWA_KVW, :] = vt[NA_W:PROJ_T_V].astype(BF16)

    n_norm = (PROJ_R_K + 2 * DIL_W) // LANES
    n_key = PROJ_R_K // LANES
    n_dil_q = DIL_W // LANES

    def finish_r(c, r):
        fine, coarse = DIL_STRIDES[1], DIL_STRIDES[2] // DIL_STRIDES[1]
        per_fine = tm // fine
        dil_tiles = []
        for t in range(PROJ_R_CHUNK // LANES):
            blk = c * (PROJ_R_CHUNK // LANES) + t
            tile = r[:, t * LANES:(t + 1) * LANES]
            if blk < n_norm:
                tile = _pair_rms(tile, gr_ref[:, blk * LANES:(blk + 1) * LANES])
            if n_key <= blk < n_key + n_dil_q:
                tile = tile * Q_SCALE
            if blk < n_key:
                k_ref[:, blk * LANES:(blk + 1) * LANES] = tile.astype(BF16)
                continue
            d = blk - n_key
            dil_ref[:, d * LANES:(d + 1) * LANES] = tile.astype(BF16)
            regroup_scr[d, 0] = tile
            dil_tiles.append(d)
        for d in dil_tiles:
            cols = slice(d * LANES, (d + 1) * LANES)
            for piece in range(per_fine // REGROUP_ROWS):
                rows = slice(piece * REGROUP_ROWS, (piece + 1) * REGROUP_ROWS)
                for ph in range(fine):
                    part = regroup_scr[d, 0, pl.ds(piece * REGROUP_ROWS * fine + ph, REGROUP_ROWS, stride=fine), :]
                    dil4_ref[0, ph, rows, cols] = part.astype(BF16)
                    regroup_scr[d, 1, ph * per_fine + piece * REGROUP_ROWS:
                                ph * per_fine + (piece + 1) * REGROUP_ROWS, :] = part
        for d in dil_tiles:
            cols = slice(d * LANES, (d + 1) * LANES)
            for ph in range(fine):
                for piece in range(per_fine // coarse // REGROUP_ROWS):
                    rows = slice(piece * REGROUP_ROWS, (piece + 1) * REGROUP_ROWS)
                    for sub_ph in range(coarse):
                        start = ph * per_fine + piece * REGROUP_ROWS * coarse + sub_ph
                        part = regroup_scr[d, 1, pl.ds(start, REGROUP_ROWS, stride=coarse), :]
                        dil16_ref[0, ph + fine * sub_ph, rows, cols] = part.astype(BF16)

    finish_q(_dot_nt(wt_ref[0:PROJ_T_Q, :], h))
    finish_v(_dot_nt(wt_ref[PROJ_T_Q:PROJ_T_Q + PROJ_T_V, :], h))
    for c in range(PROJ_R // PROJ_R_CHUNK):
        finish_r(c, _dot(h, wr_ref[:, c * PROJ_R_CHUNK:(c + 1) * PROJ_R_CHUNK]))


def _group_rms_t(yt, gain_col):
    yf = yt.astype(F32)
    ms = jnp.mean(yf * yf, axis=0, keepdims=True)
    return (yf * lax.rsqrt(ms + NORM_EPS) * gain_col).astype(BF16)


def _out_ffn_kernel(ya_ref, yb_ref, yc_ref, x_ref, gab_ref, gc_ref, wo_ref, gf_ref, wgu_ref, wd_ref,
                    o_ref, h_scr):
    ya = _group_rms_t(ya_ref[...], gab_ref[0:NA_W, :])
    yb = _group_rms_t(yb_ref[...], gab_ref[NA_W:NA_W + SWA_QW, :])
    yc = yc_ref[...].astype(F32)
    ms = jnp.mean(yc * yc, axis=-1, keepdims=True)
    yc = (yc * lax.rsqrt(ms + NORM_EPS) * gc_ref[...]).astype(BF16)
    acc = _dot_tn(ya, wo_ref[0:NA_W, :])
    acc = acc + _dot_tn(yb, wo_ref[NA_W:NA_W + SWA_QW, :])
    acc = acc + _dot(yc, wo_ref[NA_W + SWA_QW:MIX_W, :])
    x = x_ref[...] + acc

    ms = jnp.mean(x * x, axis=-1, keepdims=True)
    h_scr[...] = (x * lax.rsqrt(ms + NORM_EPS) * gf_ref[...]).astype(BF16)
    o_ref[...] = x
    lo = 0
    for width in FFN_H_CHUNKS:
        gate = _dot(h_scr[...], wgu_ref[:, lo:lo + width])
        up = _dot(h_scr[...], wgu_ref[:, FFN_HIDDEN + lo:FFN_HIDDEN + lo + width])
        act = (gate / (1.0 + jnp.exp(-gate)) * up).astype(BF16)
        o_ref[...] += _dot(act, wd_ref[lo:lo + width, :])
        lo += width


def _pipeline3(chains, stage_a, stage_b, stage_c, lag=1):
    n = len(chains)
    a, b = {}, {}
    for i in range(-2 * lag, n):
        if 0 <= i + 2 * lag < n:
            a[i + 2 * lag] = stage_a(chains[i + 2 * lag])
        if 0 <= i + lag < n:
            b[i + lag] = stage_b(chains[i + lag], a.pop(i + lag))
        if i >= 0:
            stage_c(chains[i], b.pop(i))


def _attend_t(chains, lag=1):
    def scores(c):
        qt = c["q"]()
        zeros = jnp.zeros_like(qt)
        qpad = jnp.concatenate([qt, zeros] if c["half"] == 0 else [zeros, qt], axis=0)
        kw = c["k"]()
        mid = kw.shape[0] // 2
        s = jnp.concatenate([_dot(kw[:mid], qpad), _dot(kw[mid:], qpad)], axis=0)
        return s + c["bias"]()

    def softmax(c, s):
        m = jnp.max(s, axis=0, keepdims=True)
        if c["sink"] is not None:
            m = jnp.maximum(m, c["sink"])
        return jnp.exp2(s - m).astype(BF16), m

    def values(c, pm):
        p, m = pm
        vt = c["v"]()
        own = vt[c["half"] * HEAD_DIM:(c["half"] + 1) * HEAD_DIM, :]
        ones = jnp.ones_like(own)
        v_aug = jnp.concatenate([own, ones], axis=0)
        if c["split_values"]:
            cut = max(MXU_DIM, (p.shape[0] // 2) // MXU_DIM * MXU_DIM)
            o = _dot(v_aug[:, :cut], p[:cut]) + _dot(v_aug[:, cut:], p[cut:])
        else:
            o = _dot(v_aug, p)
        den = o[HEAD_DIM:2 * HEAD_DIM, :]
        if c["sink"] is not None:
            den = den + jnp.exp2(c["sink"] - m)
        c["store"](o[0:HEAD_DIM, :] / den)

    _pipeline3(chains, scores, softmax, values, lag=lag)


NA_KEY_ROWS = NA_WIN // GRID_W
NA_Q_ROWS = QBT // GRID_W
NA_ROW_REL = 2 * NA_ROWS - 1


def _na_geometry():
    n_blocks = SEQ // QBT
    out = []
    for variant, block in enumerate((0, 1, n_blocks - 1)):
        start = min(max(block * QBT - QBT, 0), NA_LAST_START)
        assert (block * QBT - start) // QBT == variant
        out.append((start // GRID_W, block * NA_Q_ROWS))
    return out


def _na_kernel(q_ref, k_ref, v_ref, pair_ref, o_ref, tbl_scr):
    @pl.when(pl.program_id(0) == 0)
    def _():
        grid_rows = SEQ // GRID_W
        lo = _low_half((GRID_W, LANES))
        masked = jnp.full((GRID_W, LANES), NEG_INF, F32)
        for variant, (key_row0, query_row0) in enumerate(_na_geometry()):
            for kr in range(NA_KEY_ROWS):
                for pair in range(NA_Q_ROWS // 2):
                    krow = key_row0 + kr
                    rel, ok = [], []
                    for qrow in (query_row0 + 2 * pair, query_row0 + 2 * pair + 1):
                        first = min(max(qrow - NA_ROWS // 2, 0), grid_rows - NA_ROWS)
                        ok.append(first <= krow < first + NA_ROWS)
                        rel.append(krow - qrow + NA_ROWS - 1)
                    for h in range(NA_HEADS):
                        if not (ok[0] or ok[1]):
                            tile = masked
                        else:
                            tile = pair_ref[h, rel[0]]
                            if not ok[1]:
                                tile = jnp.where(lo, tile, masked)
                            elif not ok[0]:
                                tile = jnp.where(lo, masked, tile)
                        tbl_scr[h, variant, kr * GRID_W:(kr + 1) * GRID_W, pair * LANES:(pair + 1) * LANES] = tile

    def store(rows, cols):
        def put(o):
            o_ref[rows, cols] = o.astype(BF16)
        return put

    chains = []
    for j in range(SEQ // QBT):
        q0 = j * QBT
        start = min(max(q0 - QBT, 0), NA_LAST_START)
        variant = (q0 - start) // QBT
        cols = slice(q0, q0 + QBT)
        keys = slice(start, start + NA_WIN)
        for h in range(NA_HEADS):
            rows = slice(h * HEAD_DIM, (h + 1) * HEAD_DIM)
            pair = slice((h // 2) * LANES, (h // 2 + 1) * LANES)
            chains.append(dict(
                half=h % 2, sink=None, split_values=False,
                q=lambda rows=rows, cols=cols: q_ref[rows, cols],
                k=lambda keys=keys, pair=pair: k_ref[keys, pair],
                v=lambda keys=keys, pair=pair: v_ref[pair, keys],
                bias=lambda h=h, variant=variant: tbl_scr[h, variant],
                store=store(rows, cols)))
    _attend_t(chains, lag=NA_LAG)


def _na_pair_table(rpb):
    cols = np.arange(GRID_W)
    col_start = np.clip(cols - NA_COLS // 2, 0, GRID_W - NA_COLS)
    col_ok = (cols[:, None] >= col_start[None, :]) & (cols[:, None] < col_start[None, :] + NA_COLS)
    heads, n_row_rel, n_col_rel = rpb.shape
    assert n_row_rel == NA_ROW_REL
    period = 2 * GRID_W - 1
    lead = GRID_W - NA_COLS
    seq = jnp.pad(rpb.astype(F32)[:, :, ::-1], ((0, 0), (0, 0), (lead, period - lead - n_col_rel)))
    skew = jnp.tile(seq, (1, 1, GRID_W + 1))[..., :GRID_W * 2 * GRID_W]
    toe = skew.reshape(heads, n_row_rel, GRID_W, 2 * GRID_W)[:, :, ::-1, :GRID_W]
    toe = jnp.where(col_ok[None, None], toe * LOG2E, NEG_INF)
    toe = jnp.pad(toe, ((0, 0), (1, 1), (0, 0), (0, 0)), constant_values=NEG_INF)
    return jnp.concatenate([toe[:, 1:], toe[:, :-1]], axis=-1)


def _swa_kernel(layer, slopes_ref, sink_ref, q_ref, k_ref, v_ref, o_ref, bias_scr):
    @pl.when(pl.program_id(0) == 0)
    def _():
        kk = lax.broadcasted_iota(jnp.int32, (SWA_WIN, QBT), 0)
        qq = lax.broadcasted_iota(jnp.int32, (SWA_WIN, QBT), 1)
        dist = jnp.abs(kk - SWA_HALF_WINDOW - qq)
        in_window = dist <= SWA_HALF_WINDOW
        dist_f = dist.astype(F32)
        for h in range(SWA_Q_HEADS):
            bias_scr[h] = jnp.where(in_window, -slopes_ref[h] * dist_f * LOG2E, NEG_INF)

    def store(rows, cols):
        def put(o):
            o_ref[rows, cols] = o.astype(BF16)
        return put

    chains = []
    for j in range(SEQ // QBT):
        q0 = j * QBT
        lo = max(q0 - SWA_HALF_WINDOW, 0)
        hi = min(q0 + QBT + SWA_HALF_WINDOW, SEQ)
        first = lo - (q0 - SWA_HALF_WINDOW)
        cols = slice(q0, q0 + QBT)
        keys = slice(lo, hi)
        win = slice(first, first + hi - lo)
        for h in range(SWA_Q_HEADS):
            rows = slice(h * HEAD_DIM, (h + 1) * HEAD_DIM)
            chains.append(dict(
                half=h // SWA_GROUP, sink=sink_ref[layer, h] * LOG2E, split_values=True,
                q=lambda rows=rows, cols=cols: q_ref[rows, cols],
                k=lambda keys=keys: k_ref[keys, :],
                v=lambda keys=keys: v_ref[:, keys],
                bias=lambda h=h, win=win: bias_scr[h, win, :],
                store=store(rows, cols)))
    _attend_t(chains)


DIL_QB = 128
DIL_WIN = DIL_QB + 2 * DIL_HALF


def _dil_kernel(slopes_ref, q1_ref, k1_ref, v1_ref, q4_ref, k4_ref, v4_ref, q16_ref, k16_ref, v16_ref,
                o_ref, bias_scr, bias16_scr, o_scr, lse_scr, out_scr):
    pair = pl.program_id(1)

    @pl.when(pl.program_id(0) == 0)
    def _():
        slope = [slopes_ref[SWA_Q_HEADS + 2 * pair + h] for h in range(2)]
        qq = lax.broadcasted_iota(jnp.int32, (DIL_QB, DIL_WIN), 0)
        kk = lax.broadcasted_iota(jnp.int32, (DIL_QB, DIL_WIN), 1)
        for shift in range(3):
            dist = jnp.abs(kk - shift * DIL_HALF - qq)
            ok = dist <= DIL_HALF
            dist_f = dist.astype(F32)
            for h in range(2):
                for b, stride in enumerate(DIL_STRIDES[:2]):
                    bias_scr[pair, b, shift, h * DIL_QB:(h + 1) * DIL_QB, :] = jnp.where(
                        ok, -slope[h] * (dist_f * float(stride)) * LOG2E, NEG_INF)
        qq = lax.broadcasted_iota(jnp.int32, (QB, QB), 0)
        kk = lax.broadcasted_iota(jnp.int32, (QB, QB), 1)
        dist = jnp.abs(kk - qq)
        for h in range(2):
            bias16_scr[pair, h * QB:(h + 1) * QB, :] = jnp.where(
                dist <= DIL_HALF, -slope[h] * (dist.astype(F32) * float(DIL_STRIDES[2])) * LOG2E, NEG_INF)

    chains = []
    for b, stride in enumerate(DIL_STRIDES):
        sub = SEQ // stride
        qb = QB if sub == QB else DIL_QB
        for start in range(0, SEQ, qb):
            phase, local_q = divmod(start, sub)
            if sub == QB:
                local_k, n_keys, shift = 0, QB, None
            else:
                local_k = min(max(local_q - DIL_HALF, 0), sub - DIL_WIN)
                n_keys, shift = DIL_WIN, (local_q - local_k) // DIL_HALF
            qrows, krows = slice(local_q, local_q + qb), slice(local_k, local_k + n_keys)
            fine = DIL_STRIDES[1]
            if stride == 1:
                refs = (q1_ref, k1_ref, v1_ref)
                load = lambda ref, rows: ref[rows, :]
                dst = qrows
            elif stride == fine:
                refs = (q4_ref, k4_ref, v4_ref)
                load = lambda ref, rows, phase=phase: ref[0, phase, rows, :]
                dst = slice(start, start + qb)
            else:
                refs = (q16_ref, k16_ref, v16_ref)
                load = lambda ref, rows, phase=phase: ref[0, phase, rows, :]
                coarse = stride // fine
                dst = pl.ds((phase % fine) * (SEQ // fine) + local_q * coarse + phase // fine, qb, stride=coarse)
            chains.append(dict(b=b, shift=shift, dst=dst, qb=qb, order=start + qb,
                               q=lambda refs=refs, load=load, rows=qrows: load(refs[0], rows),
                               k=lambda refs=refs, load=load, rows=krows: load(refs[1], rows),
                               v=lambda refs=refs, load=load, rows=krows: load(refs[2], rows)))
    chains.sort(key=lambda c: c["order"])

    def scores(c):
        q, kw = c["q"](), c["k"]()
        lo = _low_half(q.shape)
        zeros = jnp.zeros_like(q)
        q_both = jnp.concatenate([jnp.where(lo, q, zeros), jnp.where(lo, zeros, q)], axis=0)
        bias = bias16_scr[pair] if c["shift"] is None else bias_scr[pair, c["b"], c["shift"]]
        return _dot_nt(q_both, kw) + bias

    def softmax(c, s):
        return _softmax_block(s)

    def values(c, mp):
        m, p = mp
        vw = c["v"]()
        res = _dot(p, jnp.concatenate([vw, jnp.ones_like(vw)], axis=1))
        qb = c["qb"]
        lo = _low_half((qb, LANES))
        den = jnp.where(lo, res[0:qb, LANES:2 * LANES], res[qb:2 * qb, LANES:2 * LANES])
        o_scr[c["b"], c["dst"], :] = jnp.where(lo, res[0:qb, 0:LANES], res[qb:2 * qb, 0:LANES]) / den
        lse_scr[c["b"], c["dst"], :] = jnp.where(lo, m[0:qb], m[qb:2 * qb]) + jnp.log2(den)

    _pipeline3(chains, scores, softmax, values, lag=DIL_LAG)

    fine = DIL_STRIDES[1]
    per_phase = SEQ // fine

    def merge(i, carry):
        start = pl.multiple_of(i * PREP_ROWS, PREP_ROWS)
        rows = pl.ds(start, PREP_ROWS)
        natural = pl.ds((start % per_phase) * fine + start // per_phase, PREP_ROWS, stride=fine)
        lse = [lse_scr[0, natural, :], lse_scr[1, rows, :], lse_scr[2, rows, :]]
        outs = [o_scr[0, natural, :], o_scr[1, rows, :], o_scr[2, rows, :]]
        top = jnp.maximum(jnp.maximum(lse[0], lse[1]), lse[2])
        num = jnp.zeros((PREP_ROWS, LANES), F32)
        den = jnp.zeros((PREP_ROWS, LANES), F32)
        for b in range(3):
            w = jnp.exp2(lse[b] - top)
            num = num + w * outs[b]
            den = den + w
        out_scr[natural, :] = num / den
        return carry

    lax.fori_loop(0, SEQ // PREP_ROWS, merge, 0)

    def emit(i, carry):
        rows = pl.ds(pl.multiple_of(i * PREP_ROWS, PREP_ROWS), PREP_ROWS)
        o_ref[rows, :] = out_scr[rows, :].astype(BF16)
        return carry

    lax.fori_loop(0, SEQ // PREP_ROWS, emit, 0)


def _params(semantics):
    return pltpu.CompilerParams(dimension_semantics=semantics, vmem_limit_bytes=VMEM_LIMIT)


def _resident(shape, layer):
    return pl.BlockSpec((None,) + tuple(shape), lambda *_: (layer,) + (0,) * len(shape),
                        pipeline_mode=pl.Buffered(1))


def _smem():
    return pl.BlockSpec(memory_space=pltpu.SMEM)


def _proj(x, layer, gain, w_t, w_r, gain_q, gain_r):
    t = x.shape[0]
    rows = PROJ_ROW_TILE
    feat = lambda n: pl.BlockSpec((n, rows), lambda i: (0, i))
    tok = lambda n: pl.BlockSpec((rows, n), lambda i: (i, 0))
    tiles_per_seq = SEQ // rows

    def phased(stride):
        return pl.BlockSpec((1, stride, rows // stride, 3 * DIL_W),
                            lambda i: (i // tiles_per_seq, 0, i % tiles_per_seq, 0))

    def phased_shape(stride):
        return jax.ShapeDtypeStruct((t // SEQ, stride, SEQ // stride, 3 * DIL_W), BF16)

    return pl.pallas_call(
        _proj_kernel,
        out_shape=[jax.ShapeDtypeStruct((2 * NA_W, t), BF16), jax.ShapeDtypeStruct((SWA_QW + SWA_KVW, t), BF16),
                   jax.ShapeDtypeStruct((t, PROJ_R_K), BF16), jax.ShapeDtypeStruct((t, 3 * DIL_W), BF16),
                   phased_shape(DIL_STRIDES[1]), phased_shape(DIL_STRIDES[2])],
        grid=(t // rows,),
        in_specs=[tok(D_MODEL), _resident((1, D_MODEL), layer),
                  _resident((PROJ_T_Q + PROJ_T_V, D_MODEL), layer), _resident((D_MODEL, PROJ_R), layer),
                  _resident((PROJ_T_Q, 1), layer), _resident((1, PROJ_R_K + 2 * DIL_W), layer)],
        out_specs=[feat(2 * NA_W), feat(SWA_QW + SWA_KVW), tok(PROJ_R_K), tok(3 * DIL_W),
                   phased(DIL_STRIDES[1]), phased(DIL_STRIDES[2])],
        scratch_shapes=[pltpu.VMEM((3 * DIL_W // LANES, 2, rows, LANES), F32)],
        compiler_params=_params(("parallel",)),
        name="proj",
    )(x, gain, w_t, w_r, gain_q, gain_r)


def _na(qv_t, keys, layer, table):
    t = qv_t.shape[1]
    feat = pl.BlockSpec((NA_W, SEQ), lambda b: (0, b))
    return pl.pallas_call(
        _na_kernel,
        out_shape=jax.ShapeDtypeStruct((NA_W, t), BF16),
        grid=(t // SEQ,),
        in_specs=[feat, pl.BlockSpec((SEQ, NA_W), lambda b: (b, 0)),
                  pl.BlockSpec((NA_W, SEQ), lambda b: (1, b)),
                  _resident((NA_HEADS, NA_ROW_REL + 1, GRID_W, 2 * GRID_W), layer)],
        out_specs=feat,
        scratch_shapes=[pltpu.VMEM((NA_HEADS, NA_VARIANTS, NA_WIN, QBT), F32)],
        compiler_params=_params(("arbitrary",)),
        name="na",
    )(qv_t, keys, qv_t, table)


def _swa(qv_t, keys, layer, slopes, sink):
    t = qv_t.shape[1]
    return pl.pallas_call(
        functools.partial(_swa_kernel, layer),
        out_shape=jax.ShapeDtypeStruct((SWA_QW, t), BF16),
        grid=(t // SEQ,),
        in_specs=[_smem(), _smem(),
                  pl.BlockSpec((SWA_QW, SEQ), lambda b: (0, b)),
                  pl.BlockSpec((SEQ, SWA_KVW), lambda b: (b, NA_W // SWA_KVW)),
                  pl.BlockSpec((SWA_KVW, SEQ), lambda b: (SWA_QW // SWA_KVW, b))],
        out_specs=pl.BlockSpec((SWA_QW, SEQ), lambda b: (0, b)),
        scratch_shapes=[pltpu.VMEM((SWA_Q_HEADS, SWA_WIN, QBT), F32)],
        compiler_params=_params(("arbitrary",)),
        name="swa",
    )(slopes, sink, qv_t, keys, qv_t)


def _dil(qkv, qkv4, qkv16, slopes):
    t = qkv.shape[0]
    pairs = DIL_W // LANES
    seq_block = lambda first: pl.BlockSpec((SEQ, LANES), lambda b, p: (b, first + p))

    def phased(stride, first):
        return pl.BlockSpec((1, stride, SEQ // stride, LANES), lambda b, p: (b, 0, 0, first + p))

    operands, specs = [], []
    for arr, spec in ((qkv, seq_block), (qkv4, lambda f: phased(DIL_STRIDES[1], f)),
                      (qkv16, lambda f: phased(DIL_STRIDES[2], f))):
        for part in range(3):
            operands.append(arr)
            specs.append(spec(part * pairs))
    return pl.pallas_call(
        _dil_kernel,
        out_shape=jax.ShapeDtypeStruct((t, DIL_W), BF16),
        grid=(t // SEQ, pairs),
        in_specs=[_smem()] + specs,
        out_specs=seq_block(0),
        scratch_shapes=[pltpu.VMEM((pairs, 2, 3, 2 * DIL_QB, DIL_WIN), F32),
                        pltpu.VMEM((pairs, 2 * QB, QB), F32),
                        pltpu.VMEM((3, SEQ, LANES), F32),
                        pltpu.VMEM((3, SEQ, LANES), F32),
                        pltpu.VMEM((SEQ, LANES), F32)],
        compiler_params=_params(("arbitrary", "arbitrary")),
        name="dil",
    )(slopes, *operands)


def _out_ffn(ya_t, yb_t, yc, x, layer, gain_ab, gain_c, w_out, gain_ffn, w_gu, w_down):
    t = x.shape[0]
    feat = lambda n: pl.BlockSpec((n, ROW_TILE), lambda i: (0, i))
    tok = lambda n: pl.BlockSpec((ROW_TILE, n), lambda i: (i, 0))
    return pl.pallas_call(
        _out_ffn_kernel,
        out_shape=jax.ShapeDtypeStruct((t, D_MODEL), F32),
        grid=(t // ROW_TILE,),
        in_specs=[feat(NA_W), feat(SWA_QW), tok(DIL_W), tok(D_MODEL),
                  _resident((NA_W + SWA_QW, 1), layer), _resident((1, DIL_W), layer),
                  _resident((MIX_W, D_MODEL), layer), _resident((1, D_MODEL), layer),
                  _resident((D_MODEL, 2 * FFN_HIDDEN), layer), _resident((FFN_HIDDEN, D_MODEL), layer)],
        out_specs=tok(D_MODEL),
        scratch_shapes=[pltpu.VMEM((ROW_TILE, D_MODEL), BF16)],
        compiler_params=_params(("parallel",)),
        name="out_ffn",
    )(ya_t, yb_t, yc, x, gain_ab, gain_c, w_out, gain_ffn, w_gu, w_down)


def _split_w_in(w):
    cuts = np.cumsum([NA_W, NA_W, NA_W, SWA_QW, SWA_KVW, SWA_KVW, DIL_W, DIL_W])
    qa, ka, va, qb, kb, vb, qc, kc, vc = jnp.split(w, [int(c) for c in cuts], axis=2)
    w_t = jnp.concatenate([qa, qb, va, vb], axis=2).astype(BF16).transpose(0, 2, 1)
    w_r = jnp.concatenate([ka, kb, qc, kc, vc], axis=2).astype(BF16)
    return w_t, w_r


def kernel(x, attn_norm, w_in, qk_gain, rpb, sink, out_gain, w_out, ffn_norm, w_gu, w_down):
    batch, seq, d = x.shape
    assert (seq, d) == (SEQ, D_MODEL)
    depth = w_in.shape[0]
    n_slopes = SWA_Q_HEADS + DIL_HEADS
    slopes = 2.0 ** (-8.0 * (jnp.arange(n_slopes, dtype=F32) + 1.0) / n_slopes)
    xf = x.reshape(batch * seq, d)

    g = qk_gain.astype(F32)
    tile = lambda v, n: jnp.tile(v, (1, n))
    gain_q = jnp.concatenate([tile(g[:, 0, 0], NA_HEADS), tile(g[:, 1, 0], SWA_Q_HEADS)], axis=1)[:, :, None]
    gain_r = jnp.concatenate([tile(g[:, 0, 1], NA_HEADS), tile(g[:, 1, 1], SWA_KV_HEADS),
                              tile(g[:, 2, 0], DIL_HEADS), tile(g[:, 2, 1], DIL_HEADS)], axis=1)[:, None, :]
    w_t, w_r = _split_w_in(w_in)
    table = _na_pair_table(rpb.reshape((depth * NA_HEADS,) + rpb.shape[2:]))
    table = table.reshape((depth, NA_HEADS) + table.shape[1:])
    og = out_gain.astype(F32)
    gain_ab, gain_c = og[:, :NA_W + SWA_QW, None], og[:, None, NA_W + SWA_QW:]
    w_out_b, w_gu_b, w_down_b = w_out.astype(BF16), w_gu.astype(BF16), w_down.astype(BF16)
    attn_gain, ffn_gain = attn_norm.astype(F32)[:, None, :], ffn_norm.astype(F32)[:, None, :]
    sink = sink.astype(F32)

    for l in range(depth):
        qva_t, qvb_t, keys, dil1, dil4, dil16 = _proj(xf, l, attn_gain, w_t, w_r, gain_q, gain_r)
        ya_t = _na(qva_t, keys, l, table)
        yb_t = _swa(qvb_t, keys, l, slopes, sink)
        yc = _dil(dil1, dil4, dil16, slopes)
        xf = _out_ffn(ya_t, yb_t, yc, xf, l, gain_ab, gain_c, w_out_b, ffn_gain, w_gu_b, w_down_b)
    return xf.reshape(batch, seq, d)
```
